```python
import math
import jax, jax.numpy as jnp
from jax import lax
import numpy as np

D_MODEL = 1024
BATCH = 32
SEQ = 2048
DEPTH = 2

RET_HEADS = 4
RET_DK = 256
RET_DV = 512
RET_CHUNK = 128
ROPE_BASE = 10000.0
GN_EPS = 1e-5
NSA_HEADS = 16
NSA_KV_GROUPS = 2
NSA_HD = 64
CMP_LEN = 32
CMP_STRIDE = 16
CMP_HIDDEN = 256
SEL_BLOCK = 64
SEL_TOPK = 8
WINDOW = 512
NSA_Q_BLOCK = 64
FORCED_SCORE = 1e4
REL_BUCKETS = 32
REL_MAX_DIST = 128
FFN_HIDDEN = -(-8 * D_MODEL // (3 * 256)) * 256
EPS = 1e-6
NEG_INF = -1e30
SPLIT_SIZES = (RET_HEADS * RET_DK, RET_HEADS * RET_DK, RET_HEADS * RET_DV, RET_HEADS * RET_DV,
               NSA_HEADS * NSA_HD, 6 * NSA_KV_GROUPS * NSA_HD, 3 * NSA_HEADS, D_MODEL, D_MODEL)
C_IN = sum(SPLIT_SIZES)

kernel_name = "hybrid_retnet_nsa_gated_block"


def rms_norm(x, g):
    xf = x.astype(jnp.float32)
    y = xf * lax.rsqrt(jnp.mean(xf * xf, -1, keepdims=True) + EPS)
    return (y * g.astype(jnp.float32)).astype(x.dtype)


def masked_softmax(logits, mask):
    z = jnp.where(mask, logits.astype(jnp.float32), NEG_INF)
    z = z - jnp.max(z, -1, keepdims=True)
    e = jnp.exp(z) * mask
    return e / jnp.maximum(jnp.sum(e, -1, keepdims=True), 1e-30)


def t5_bucket(dist):
    dist = jnp.maximum(dist, 0)
    max_exact = REL_BUCKETS // 2
    large = max_exact + (jnp.log(jnp.maximum(dist, 1).astype(jnp.float32) / max_exact)
                         / math.log(REL_MAX_DIST / max_exact) * (REL_BUCKETS - max_exact)).astype(jnp.int32)
    large = jnp.minimum(large, REL_BUCKETS - 1)
    return jnp.where(dist < max_exact, dist, large)


def rotary(x, pos):
    half = x.shape[-1] // 2
    freqs = ROPE_BASE ** (-jnp.arange(half, dtype=jnp.float32) / half)
    ang = pos.astype(jnp.float32)[:, None] * freqs
    cos = jnp.cos(ang)[None, :, None, :]
    sin = jnp.sin(ang)[None, :, None, :]
    x1, x2 = x[..., :half], x[..., half:]
    return jnp.concatenate([x1 * cos - x2 * sin, x1 * sin + x2 * cos], -1)


def retention(q, k, v):
    B, T, H, dk = q.shape
    dv = v.shape[-1]
    nc = T // RET_CHUNK
    lg = jnp.log(1.0 - 2.0 ** (-5.0 - jnp.arange(H, dtype=jnp.float32)))
    n = jnp.arange(RET_CHUNK, dtype=jnp.float32)
    diff = n[:, None] - n[None, :]
    dmask = jnp.where(diff >= 0, jnp.exp(jnp.maximum(diff, 0.0)[None] * lg[:, None, None]), 0.0)
    xi = jnp.exp((n + 1.0)[None] * lg[:, None])
    zeta = jnp.exp((RET_CHUNK - 1.0 - n)[None] * lg[:, None])
    g_chunk = jnp.exp(RET_CHUNK * lg)

    def to_chunks(t):
        return t.reshape(B, nc, RET_CHUNK, H, t.shape[-1]).transpose(1, 0, 3, 2, 4)

    def step(R, qkv):
        qc, kc, vc = qkv
        s = jnp.einsum('bhnd,bhmd->bhnm', qc, kc) * dmask
        o = (jnp.einsum('bhnm,bhme->bhne', s, vc)
             + jnp.einsum('bhnd,bhde->bhne', qc, R) * xi[:, :, None])
        R = g_chunk[:, None, None] * R + jnp.einsum('bhmd,bhme->bhde', kc * zeta[:, :, None], vc)
        return R, o

    R0 = jnp.zeros((B, H, dk, dv), jnp.float32)
    _, o = lax.scan(step, R0, (to_chunks(q), to_chunks(k), to_chunks(v)))
    return o.transpose(1, 0, 3, 2, 4).reshape(B, T, H, dv)


def compress(kv, pos_emb, w1, w2):
    T = kv.shape[2]
    n_cmp = (T - CMP_LEN) // CMP_STRIDE + 1
    idx = jnp.arange(n_cmp)[:, None] * CMP_STRIDE + jnp.arange(CMP_LEN)[None]
    blocks = kv[:, :, idx, :] + pos_emb
    flat = blocks.reshape(blocks.shape[0], blocks.shape[1], n_cmp, CMP_LEN * NSA_HD)
    return jax.nn.gelu(flat @ w1) @ w2


def nsa(q, k_c, v_c, k_s, v_s, k_w, v_w, gates, pos_k, pos_v, w1_k, w2_k, w1_v, w2_v, rel_bias):
    B, T, H, hd = q.shape
    G = NSA_KV_GROUPS
    HG = H // G
    QB = NSA_Q_BLOCK
    scale = hd ** -0.5
    grp = lambda t: t.transpose(0, 2, 1, 3)
    kc = compress(grp(k_c), pos_k, w1_k, w2_k)
    vc = compress(grp(v_c), pos_v, w1_v, w2_v)
    n_cmp = kc.shape[2]
    cmp_end = jnp.arange(n_cmp) * CMP_STRIDE + CMP_LEN - 1
    n_sel = T // SEL_BLOCK
    sel_k = min(SEL_TOPK, n_sel)
    sel_start = jnp.arange(n_sel) * SEL_BLOCK
    overlap = ((cmp_end[:, None] - CMP_LEN + 1 < sel_start[None] + SEL_BLOCK)
               & (cmp_end[:, None] >= sel_start[None])).astype(jnp.float32)
    ks_blocks = grp(k_s).reshape(B, G, n_sel, SEL_BLOCK, hd)
    vs_blocks = grp(v_s).reshape(B, G, n_sel, SEL_BLOCK, hd)
    kw_pad = jnp.pad(grp(k_w), ((0, 0), (0, 0), (WINDOW, 0), (0, 0)))
    vw_pad = jnp.pad(grp(v_w), ((0, 0), (0, 0), (WINDOW, 0), (0, 0)))
    table = rel_bias.reshape(REL_BUCKETS, G, HG).astype(jnp.float32)
    nq = T // QB
    qb = q.reshape(B, nq, QB, G, HG, hd).transpose(1, 0, 3, 4, 2, 5)
    gb = gates.reshape(B, nq, QB, G, HG, 3).transpose(1, 0, 3, 4, 2, 5)
    starts = jnp.arange(nq, dtype=jnp.int32) * QB
    bidx = jnp.arange(B)[:, None, None, None]
    gidx = jnp.arange(G)[None, :, None, None]
    jj = jnp.arange(n_sel)

    def head_bias(dist):
        return jnp.moveaxis(table[t5_bucket(dist)], (-2, -1), (0, 1))

    def sel_bias(dist):
        b = jax.vmap(lambda tb, bk: tb[bk], in_axes=(1, 1), out_axes=1)(table, t5_bucket(dist))
        return jnp.moveaxis(b, -1, 2)

    def block(args):
        qx, gx, s = args
        pos = s + jnp.arange(QB, dtype=jnp.int32)
        dc = pos[:, None] - cmp_end[None]
        lc = jnp.einsum('bghqd,bgnd->bghqn', qx, kc) * scale + head_bias(dc)
        pc = masked_softmax(lc, dc >= 0)
        oc = jnp.einsum('bghqn,bgnd->bghqd', pc.astype(vc.dtype), vc)
        imp = jnp.einsum('bghqn,nj->bgqj', pc, overlap)
        cur = pos // SEL_BLOCK
        forced = (jj[None] == 0) | (jj[None] == cur[:, None]) | (jj[None] == cur[:, None] - 1)
        imp = jnp.where(forced, FORCED_SCORE, imp)
        imp = jnp.where(sel_start[None] <= pos[:, None], imp, NEG_INF)
        _, sel = lax.top_k(imp, sel_k)
        ksel = ks_blocks[bidx, gidx, sel]
        vsel = vs_blocks[bidx, gidx, sel]
        kpos = sel[..., None] * SEL_BLOCK + jnp.arange(SEL_BLOCK)
        ds = pos[:, None, None] - kpos
        ls = jnp.einsum('bghqd,bgqkld->bghqkl', qx, ksel) * scale + sel_bias(ds)
        bsz, g_, hg_, qn, kk, bs = ls.shape
        ps = masked_softmax(ls.reshape(bsz, g_, hg_, qn, kk * bs),
                            (ds >= 0)[:, :, None].reshape(bsz, g_, 1, qn, kk * bs))
        os_ = jnp.einsum('bghqm,bgqmd->bghqd', ps.astype(vsel.dtype),
                         vsel.reshape(bsz, g_, qn, kk * bs, hd))
        kw = lax.dynamic_slice_in_dim(kw_pad, s, WINDOW + QB, axis=2)
        vw = lax.dynamic_slice_in_dim(vw_pad, s, WINDOW + QB, axis=2)
        wpos = s - WINDOW + jnp.arange(WINDOW + QB, dtype=jnp.int32)
        dw = pos[:, None] - wpos[None]
        lw = jnp.einsum('bghqd,bgld->bghql', qx, kw) * scale + head_bias(dw)
        pw = masked_softmax(lw, (dw >= 0) & (dw < WINDOW) & (wpos[None] >= 0))
        ow = jnp.einsum('bghql,bgld->bghqd', pw.astype(vw.dtype), vw)
        return gx[..., 0:1] * oc + gx[..., 1:2] * os_ + gx[..., 2:3] * ow

    out = lax.map(block, (qb, gb, starts))
    return out.transpose(1, 0, 4, 2, 3, 5).reshape(B, T, H * hd)


def setup_inputs(seed: int = 0) -> dict:
    key = jax.random.key(seed)
    ks = jax.random.split(key, 20)
    nrm = lambda k, shape, fan: jax.random.normal(k, shape, jnp.float32) * (fan ** -0.5)
    gain = lambda k, shape: 1.0 + 0.01 * jax.random.normal(k, shape, jnp.float32)
    L = DEPTH
    return {
        "x": jax.random.normal(ks[0], (BATCH, SEQ, D_MODEL), jnp.float32),
        "norm_mix_g": gain(ks[1], (L, D_MODEL)),
        "w_in": nrm(ks[2], (L, D_MODEL, C_IN), D_MODEL),
        "cmp_pos_k": 0.02 * jax.random.normal(ks[3], (L, CMP_LEN, NSA_HD), jnp.float32),
        "cmp_pos_v": 0.02 * jax.random.normal(ks[4], (L, CMP_LEN, NSA_HD), jnp.float32),
        "cmp_w1_k": nrm(ks[5], (L, CMP_LEN * NSA_HD, CMP_HIDDEN), CMP_LEN * NSA_HD),
        "cmp_w2_k": nrm(ks[6], (L, CMP_HIDDEN, NSA_HD), CMP_HIDDEN),
        "cmp_w1_v": nrm(ks[7], (L, CMP_LEN * NSA_HD, CMP_HIDDEN), CMP_LEN * NSA_HD),
        "cmp_w2_v": nrm(ks[8], (L, CMP_HIDDEN, NSA_HD), CMP_HIDDEN),
        "w_o_ret": nrm(ks[9], (L, RET_HEADS * RET_DV, D_MODEL), RET_HEADS * RET_DV),
        "w_o_nsa": nrm(ks[10], (L, NSA_HEADS * NSA_HD, D_MODEL), NSA_HEADS * NSA_HD),
        "w_out": nrm(ks[11], (L, D_MODEL, D_MODEL), D_MODEL),
        "norm_ffn_g": gain(ks[12], (L, D_MODEL)),
        "w_ffn_in": nrm(ks[13], (L, D_MODEL, 2 * FFN_HIDDEN), D_MODEL),
        "w_ffn_out": nrm(ks[14], (L, FFN_HIDDEN, D_MODEL), FFN_HIDDEN),
        "rel_bias": 0.2 * jax.random.normal(ks[15], (REL_BUCKETS, NSA_HEADS), jnp.float32),
        "norm_final_g": gain(ks[16], (D_MODEL,)),
    }


def reference(x, norm_mix_g, w_in, cmp_pos_k, cmp_pos_v, cmp_w1_k, cmp_w2_k, cmp_w1_v, cmp_w2_v,
              w_o_ret, w_o_nsa, w_out, norm_ffn_g, w_ffn_in, w_ffn_out, rel_bias, norm_final_g):
    B, T, _ = x.shape
    pos = jnp.arange(T, dtype=jnp.int32)
    cuts = [int(c) for c in np.cumsum(SPLIT_SIZES)[:-1]]
    for i in range(DEPTH):
        h = rms_norm(x, norm_mix_g[i])
        proj = h @ w_in[i]
        q_r, k_r, v_r, g_r, q_n, kv_n, gate_n, m_a, m_b = jnp.split(proj, cuts, axis=-1)
        qr = rotary(q_r.reshape(B, T, RET_HEADS, RET_DK).astype(jnp.float32), pos)
        kr = rotary(k_r.reshape(B, T, RET_HEADS, RET_DK).astype(jnp.float32), pos) * (RET_DK ** -0.5)
        vr = v_r.reshape(B, T, RET_HEADS, RET_DV).astype(jnp.float32)
        o_r = retention(qr, kr, vr)
        mu = jnp.mean(o_r, -1, keepdims=True)
        var = jnp.mean(jnp.square(o_r - mu), -1, keepdims=True)
        o_r = ((o_r - mu) * lax.rsqrt(var + GN_EPS)).reshape(B, T, RET_HEADS * RET_DV).astype(x.dtype)
        y_ret = (jax.nn.silu(g_r) * o_r) @ w_o_ret[i]
        kv = kv_n.reshape(B, T, 6, NSA_KV_GROUPS, NSA_HD)
        o_n = nsa(q_n.reshape(B, T, NSA_HEADS, NSA_HD),
                  kv[:, :, 0], kv[:, :, 1], kv[:, :, 2], kv[:, :, 3], kv[:, :, 4], kv[:, :, 5],
                  jax.nn.sigmoid(gate_n).reshape(B, T, NSA_HEADS, 3),
                  cmp_pos_k[i], cmp_pos_v[i], cmp_w1_k[i], cmp_w2_k[i], cmp_w1_v[i], cmp_w2_v[i],
                  rel_bias)
        y_nsa = o_n.astype(x.dtype) @ w_o_nsa[i]
        mixed = (jax.nn.sigmoid(m_a) * y_ret + jax.nn.sigmoid(m_b) * y_nsa) @ w_out[i]
        x = x + mixed.astype(x.dtype)
        h = rms_norm(x, norm_ffn_g[i])
        a, b = jnp.split(h @ w_ffn_in[i], 2, axis=-1)
        x = x + ((jax.nn.silu(a) * b) @ w_ffn_out[i]).astype(x.dtype)
    return rms_norm(x, norm_final_g)
```

```python
import functools
import math

import jax
import jax.numpy as jnp
import numpy as np
from jax import lax
from jax.experimental import pallas as pl
from jax.experimental.pallas import tpu as pltpu

F32 = jnp.float32
BF16 = jnp.bfloat16

D_MODEL = 1024
RET_HEADS = 4
RET_DK = 256
RET_DV = 512
RET_CHUNK = 128
ROPE_BASE = 10000.0
GN_EPS = 1e-5
NSA_HEADS = 16
NSA_KV_GROUPS = 2
NSA_HG = NSA_HEADS // NSA_KV_GROUPS
NSA_HD = 64
CMP_LEN = 32
CMP_STRIDE = 16
CMP_HIDDEN = 256
SEL_BLOCK = 64
SEL_TOPK = 8
WINDOW = 512
FORCED_SCORE = 1e4
REL_BUCKETS = 32
REL_MAX_DIST = 128
FFN_HIDDEN = -(-8 * D_MODEL // (3 * 256)) * 256
EPS = 1e-6
NEG_INF = -1e30
PICKED = -3e38

LANES = 128
MIB = 1024 * 1024

_SEG = np.cumsum([0, RET_HEADS * RET_DK, RET_HEADS * RET_DK, RET_HEADS * RET_DV, RET_HEADS * RET_DV,
                  NSA_HEADS * NSA_HD, 6 * NSA_KV_GROUPS * NSA_HD, 3 * NSA_HEADS, D_MODEL, D_MODEL])
N_QR, N_KR, N_VR, N_GR = RET_HEADS * RET_DK, RET_HEADS * RET_DK, RET_HEADS * RET_DV, RET_HEADS * RET_DV
N_QN, N_KV = NSA_HEADS * NSA_HD, 6 * NSA_KV_GROUPS * NSA_HD
N_GATE = NSA_KV_GROUPS * LANES
_P = np.cumsum([0, N_QR, N_KR, N_VR, N_GR, N_QN, N_KV, N_GATE, D_MODEL, D_MODEL])
P_QR, P_KR, P_VR, P_GR, P_QN, P_KV, P_GATE, P_MA, P_MB, P_END = [int(v) for v in _P]

TM_PROJ = 512
TM_MERGE = 512
TQ = 128
NT = (((1,), (1,)), ((), ()))


def _dot(a, b):
    return jnp.dot(a, b, preferred_element_type=F32)


def _dot_nt(a, b):
    return lax.dot_general(a, b, NT, preferred_element_type=F32)


def _sigmoid(x):
    return 1.0 / (1.0 + jnp.exp(-x))


def _silu(x):
    return x * _sigmoid(x)


def _inproj_kernel(x_ref, g_ref, cos_ref, sin_ref, w_ref,
                   qr_o, kr_o, vr_o, gr_o, qn_o, kv_o, gate_o, ma_o, mb_o, h_scr):
    x = x_ref[0]
    h = x * lax.rsqrt(jnp.mean(x * x, -1, keepdims=True) + EPS) * g_ref[...]
    h_scr[...] = h.astype(BF16)
    cos = cos_ref[...]
    sin = sin_ref[...]
    half = RET_DK // 2

    def proj(lo, width):
        return _dot(h_scr[...], w_ref[:, lo:lo + width])

    for hd in range(RET_HEADS):
        for out, base, scale in ((qr_o, P_QR, 1.0), (kr_o, P_KR, RET_DK ** -0.5)):
            y = proj(base + hd * RET_DK, RET_DK)
            x1, x2 = y[:, :half], y[:, half:]
            out[0, :, hd * RET_DK:hd * RET_DK + half] = ((x1 * cos - x2 * sin) * scale).astype(BF16)
            out[0, :, hd * RET_DK + half:(hd + 1) * RET_DK] = ((x1 * sin + x2 * cos) * scale).astype(BF16)
    cw = 512
    for c in range(N_VR // cw):
        vr_o[0, :, c * cw:(c + 1) * cw] = proj(P_VR + c * cw, cw).astype(BF16)
        gr_o[0, :, c * cw:(c + 1) * cw] = _silu(proj(P_GR + c * cw, cw)).astype(BF16)
    for c in range(D_MODEL // cw):
        ma_o[0, :, c * cw:(c + 1) * cw] = _sigmoid(proj(P_MA + c * cw, cw)).astype(BF16)
        mb_o[0, :, c * cw:(c + 1) * cw] = _sigmoid(proj(P_MB + c * cw, cw)).astype(BF16)
    per = cw // NSA_HD
    for c in range(N_QN // cw):
        y = (proj(P_QN + c * cw, cw) * NSA_HD ** -0.5).astype(BF16)
        for j in range(per):
            qn_o[0, c * per + j] = y[:, j * NSA_HD:(j + 1) * NSA_HD]
    kw = N_KV // 2
    for c in range(2):
        y = proj(P_KV + c * kw, kw).astype(BF16)
        for j in range(kw // NSA_HD):
            s = c * (kw // NSA_HD) + j
            kv_o[s // NSA_KV_GROUPS, 0, s % NSA_KV_GROUPS] = y[:, j * NSA_HD:(j + 1) * NSA_HD]
    y = _sigmoid(proj(P_GATE, N_GATE))
    for g in range(NSA_KV_GROUPS):
        gate_o[0, g] = y[:, g * LANES:(g + 1) * LANES]


def _pack_w_in(w):
    s = [int(v) for v in _SEG]
    gate = w[:, s[6]:s[7]].reshape(D_MODEL, NSA_KV_GROUPS, 3 * NSA_HG)
    gate = jnp.pad(gate, ((0, 0), (0, 0), (0, LANES - 3 * NSA_HG))).reshape(D_MODEL, N_GATE)
    return jnp.concatenate([w[:, s[0]:s[6]], gate, w[:, s[7]:s[9]]], axis=1).astype(BF16)


def _in_projection(x, g, w_packed, cos, sin):
    B, T, D = x.shape
    tm = min(TM_PROJ, T)
    nt = T // tm
    row = lambda width: pl.BlockSpec((1, tm, width), lambda b, t: (b, t, 0))
    out_shape = (
        jax.ShapeDtypeStruct((B, T, N_QR), BF16), jax.ShapeDtypeStruct((B, T, N_KR), BF16),
        jax.ShapeDtypeStruct((B, T, N_VR), BF16), jax.ShapeDtypeStruct((B, T, N_GR), BF16),
        jax.ShapeDtypeStruct((B, NSA_HEADS, T, NSA_HD), BF16),
        jax.ShapeDtypeStruct((6, B, NSA_KV_GROUPS, T, NSA_HD), BF16),
        jax.ShapeDtypeStruct((B, NSA_KV_GROUPS, T, LANES), F32),
        jax.ShapeDtypeStruct((B, T, D_MODEL), BF16), jax.ShapeDtypeStruct((B, T, D_MODEL), BF16))
    out_specs = (
        row(N_QR), row(N_KR), row(N_VR), row(N_GR),
        pl.BlockSpec((1, NSA_HEADS, tm, NSA_HD), lambda b, t: (b, 0, t, 0)),
        pl.BlockSpec((6, 1, NSA_KV_GROUPS, tm, NSA_HD), lambda b, t: (0, b, 0, t, 0)),
        pl.BlockSpec((1, NSA_KV_GROUPS, tm, LANES), lambda b, t: (b, 0, t, 0)),
        row(D_MODEL), row(D_MODEL))
    out_bytes = tm * (2 * (N_QR + N_KR + N_VR + N_GR + N_QN + N_KV + 2 * D_MODEL) + 4 * N_GATE)
    vmem = 2 * out_bytes + 2 * tm * D * 4 + 2 * D * P_END + tm * D * 2 + 8 * MIB
    return pl.pallas_call(
        _inproj_kernel,
        grid=(B, nt),
        in_specs=[pl.BlockSpec((1, tm, D), lambda b, t: (b, t, 0)),
                  pl.BlockSpec((1, D), lambda b, t: (0, 0)),
                  pl.BlockSpec((tm, RET_DK // 2), lambda b, t: (t, 0)),
                  pl.BlockSpec((tm, RET_DK // 2), lambda b, t: (t, 0)),
                  pl.BlockSpec((D, P_END), lambda b, t: (0, 0), pipeline_mode=pl.Buffered(1))],
        out_specs=out_specs,
        out_shape=out_shape,
        scratch_shapes=[pltpu.VMEM((tm, D), BF16)],
        compiler_params=pltpu.CompilerParams(vmem_limit_bytes=int(vmem)),
        name="in_projection",
    )(x, g.reshape(1, D), cos, sin, w_packed)


def _retention_kernel(q_ref, k_ref, v_ref, g_ref, dmask_ref, xi_ref, zeta_ref, gch_ref, o_ref, r_scr):
    C = RET_CHUNK
    nc = q_ref.shape[1] // C
    r_scr[...] = jnp.zeros_like(r_scr)
    dmask = dmask_ref[0]
    xi = xi_ref[0]
    zeta = zeta_ref[0]
    gch = gch_ref[0, :, 0:1]

    def step(c, carry):
        rows = pl.ds(pl.multiple_of(c * C, C), C)
        qc = q_ref[0, rows, :]
        kc = k_ref[0, rows, :]
        vc = v_ref[0, rows, :]
        r = r_scr[...]
        s = _dot_nt(qc, kc) * dmask
        o = _dot(s.astype(BF16), vc) + _dot(qc, r.astype(BF16)) * xi
        kz = (kc.astype(F32) * zeta).astype(BF16)
        r_scr[...] = gch * r + lax.dot_general(kz, vc, (((0,), (0,)), ((), ())), preferred_element_type=F32)
        mu = jnp.mean(o, -1, keepdims=True)
        d = o - mu
        var = jnp.mean(d * d, -1, keepdims=True)
        on = d * lax.rsqrt(var + GN_EPS)
        o_ref[0, rows, :] = (g_ref[0, rows, :].astype(F32) * on).astype(BF16)
        return carry

    lax.fori_loop(0, nc, step, 0)


def _retention(qr, kr, vr, gr):
    B, T, _ = qr.shape
    C = RET_CHUNK
    lg = jnp.log(1.0 - 2.0 ** (-5.0 - jnp.arange(RET_HEADS, dtype=F32)))
    n = jnp.arange(C, dtype=F32)
    diff = n[:, None] - n[None, :]
    dmask = jnp.where(diff >= 0, jnp.exp(jnp.maximum(diff, 0.0)[None] * lg[:, None, None]), 0.0)
    xi = jnp.exp((n + 1.0)[None] * lg[:, None])[:, :, None]
    zeta = jnp.exp((C - 1.0 - n)[None] * lg[:, None])[:, :, None]
    gch = jnp.broadcast_to(jnp.exp(C * lg)[:, None, None], (RET_HEADS, 1, LANES))
    qk_spec = pl.BlockSpec((1, T, RET_DK), lambda b, h: (b, 0, h))
    v_spec = pl.BlockSpec((1, T, RET_DV), lambda b, h: (b, 0, h))
    return pl.pallas_call(
        _retention_kernel,
        grid=(B, RET_HEADS),
        in_specs=[qk_spec, qk_spec, v_spec, v_spec,
                  pl.BlockSpec((1, C, C), lambda b, h: (h, 0, 0)),
                  pl.BlockSpec((1, C, 1), lambda b, h: (h, 0, 0)),
                  pl.BlockSpec((1, C, 1), lambda b, h: (h, 0, 0)),
                  pl.BlockSpec((1, 1, LANES), lambda b, h: (h, 0, 0))],
        out_specs=v_spec,
        out_shape=jax.ShapeDtypeStruct((B, T, RET_HEADS * RET_DV), BF16),
        scratch_shapes=[pltpu.VMEM((RET_DK, RET_DV), F32)],
        compiler_params=pltpu.CompilerParams(vmem_limit_bytes=32 * MIB),
        name="retention",
    )(qr, kr, vr, gr, dmask, xi, zeta, gch)


def _gelu_tanh(x):
    return 0.5 * x * (1.0 + jnp.tanh(math.sqrt(2.0 / math.pi) * (x + 0.044715 * (x * x * x))))


def _compress_kernel(xk_ref, xv_ref, pk_ref, pv_ref, w1k_ref, w2k_ref, w1v_ref, w2v_ref, ko_ref, vo_ref):
    half = CMP_STRIDE * NSA_HD

    def mlp(x_ref, p_ref, w1_ref, w2_ref, o_ref):
        for g in range(NSA_KV_GROUPS):
            x = x_ref[0, g].astype(F32)
            a = _dot((x + p_ref[0:1, :]).astype(BF16), w1_ref[0:half, :])
            b = _dot((x + p_ref[1:2, :]).astype(BF16), w1_ref[half:2 * half, :])
            rows = b.shape[0]
            pre = a + pltpu.roll(b, rows - 1, 0)
            o_ref[0, g] = _dot(_gelu_tanh(pre).astype(BF16), w2_ref[...]).astype(BF16)

    mlp(xk_ref, pk_ref, w1k_ref, w2k_ref, ko_ref)
    mlp(xv_ref, pv_ref, w1v_ref, w2v_ref, vo_ref)


def _compress(kc16, vc16, pos_k, pos_v, w1k, w2k, w1v, w2v):
    B, G, NC, F = kc16.shape
    x_spec = pl.BlockSpec((1, G, NC, F), lambda b: (b, 0, 0, 0))
    full = lambda a: pl.BlockSpec(a.shape, lambda b: (0,) * a.ndim)
    o_spec = pl.BlockSpec((1, G, NC, NSA_HD), lambda b: (b, 0, 0, 0))
    o_shape = jax.ShapeDtypeStruct((B, G, NC, NSA_HD), BF16)
    pk = pos_k.reshape(2, F)
    pv = pos_v.reshape(2, F)
    w = [a.astype(BF16) for a in (w1k, w2k, w1v, w2v)]
    return pl.pallas_call(
        _compress_kernel,
        grid=(B,),
        in_specs=[x_spec, x_spec, full(pk), full(pv)] + [full(a) for a in w],
        out_specs=(o_spec, o_spec),
        out_shape=(o_shape, o_shape),
        compiler_params=pltpu.CompilerParams(vmem_limit_bytes=32 * MIB),
        name="nsa_compress",
    )(kc16, vc16, pk, pv, *w)


def _bias_kernel(tab_ref, nb_ref, cb_ref, near_o, cmp_o):
    h = pl.program_id(0)
    far = tab_ref[REL_BUCKETS - 1, h]

    def build(bk):
        acc = jnp.zeros(bk.shape, F32)
        for k in range(REL_BUCKETS - 1):
            acc = jnp.where(bk == k, tab_ref[k, h] - far, acc)
        return acc

    near_o[0] = build(nb_ref[...])
    cmp_o[0] = build(cb_ref[...])


def _t5_bucket(dist):
    dist = jnp.maximum(dist, 0)
    max_exact = REL_BUCKETS // 2
    large = max_exact + (jnp.log(jnp.maximum(dist, 1).astype(F32) / max_exact)
                         / math.log(REL_MAX_DIST / max_exact) * (REL_BUCKETS - max_exact)).astype(jnp.int32)
    large = jnp.minimum(large, REL_BUCKETS - 1)
    return jnp.where(dist < max_exact, dist, large)


def _bias_tables(rel_bias, T, tq):
    nc = T // CMP_STRIDE
    i = jnp.arange(tq, dtype=jnp.int32)
    near_b = _t5_bucket(jnp.arange(2, dtype=jnp.int32)[:, None, None] * tq + i[None, :, None] - i[None, None, :])
    cmp_end = jnp.arange(nc, dtype=jnp.int32) * CMP_STRIDE + CMP_LEN - 1
    cmp_b = _t5_bucket(jnp.arange(T, dtype=jnp.int32)[:, None] - cmp_end[None, :])
    return pl.pallas_call(
        _bias_kernel,
        grid=(NSA_HEADS,),
        in_specs=[pl.BlockSpec(memory_space=pltpu.SMEM),
                  pl.BlockSpec(near_b.shape, lambda h: (0, 0, 0)),
                  pl.BlockSpec(cmp_b.shape, lambda h: (0, 0))],
        out_specs=(pl.BlockSpec((1,) + near_b.shape, lambda h: (h, 0, 0, 0)),
                   pl.BlockSpec((1,) + cmp_b.shape, lambda h: (h, 0, 0))),
        out_shape=(jax.ShapeDtypeStruct((NSA_HEADS,) + near_b.shape, F32),
                   jax.ShapeDtypeStruct((NSA_HEADS,) + cmp_b.shape, F32)),
        name="rel_bias_tables",
    )(rel_bias, near_b, cmp_b)


def _nsa_kernel(q_ref, kc_ref, vc_ref, ks_ref, vs_ref, kw_ref, vw_ref, gt_ref,
                near_ref, cb_ref, ovl_ref, exp_ref, o_ref,
                msk_scr, m_scr, l_scr, acc_scr, os_scr, *, tq, sel_k):
    HG = NSA_HG
    R = HG * tq
    T = ks_ref.shape[2]
    NC = kc_ref.shape[2]
    NS = ovl_ref.shape[0]
    qi = pl.program_id(2)
    s0 = pl.multiple_of(qi * tq, tq)
    q = q_ref[0, 0].reshape(R, NSA_HD)

    row_i = lax.broadcasted_iota(jnp.int32, (tq, tq), 0)
    col_i = lax.broadcasted_iota(jnp.int32, (tq, tq), 1)
    causal_add = jnp.where(col_i <= row_i, 0.0, NEG_INF)

    lc = _dot_nt(q, kc_ref[0, 0]).reshape(HG, tq, NC) + cb_ref[:, pl.ds(s0, tq), :]
    pos_c = s0 + lax.broadcasted_iota(jnp.int32, (tq, NC), 0)
    cend = lax.broadcasted_iota(jnp.int32, (tq, NC), 1) * CMP_STRIDE + (CMP_LEN - 1)
    valid_c = cend <= pos_c
    z = jnp.where(valid_c[None], lc, NEG_INF)
    z = z - jnp.max(z, -1, keepdims=True)
    e = jnp.exp(z) * valid_c.astype(F32)[None]
    pc = e / jnp.maximum(jnp.sum(e, -1, keepdims=True), 1e-30)
    oc = _dot(pc.reshape(R, NC).astype(BF16), vc_ref[0, 0])
    psum = jnp.sum(pc, axis=0)

    p_hi = psum.astype(BF16)
    r1 = psum - p_hi.astype(F32)
    p_mid = r1.astype(BF16)
    p_lo = (r1 - p_mid.astype(F32)).astype(BF16)
    ovl = ovl_ref[...]
    imp = _dot_nt(ovl, p_hi) + _dot_nt(ovl, p_mid) + _dot_nt(ovl, p_lo)
    jrow = lax.broadcasted_iota(jnp.int32, (NS, tq), 0)
    posl = s0 + lax.broadcasted_iota(jnp.int32, (NS, tq), 1)
    cur = lax.shift_right_logical(posl, int(math.log2(SEL_BLOCK)))
    forced = (jrow == 0) | (jrow == cur) | (jrow == cur - 1)
    imp = jnp.where(forced, FORCED_SCORE, imp)
    imp = jnp.where(jrow * SEL_BLOCK <= posl, imp, NEG_INF)
    sel = jnp.zeros((NS, tq), F32)
    for _ in range(sel_k):
        mx = jnp.max(imp, axis=0, keepdims=True)
        first = jnp.min(jnp.where(imp == mx, jrow, NS), axis=0, keepdims=True)
        hit = jrow == first
        sel = jnp.where(hit, 1.0, sel)
        imp = jnp.where(hit, PICKED, imp)
    sel_sq = jnp.concatenate([sel, jnp.zeros((tq - NS, tq), F32)], axis=0) if NS < tq else sel
    sel_t = sel_sq.T.astype(BF16)
    msk_scr[...] = (_dot(sel_t, exp_ref[...]) - 1.0) * -NEG_INF

    def reset():
        m_scr[...] = jnp.full(m_scr.shape, NEG_INF, F32)
        l_scr[...] = jnp.zeros_like(l_scr)
        acc_scr[...] = jnp.zeros_like(acc_scr)

    def update(k, v, add):
        s = (_dot_nt(q, k).reshape(HG, tq, tq) + add).reshape(R, tq)
        m_prev = m_scr[...]
        m_next = jnp.maximum(m_prev, jnp.max(s, axis=1, keepdims=True))
        alpha = jnp.exp(m_prev - m_next)
        p = jnp.exp(s - m_next)
        l_scr[...] = alpha * l_scr[...] + jnp.sum(p, axis=1, keepdims=True)
        acc_scr[...] = alpha[:, :NSA_HD] * acc_scr[...] + _dot(p.astype(BF16), v)
        m_scr[...] = m_next

    def chunk(ref, idx):
        return ref[0, 0, pl.ds(pl.multiple_of(idx * tq, tq), tq), :]

    def sel_mask(idx):
        return msk_scr[:, pl.ds(pl.multiple_of(idx * tq, tq), tq)]

    reset()
    update(chunk(ks_ref, qi), chunk(vs_ref, qi), (sel_mask(qi) + causal_add)[None] + near_ref[:, 0])

    @pl.when(qi >= 1)
    def _():
        update(chunk(ks_ref, qi - 1), chunk(vs_ref, qi - 1), sel_mask(qi - 1)[None] + near_ref[:, 1])

    def far_step(c, carry):
        update(chunk(ks_ref, c), chunk(vs_ref, c), sel_mask(c)[None])
        return carry

    lax.fori_loop(0, jnp.maximum(qi - 1, 0), far_step, 0)
    os_scr[...] = acc_scr[...] / l_scr[:, :NSA_HD]

    ND = WINDOW // tq
    reset()
    update(chunk(kw_ref, qi), chunk(vw_ref, qi), causal_add[None] + near_ref[:, 0])
    for d in range(1, ND + 1):
        if d == 1:
            add = near_ref[:, 1]
        elif d == ND:
            add = jnp.where(col_i > row_i, 0.0, NEG_INF)[None]
        else:
            add = jnp.zeros((1, tq, tq), F32)

        @pl.when(qi >= d)
        def _(d=d, add=add):
            update(chunk(kw_ref, qi - d), chunk(vw_ref, qi - d), add)
    ow = acc_scr[...] / l_scr[:, :NSA_HD]

    gt = gt_ref[0, 0]
    os_ = os_scr[...]
    for hg in range(HG):
        rows = slice(hg * tq, (hg + 1) * tq)
        o = (gt[:, 3 * hg:3 * hg + 1] * oc[rows] + gt[:, 3 * hg + 1:3 * hg + 2] * os_[rows]
             + gt[:, 3 * hg + 2:3 * hg + 3] * ow[rows])
        o_ref[0, 0, hg] = o.astype(BF16)


def _nsa_attention(qn, kc, vc, ks, vs, kw, vw, gates, near, cmpb):
    B, G, HG, T, hd = qn.shape
    tq = min(TQ, T)
    nq = T // tq
    NC = kc.shape[2]
    NS = T // SEL_BLOCK
    sel_k = min(SEL_TOPK, NS)
    n = np.arange(NC)
    j = np.arange(NS)
    ovl = ((n[None, :] * CMP_STRIDE < (j[:, None] + 1) * SEL_BLOCK)
           & (n[None, :] * CMP_STRIDE + CMP_LEN - 1 >= j[:, None] * SEL_BLOCK) & (n[None, :] < NC - 1))
    expand = (np.arange(T)[None, :] // SEL_BLOCK) == np.arange(tq)[:, None]
    ovl = jnp.asarray(ovl, BF16)
    expand = jnp.asarray(expand, BF16)
    seq = lambda rows: pl.BlockSpec((1, 1, rows, hd), lambda g, b, t: (b, g, 0, 0))
    return pl.pallas_call(
        functools.partial(_nsa_kernel, tq=tq, sel_k=sel_k),
        grid=(G, B, nq),
        in_specs=[pl.BlockSpec((1, 1, HG, tq, hd), lambda g, b, t: (b, g, 0, t, 0)),
                  seq(NC), seq(NC), seq(T), seq(T), seq(T), seq(T),
                  pl.BlockSpec((1, 1, tq, LANES), lambda g, b, t: (b, g, t, 0)),
                  pl.BlockSpec((HG, 2, tq, tq), lambda g, b, t: (g, 0, 0, 0)),
                  pl.BlockSpec((HG, T, NC), lambda g, b, t: (g, 0, 0)),
                  pl.BlockSpec(ovl.shape, lambda g, b, t: (0, 0)),
                  pl.BlockSpec(expand.shape, lambda g, b, t: (0, 0))],
        out_specs=pl.BlockSpec((1, 1, HG, tq, hd), lambda g, b, t: (b, g, 0, t, 0)),
        out_shape=jax.ShapeDtypeStruct((B, G, HG, T, hd), BF16),
        scratch_shapes=[pltpu.VMEM((tq, T), F32),
                        pltpu.VMEM((HG * tq, LANES), F32), pltpu.VMEM((HG * tq, LANES), F32),
                        pltpu.VMEM((HG * tq, hd), F32), pltpu.VMEM((HG * tq, hd), F32)],
        compiler_params=pltpu.CompilerParams(vmem_limit_bytes=48 * MIB),
        name="nsa_attention",
    )(qn, kc, vc, ks, vs, kw, vw, gates, near, cmpb, ovl, expand)


def _merge_kernel(x_ref, oret_ref, on_ref, ma_ref, mb_ref, wr_ref, wn_ref, wo_ref, o_ref):
    y_ret = _dot(oret_ref[...], wr_ref[...])
    y_nsa = _dot(on_ref[...], wn_ref[...])
    mixed = ma_ref[...].astype(F32) * y_ret + mb_ref[...].astype(F32) * y_nsa
    o_ref[...] = x_ref[...] + _dot(mixed.astype(BF16), wo_ref[...])


def _merge(x2, oret, on, ma, mb, w_o_ret, w_o_nsa, w_out):
    M, D = x2.shape
    tm = min(TM_MERGE, M)
    row = lambda width: pl.BlockSpec((tm, width), lambda i: (i, 0))
    res = lambda a: pl.BlockSpec(a.shape, lambda i: (0, 0), pipeline_mode=pl.Buffered(1))
    w = [a.astype(BF16) for a in (w_o_ret, w_o_nsa, w_out)]
    return pl.pallas_call(
        _merge_kernel,
        grid=(M // tm,),
        in_specs=[row(D), row(oret.shape[1]), row(D), row(D), row(D)] + [res(a) for a in w],
        out_specs=row(D),
        out_shape=jax.ShapeDtypeStruct((M, D), F32),
        compiler_params=pltpu.CompilerParams(vmem_limit_bytes=48 * MIB),
        name="merge_out_projection",
    )(x2, oret, on, ma, mb, *w)


def _rms(x, g):
    return x * lax.rsqrt(jnp.mean(x * x, -1, keepdims=True) + EPS) * g


def _ffn_kernel(x_ref, g_ref, wi_ref, wo_ref, gf_ref, o_ref, h_scr, acc_scr, *, final_norm):
    x = x_ref[...]
    h_scr[...] = _rms(x, g_ref[...]).astype(BF16)
    acc_scr[...] = x
    cw = FFN_HIDDEN // 2
    for c in range(2):
        a = _dot(h_scr[...], wi_ref[:, c * cw:(c + 1) * cw])
        b = _dot(h_scr[...], wi_ref[:, FFN_HIDDEN + c * cw:FFN_HIDDEN + (c + 1) * cw])
        acc_scr[...] += _dot((_silu(a) * b).astype(BF16), wo_ref[c * cw:(c + 1) * cw, :])
    y = acc_scr[...]
    o_ref[...] = _rms(y, gf_ref[...]) if final_norm else y


def _ffn(x2, g, w_in, w_out, g_final, final_norm):
    M, D = x2.shape
    tm = min(TM_MERGE, M)
    row = pl.BlockSpec((tm, D), lambda i: (i, 0))
    vec = pl.BlockSpec((1, D), lambda i: (0, 0))
    res = lambda a: pl.BlockSpec(a.shape, lambda i: (0, 0), pipeline_mode=pl.Buffered(1))
    wi, wo = w_in.astype(BF16), w_out.astype(BF16)
    return pl.pallas_call(
        functools.partial(_ffn_kernel, final_norm=final_norm),
        grid=(M // tm,),
        in_specs=[row, vec, res(wi), res(wo), vec],
        out_specs=row,
        out_shape=jax.ShapeDtypeStruct((M, D), F32),
        scratch_shapes=[pltpu.VMEM((tm, D), BF16), pltpu.VMEM((tm, D), F32)],
        compiler_params=pltpu.CompilerParams(vmem_limit_bytes=52 * MIB),
        name="swiglu_ffn",
    )(x2, g.reshape(1, D), wi, wo, g_final.reshape(1, D))


def kernel(x, norm_mix_g, w_in, cmp_pos_k, cmp_pos_v, cmp_w1_k, cmp_w2_k, cmp_w1_v, cmp_w2_v, w_o_ret, w_o_nsa,
           w_out, norm_ffn_g, w_ffn_in, w_ffn_out, rel_bias, norm_final_g):
    B, T, D = x.shape
    depth = w_in.shape[0]
    G, HG = NSA_KV_GROUPS, NSA_HG
    half = RET_DK // 2
    freqs = ROPE_BASE ** (-jnp.arange(half, dtype=F32) / half)
    ang = jnp.arange(T, dtype=jnp.int32).astype(F32)[:, None] * freqs
    cos, sin = jnp.cos(ang), jnp.sin(ang)
    near, cmpb = _bias_tables(rel_bias, T, min(TQ, T))
    for i in range(depth):
        qr, kr, vr, gr, qn, kvn, gates, ma, mb = _in_projection(x, norm_mix_g[i], _pack_w_in(w_in[i]), cos, sin)
        o_ret = _retention(qr, kr, vr, gr)
        blocks16 = lambda a: a.reshape(B, G, T // CMP_STRIDE, CMP_STRIDE * NSA_HD)
        kc, vc = _compress(blocks16(kvn[0]), blocks16(kvn[1]), cmp_pos_k[i], cmp_pos_v[i],
                           cmp_w1_k[i], cmp_w2_k[i], cmp_w1_v[i], cmp_w2_v[i])
        o_n = _nsa_attention(qn.reshape(B, G, HG, T, NSA_HD), kc, vc, kvn[2], kvn[3], kvn[4], kvn[5],
                             gates, near, cmpb)
        o_n = o_n.reshape(B, NSA_HEADS, T, NSA_HD).transpose(0, 2, 1, 3).reshape(B * T, NSA_HEADS * NSA_HD)
        x2 = _merge(x.reshape(B * T, D), o_ret.reshape(B * T, -1), o_n, ma.reshape(B * T, D), mb.reshape(B * T, D),
                    w_o_ret[i], w_o_nsa[i], w_out[i])
        x2 = _ffn(x2, norm_ffn_g[i], w_ffn_in[i], w_ffn_out[i], norm_final_g, final_norm=(i == depth - 1))
        x = x2.reshape(B, T, D)
    return x
```

```python
import functools
import math

import jax
import jax.numpy as jnp
import numpy as np
from jax import lax
from jax.experimental import pallas as pl
from jax.experimental.pallas import tpu as pltpu

F32 = jnp.float32
BF16 = jnp.bfloat16

D_MODEL = 1024
RET_HEADS = 4
RET_DK = 256
RET_DV = 512
RET_CHUNK = 128
ROPE_BASE = 10000.0
GN_EPS = 1e-5
NSA_HEADS = 16
NSA_KV_GROUPS = 2
NSA_HG = NSA_HEADS // NSA_KV_GROUPS
NSA_HD = 64
CMP_LEN = 32
CMP_STRIDE = 16
CMP_HIDDEN = 256
SEL_BLOCK = 64
SEL_TOPK = 8
WINDOW = 512
FORCED_SCORE = 1e4
REL_BUCKETS = 32
REL_MAX_DIST = 128
FFN_HIDDEN = -(-8 * D_MODEL // (3 * 256)) * 256
EPS = 1e-6
NEG_INF = -1e30
PICKED = -3e38
MASKED_CODE = REL_BUCKETS
NEAR_OFFSETS = (1, 0, 2, 4, 5, 3)

LANES = 128
MIB = 1024 * 1024

_SEG = np.cumsum([0, RET_HEADS * RET_DK, RET_HEADS * RET_DK, RET_HEADS * RET_DV, RET_HEADS * RET_DV,
                  NSA_HEADS * NSA_HD, 6 * NSA_KV_GROUPS * NSA_HD, 3 * NSA_HEADS, D_MODEL, D_MODEL])
N_QR, N_KR, N_VR, N_GR = RET_HEADS * RET_DK, RET_HEADS * RET_DK, RET_HEADS * RET_DV, RET_HEADS * RET_DV
N_QN, N_KV = NSA_HEADS * NSA_HD, 6 * NSA_KV_GROUPS * NSA_HD
N_GATE = NSA_KV_GROUPS * LANES
_P = np.cumsum([0, N_QR, N_KR, N_VR, N_GR, N_QN, N_KV, N_GATE, D_MODEL, D_MODEL])
P_QR, P_KR, P_VR, P_GR, P_QN, P_KV, P_GATE, P_MA, P_MB, P_END = [int(v) for v in _P]

TM_PROJ = 512
TM_MERGE = 512
TQ = 128
NT = (((1,), (1,)), ((), ()))


def _dot(a, b):
    return jnp.dot(a, b, preferred_element_type=F32)


def _dot_nt(a, b):
    return lax.dot_general(a, b, NT, preferred_element_type=F32)


def _sigmoid(x):
    return 1.0 / (1.0 + jnp.exp(-x))


def _silu(x):
    return x * _sigmoid(x)


def _inproj_kernel(x_ref, g_ref, cos_ref, sin_ref, feat_ref, w_ref,
                   qr_o, kr_o, vr_o, gr_o, qn_o, kvc_o, ks_o, vs_o, kw_o, vw_o, gate_o, ma_o, mb_o, h_scr):
    x = x_ref[0]
    h = x * lax.rsqrt(jnp.mean(x * x, -1, keepdims=True) + EPS) * g_ref[...]
    h_scr[...] = h.astype(BF16)
    cos = cos_ref[...]
    sin = sin_ref[...]
    half = RET_DK // 2

    def proj(lo, width):
        return _dot(h_scr[...], w_ref[:, lo:lo + width])

    for hd in range(RET_HEADS):
        for out, base, scale in ((qr_o, P_QR, 1.0), (kr_o, P_KR, RET_DK ** -0.5)):
            y = proj(base + hd * RET_DK, RET_DK)
            x1, x2 = y[:, :half], y[:, half:]
            out[0, :, hd * RET_DK:hd * RET_DK + half] = ((x1 * cos - x2 * sin) * scale).astype(BF16)
            out[0, :, hd * RET_DK + half:(hd + 1) * RET_DK] = ((x1 * sin + x2 * cos) * scale).astype(BF16)
    cw = 512
    for c in range(N_VR // cw):
        vr_o[0, :, c * cw:(c + 1) * cw] = proj(P_VR + c * cw, cw).astype(BF16)
        gr_o[0, :, c * cw:(c + 1) * cw] = _silu(proj(P_GR + c * cw, cw)).astype(BF16)
    for c in range(D_MODEL // cw):
        ma_o[0, :, c * cw:(c + 1) * cw] = _sigmoid(proj(P_MA + c * cw, cw)).astype(BF16)
        mb_o[0, :, c * cw:(c + 1) * cw] = _sigmoid(proj(P_MB + c * cw, cw)).astype(BF16)
    per = cw // NSA_HD
    for c in range(N_QN // cw):
        y = (proj(P_QN + c * cw, cw) * NSA_HD ** -0.5).astype(BF16)
        for j in range(per):
            qn_o[0, c * per + j] = y[:, j * NSA_HD:(j + 1) * NSA_HD]
    y = proj(P_KV, N_KV).astype(BF16)
    ones = jnp.ones((y.shape[0], LANES), BF16)
    for g in range(NSA_KV_GROUPS):
        piece = lambda j: y[:, (j * NSA_KV_GROUPS + g) * NSA_HD:(j * NSA_KV_GROUPS + g + 1) * NSA_HD]
        kvc_o[0, 0, g] = piece(0)
        kvc_o[1, 0, g] = piece(1)
        ks_o[0, g, :, :NSA_HD] = piece(2)
        ks_o[0, g, :, NSA_HD:] = feat_ref[:, NSA_HD:]
        kw_o[0, g] = piece(4)
        for out, j in ((vs_o, 3), (vw_o, 5)):
            out[0, g, :, :NSA_HD] = piece(j)
            out[0, g, :, NSA_HD:LANES] = piece(j)
            out[0, g, :, LANES:] = ones
    y = _sigmoid(proj(P_GATE, N_GATE))
    for g in range(NSA_KV_GROUPS):
        gate_o[0, g] = y[:, g * LANES:(g + 1) * LANES]


def _pack_w_in(w):
    s = [int(v) for v in _SEG]
    gate = w[:, s[6]:s[7]].reshape(D_MODEL, NSA_KV_GROUPS, 3 * NSA_HG)
    gate = jnp.pad(gate, ((0, 0), (0, 0), (0, LANES - 3 * NSA_HG))).reshape(D_MODEL, N_GATE)
    return jnp.concatenate([w[:, s[0]:s[6]], gate, w[:, s[7]:s[9]]], axis=1).astype(BF16)


def _in_projection(x, g, w_packed, cos, sin):
    B, T, D = x.shape
    G = NSA_KV_GROUPS
    tm = min(TM_PROJ, T)
    nt = T // tm
    assert T // SEL_BLOCK <= LANES - NSA_HD
    own_block = (np.arange(T)[:, None] // SEL_BLOCK) == (np.arange(LANES)[None, :] - NSA_HD)
    feat = jnp.asarray(np.where(own_block, NEG_INF, 0.0), BF16)
    row = lambda width: pl.BlockSpec((1, tm, width), lambda b, t: (b, t, 0))
    grp = lambda width: pl.BlockSpec((1, G, tm, width), lambda b, t: (b, 0, t, 0))
    grp_shape = lambda width, dt=BF16: jax.ShapeDtypeStruct((B, G, T, width), dt)
    out_shape = (
        jax.ShapeDtypeStruct((B, T, N_QR), BF16), jax.ShapeDtypeStruct((B, T, N_KR), BF16),
        jax.ShapeDtypeStruct((B, T, N_VR), BF16), jax.ShapeDtypeStruct((B, T, N_GR), BF16),
        jax.ShapeDtypeStruct((B, NSA_HEADS, T, NSA_HD), BF16),
        jax.ShapeDtypeStruct((2, B, G, T, NSA_HD), BF16),
        grp_shape(LANES), grp_shape(2 * LANES), grp_shape(NSA_HD), grp_shape(2 * LANES),
        grp_shape(LANES, F32),
        jax.ShapeDtypeStruct((B, T, D_MODEL), BF16), jax.ShapeDtypeStruct((B, T, D_MODEL), BF16))
    out_specs = (
        row(N_QR), row(N_KR), row(N_VR), row(N_GR),
        pl.BlockSpec((1, NSA_HEADS, tm, NSA_HD), lambda b, t: (b, 0, t, 0)),
        pl.BlockSpec((2, 1, G, tm, NSA_HD), lambda b, t: (0, b, 0, t, 0)),
        grp(LANES), grp(2 * LANES), grp(NSA_HD), grp(2 * LANES),
        grp(LANES),
        row(D_MODEL), row(D_MODEL))
    out_bytes = tm * (2 * (N_QR + N_KR + N_VR + N_GR + N_QN + 2 * D_MODEL) + 2 * G * (3 * NSA_HD + 5 * LANES)
                      + 4 * N_GATE)
    vmem = 2 * out_bytes + 2 * tm * D * 4 + 2 * D * P_END + tm * D * 2 + 8 * MIB
    return pl.pallas_call(
        _inproj_kernel,
        grid=(B, nt),
        in_specs=[pl.BlockSpec((1, tm, D), lambda b, t: (b, t, 0)),
                  pl.BlockSpec((1, D), lambda b, t: (0, 0)),
                  pl.BlockSpec((tm, RET_DK // 2), lambda b, t: (t, 0)),
                  pl.BlockSpec((tm, RET_DK // 2), lambda b, t: (t, 0)),
                  pl.BlockSpec((tm, LANES), lambda b, t: (t, 0)),
                  pl.BlockSpec((D, P_END), lambda b, t: (0, 0), pipeline_mode=pl.Buffered(1))],
        out_specs=out_specs,
        out_shape=out_shape,
        scratch_shapes=[pltpu.VMEM((tm, D), BF16)],
        compiler_params=pltpu.CompilerParams(vmem_limit_bytes=int(vmem)),
        name="in_projection",
    )(x, g.reshape(1, D), cos, sin, feat, w_packed)


def _retention_kernel(q_ref, k_ref, v_ref, g_ref, dmask_ref, xi_ref, zeta_ref, gch_ref, o_ref, r_scr):
    C = RET_CHUNK
    nc = q_ref.shape[1] // C
    r_scr[...] = jnp.zeros_like(r_scr)
    dmask = dmask_ref[0]
    xi = xi_ref[0]
    zeta = zeta_ref[0]
    gch = gch_ref[0, :, 0:1]

    def step(c, carry):
        rows = pl.ds(pl.multiple_of(c * C, C), C)
        qc = q_ref[0, rows, :]
        kc = k_ref[0, rows, :]
        vc = v_ref[0, rows, :]
        r = r_scr[...]
        s = _dot_nt(qc, kc) * dmask
        o = _dot(s.astype(BF16), vc) + _dot(qc, r.astype(BF16)) * xi
        kz = (kc.astype(F32) * zeta).astype(BF16)
        r_scr[...] = gch * r + lax.dot_general(kz, vc, (((0,), (0,)), ((), ())), preferred_element_type=F32)
        mu = jnp.mean(o, -1, keepdims=True)
        d = o - mu
        var = jnp.mean(d * d, -1, keepdims=True)
        on = d * lax.rsqrt(var + GN_EPS)
        o_ref[0, rows, :] = (g_ref[0, rows, :].astype(F32) * on).astype(BF16)
        return carry

    lax.fori_loop(0, nc, step, 0, unroll=4)


def _retention(qr, kr, vr, gr):
    B, T, _ = qr.shape
    C = RET_CHUNK
    lg = jnp.log(1.0 - 2.0 ** (-5.0 - jnp.arange(RET_HEADS, dtype=F32)))
    n = jnp.arange(C, dtype=F32)
    diff = n[:, None] - n[None, :]
    dmask = jnp.where(diff >= 0, jnp.exp(jnp.maximum(diff, 0.0)[None] * lg[:, None, None]), 0.0)
    xi = jnp.exp((n + 1.0)[None] * lg[:, None])[:, :, None]
    zeta = jnp.exp((C - 1.0 - n)[None] * lg[:, None])[:, :, None]
    gch = jnp.broadcast_to(jnp.exp(C * lg)[:, None, None], (RET_HEADS, 1, LANES))
    qk_spec = pl.BlockSpec((1, T, RET_DK), lambda b, h: (b, 0, h))
    v_spec = pl.BlockSpec((1, T, RET_DV), lambda b, h: (b, 0, h))
    return pl.pallas_call(
        _retention_kernel,
        grid=(B, RET_HEADS),
        in_specs=[qk_spec, qk_spec, v_spec, v_spec,
                  pl.BlockSpec((1, C, C), lambda b, h: (h, 0, 0)),
                  pl.BlockSpec((1, C, 1), lambda b, h: (h, 0, 0)),
                  pl.BlockSpec((1, C, 1), lambda b, h: (h, 0, 0)),
                  pl.BlockSpec((1, 1, LANES), lambda b, h: (h, 0, 0))],
        out_specs=v_spec,
        out_shape=jax.ShapeDtypeStruct((B, T, RET_HEADS * RET_DV), BF16),
        scratch_shapes=[pltpu.VMEM((RET_DK, RET_DV), F32)],
        compiler_params=pltpu.CompilerParams(vmem_limit_bytes=32 * MIB),
        name="retention",
    )(qr, kr, vr, gr, dmask, xi, zeta, gch)


def _gelu_tanh(x):
    return 0.5 * x * (1.0 + jnp.tanh(math.sqrt(2.0 / math.pi) * (x + 0.044715 * (x * x * x))))


def _compress_kernel(x_ref, pk_ref, pv_ref, w1k_ref, w2k_ref, w1v_ref, w2v_ref, ko_ref, vo_ref):
    half = CMP_STRIDE * NSA_HD

    def mlp(s, g, p_ref, w1_ref, w2_ref):
        x = x_ref[s, 0, g].astype(F32)
        a = _dot((x + p_ref[0:1, :]).astype(BF16), w1_ref[0:half, :])
        b = _dot((x + p_ref[1:2, :]).astype(BF16), w1_ref[half:2 * half, :])
        pre = a + pltpu.roll(b, b.shape[0] - 1, 0)
        return _dot(_gelu_tanh(pre).astype(BF16), w2_ref[...]).astype(BF16)

    for g in range(NSA_KV_GROUPS):
        ko_ref[0, g] = mlp(0, g, pk_ref, w1k_ref, w2k_ref)
        v = mlp(1, g, pv_ref, w1v_ref, w2v_ref)
        vo_ref[0, g, :, :NSA_HD] = v
        vo_ref[0, g, :, NSA_HD:] = v


def _compress(kvc16, pos_k, pos_v, w1k, w2k, w1v, w2v):
    _, B, G, NC, F = kvc16.shape
    full = lambda a: pl.BlockSpec(a.shape, lambda b: (0,) * a.ndim)
    o_spec = lambda width: pl.BlockSpec((1, G, NC, width), lambda b: (b, 0, 0, 0))
    o_shape = lambda width: jax.ShapeDtypeStruct((B, G, NC, width), BF16)
    pk = pos_k.reshape(2, F)
    pv = pos_v.reshape(2, F)
    w = [a.astype(BF16) for a in (w1k, w2k, w1v, w2v)]
    return pl.pallas_call(
        _compress_kernel,
        grid=(B,),
        in_specs=[pl.BlockSpec((2, 1, G, NC, F), lambda b: (0, b, 0, 0, 0)), full(pk), full(pv)]
        + [full(a) for a in w],
        out_specs=(o_spec(NSA_HD), o_spec(2 * NSA_HD)),
        out_shape=(o_shape(NSA_HD), o_shape(2 * NSA_HD)),
        compiler_params=pltpu.CompilerParams(vmem_limit_bytes=32 * MIB),
        name="nsa_compress",
    )(kvc16, pk, pv, *w)


def _bias_kernel(tab_ref, nb_ref, cb_ref, near_o, cmp_o):
    h = pl.program_id(0)
    far = tab_ref[REL_BUCKETS - 1, h]

    def build(bk):
        acc = jnp.zeros(bk.shape, F32)
        for k in range(REL_BUCKETS - 1):
            acc = jnp.where(bk == k, tab_ref[k, h] - far, acc)
        return jnp.where(bk == MASKED_CODE, NEG_INF, acc)

    near_o[0] = build(nb_ref[...])
    cmp_o[0] = build(cb_ref[...])


def _t5_bucket(dist):
    dist = jnp.maximum(dist, 0)
    max_exact = REL_BUCKETS // 2
    large = max_exact + (jnp.log(jnp.maximum(dist, 1).astype(F32) / max_exact)
                         / math.log(REL_MAX_DIST / max_exact) * (REL_BUCKETS - max_exact)).astype(jnp.int32)
    large = jnp.minimum(large, REL_BUCKETS - 1)
    return jnp.where(dist < max_exact, dist, large)


def _bias_tables(rel_bias, T, tq):
    nc = T // CMP_STRIDE
    i = jnp.arange(tq, dtype=jnp.int32)
    c = jnp.arange(2 * tq, dtype=jnp.int32)
    d = jnp.asarray(NEAR_OFFSETS, jnp.int32)[:, None, None] * tq + i[None, :, None] - c[None, None, :]
    near_b = jnp.where((d < 0) | (d >= WINDOW), MASKED_CODE, _t5_bucket(d))
    cmp_end = jnp.arange(nc, dtype=jnp.int32) * CMP_STRIDE + CMP_LEN - 1
    cmp_b = _t5_bucket(jnp.arange(T, dtype=jnp.int32)[:, None] - cmp_end[None, :])
    return pl.pallas_call(
        _bias_kernel,
        grid=(NSA_HEADS,),
        in_specs=[pl.BlockSpec(memory_space=pltpu.SMEM),
                  pl.BlockSpec(near_b.shape, lambda h: (0, 0, 0)),
                  pl.BlockSpec(cmp_b.shape, lambda h: (0, 0))],
        out_specs=(pl.BlockSpec((1,) + near_b.shape, lambda h: (h, 0, 0, 0)),
                   pl.BlockSpec((1,) + cmp_b.shape, lambda h: (h, 0, 0))),
        out_shape=(jax.ShapeDtypeStruct((NSA_HEADS,) + near_b.shape, F32),
                   jax.ShapeDtypeStruct((NSA_HEADS,) + cmp_b.shape, F32)),
        name="rel_bias_tables",
    )(rel_bias, near_b, cmp_b)


def _nsa_kernel(q_ref, kc_ref, vc_ref, ks_ref, vs_ref, kw_ref, vw_ref, gt_ref,
                near_ref, cb_ref, ovl_ref, gexp_ref, o_ref,
                qa_scr, s_scr, sw_scr, mx_scr, mb_scr, acc_scr, os_scr, *, tq, sel_k):
    HG = NSA_HG
    R = HG * tq
    T = ks_ref.shape[2]
    NC = kc_ref.shape[2]
    NS = ovl_ref.shape[0]
    qi = pl.program_id(2)
    s0 = pl.multiple_of(qi * tq, tq)
    q = q_ref[0, 0].reshape(R, NSA_HD)

    lc = _dot_nt(q, kc_ref[0, 0]).reshape(HG, tq, NC) + cb_ref[:, pl.ds(s0, tq), :]
    pos_c = s0 + lax.broadcasted_iota(jnp.int32, (tq, NC), 0)
    cend = lax.broadcasted_iota(jnp.int32, (tq, NC), 1) * CMP_STRIDE + (CMP_LEN - 1)
    valid_c = cend <= pos_c
    z = jnp.where(valid_c[None], lc, NEG_INF)
    z = z - jnp.max(z, -1, keepdims=True)
    e = jnp.exp(z) * valid_c.astype(F32)[None]
    pc = e / jnp.maximum(jnp.sum(e, -1, keepdims=True), 1e-30)
    oc = _dot(pc.reshape(R, NC).astype(BF16), vc_ref[0, 0])
    psum = jnp.sum(pc, axis=0)

    p_hi = psum.astype(BF16)
    r1 = psum - p_hi.astype(F32)
    p_mid = r1.astype(BF16)
    p_lo = (r1 - p_mid.astype(F32)).astype(BF16)
    ovl = ovl_ref[...]
    imp = _dot_nt(ovl, p_hi) + _dot_nt(ovl, p_mid) + _dot_nt(ovl, p_lo)
    jrow = lax.broadcasted_iota(jnp.int32, (NS, tq), 0)
    posl = s0 + lax.broadcasted_iota(jnp.int32, (NS, tq), 1)
    cur = lax.shift_right_logical(posl, int(math.log2(SEL_BLOCK)))
    forced = (jrow == 0) | (jrow == cur) | (jrow == cur - 1)
    imp = jnp.where(forced, FORCED_SCORE, imp)
    imp = jnp.where(jrow * SEL_BLOCK <= posl, imp, NEG_INF)
    sel = jnp.zeros((NS, tq), F32)
    for _ in range(sel_k):
        mx = jnp.max(imp, axis=0, keepdims=True)
        first = jnp.min(jnp.where(imp == mx, jrow, NS), axis=0, keepdims=True)
        hit = jrow == first
        sel = jnp.where(hit, 1.0, sel)
        imp = jnp.where(hit, PICKED, imp)
    sel_sq = jnp.concatenate([sel, jnp.zeros((tq - NS, tq), F32)], axis=0) if NS < tq else sel
    notsel = pltpu.roll(1.0 - sel_sq.T, NSA_HD, 1).astype(BF16)
    for hg in range(HG):
        qa_scr[hg * tq:(hg + 1) * tq, :NSA_HD] = q_ref[0, 0, hg]
        qa_scr[hg * tq:(hg + 1) * tq, NSA_HD:] = notsel[:, NSA_HD:]
    qa = qa_scr[...]

    UW = 2 * tq
    nu = lax.shift_right_logical(qi, 1) + 1
    odd = lax.rem(qi, 2) == 1
    last_tab = jnp.where(odd, 0, 1)
    prev_tab = jnp.where(odd, 5, 2)

    def units(ref, u, n=1):
        return ref[0, 0, pl.ds(pl.multiple_of(u * UW, UW), n * UW), :]

    def cols(slot, n=1):
        return pl.ds(pl.multiple_of(slot * UW, UW), n * UW)

    def tiles(x):
        return [x[:, c * LANES:(c + 1) * LANES] for c in range(x.shape[1] // LANES)]

    def reset():
        mx_scr[...] = jnp.full(mx_scr.shape, NEG_INF, F32)
        acc_scr[...] = jnp.zeros_like(acc_scr)

    def logits(s, s_ref, slot):
        n = s.shape[1] // UW
        s_ref[:, cols(slot, n)] = s
        mx_scr[...] = functools.reduce(jnp.maximum, tiles(s), mx_scr[...])

    def biased(s, e):
        return (s.reshape(HG, tq, UW) + near_ref[:, e]).reshape(R, UW)

    def row_max():
        mb_scr[...] = jnp.broadcast_to(jnp.max(mx_scr[...], axis=1, keepdims=True), mb_scr.shape)

    def weigh(s_ref, slot, v_ref, u, n=1):
        m = mb_scr[...]
        p = jnp.concatenate([jnp.exp(t - m) for t in tiles(s_ref[:, cols(slot, n)])], axis=1)
        acc_scr[...] += _dot(p.astype(BF16), units(v_ref, u, n))

    def normalized():
        acc = acc_scr[...]
        return acc[:, :LANES] / acc[:, LANES:]

    lane = lax.broadcasted_iota(jnp.int32, (tq, LANES), 1)

    def dense(x):
        return jnp.concatenate([jnp.where(lane < NSA_HD, x[2 * j * tq:(2 * j + 1) * tq],
                                          x[(2 * j + 1) * tq:(2 * j + 2) * tq]) for j in range(HG // 2)], axis=1)

    reset()
    logits(biased(_dot_nt(qa, units(ks_ref, nu - 1)), last_tab), s_scr, nu - 1)

    @pl.when(nu >= 2)
    def _():
        logits(biased(_dot_nt(qa, units(ks_ref, nu - 2)), prev_tab), s_scr, nu - 2)

    n_plain = jnp.maximum(nu - 2, 0)

    def plain_pair(i, carry):
        logits(_dot_nt(qa, units(ks_ref, 2 * i, 2)), s_scr, 2 * i)
        return carry

    lax.fori_loop(0, lax.shift_right_logical(n_plain, 1), plain_pair, 0)

    @pl.when(lax.rem(n_plain, 2) == 1)
    def _():
        logits(_dot_nt(qa, units(ks_ref, n_plain - 1)), s_scr, n_plain - 1)

    row_max()

    def weigh_pair(i, carry):
        weigh(s_scr, 2 * i, vs_ref, 2 * i, 2)
        return carry

    lax.fori_loop(0, lax.shift_right_logical(nu, 1), weigh_pair, 0)

    @pl.when(lax.rem(nu, 2) == 1)
    def _():
        weigh(s_scr, nu - 1, vs_ref, nu - 1)

    os_scr[...] = normalized()

    reset()
    logits(biased(_dot_nt(q, units(kw_ref, nu - 1)), last_tab), sw_scr, 2)

    @pl.when(nu == 2)
    def _():
        logits(biased(_dot_nt(q, units(kw_ref, 0)), prev_tab), sw_scr, 1)

    @pl.when(nu >= 3)
    def _():
        s = _dot_nt(q, units(kw_ref, nu - 3, 2))
        logits(jnp.concatenate([biased(s[:, :UW], jnp.where(odd, 4, 3)), biased(s[:, UW:], prev_tab)], axis=1),
               sw_scr, 0)

    row_max()
    weigh(sw_scr, 2, vw_ref, nu - 1)

    @pl.when(nu == 2)
    def _():
        weigh(sw_scr, 1, vw_ref, 0)

    @pl.when(nu >= 3)
    def _():
        weigh(sw_scr, 0, vw_ref, nu - 3, 2)

    gt = gt_ref[0, 0]
    g_hi = gt.astype(BF16)
    g_lo = (gt - g_hi.astype(F32)).astype(BF16)
    ge = _dot(jnp.concatenate([g_hi, g_lo], axis=1), gexp_ref[...])
    W = HG * NSA_HD
    o = ge[:, :W] * dense(oc) + ge[:, W:2 * W] * dense(os_scr[...]) + ge[:, 2 * W:] * dense(normalized())
    o_ref[0] = o.astype(BF16)


def _nsa_attention(qn, kc, vc, ks, vs, kw, vw, gates, near, cmpb):
    B, G, HG, T, hd = qn.shape
    tq = TQ
    assert T % (2 * tq) == 0 and WINDOW == 4 * tq and tq >= REL_MAX_DIST and tq == LANES
    nq = T // tq
    NC = kc.shape[2]
    NS = T // SEL_BLOCK
    sel_k = min(SEL_TOPK, NS)
    n = np.arange(NC)
    j = np.arange(NS)
    ovl = ((n[None, :] * CMP_STRIDE < (j[:, None] + 1) * SEL_BLOCK)
           & (n[None, :] * CMP_STRIDE + CMP_LEN - 1 >= j[:, None] * SEL_BLOCK) & (n[None, :] < NC - 1))
    ovl = jnp.asarray(ovl, BF16)
    W = HG * hd
    gexp = np.zeros((2 * LANES, 3 * W), np.float32)
    for hg in range(HG):
        for r in range(3):
            for part in range(2):
                gexp[part * LANES + 3 * hg + r, r * W + hg * hd:r * W + (hg + 1) * hd] = 1.0
    gexp = jnp.asarray(gexp, BF16)
    seq = lambda rows, width: pl.BlockSpec((1, 1, rows, width), lambda g, b, t: (b, g, 0, 0))
    R = HG * tq
    return pl.pallas_call(
        functools.partial(_nsa_kernel, tq=tq, sel_k=sel_k),
        grid=(G, B, nq),
        in_specs=[pl.BlockSpec((1, 1, HG, tq, hd), lambda g, b, t: (b, g, 0, t, 0)),
                  seq(NC, hd), seq(NC, LANES), seq(T, LANES), seq(T, 2 * LANES), seq(T, hd), seq(T, 2 * LANES),
                  pl.BlockSpec((1, 1, tq, LANES), lambda g, b, t: (b, g, t, 0)),
                  pl.BlockSpec((HG, len(NEAR_OFFSETS), tq, 2 * tq), lambda g, b, t: (g, 0, 0, 0),
                               pipeline_mode=pl.Buffered(1)),
                  pl.BlockSpec((HG, T, NC), lambda g, b, t: (g, 0, 0), pipeline_mode=pl.Buffered(1)),
                  pl.BlockSpec(ovl.shape, lambda g, b, t: (0, 0)),
                  pl.BlockSpec(gexp.shape, lambda g, b, t: (0, 0))],
        out_specs=pl.BlockSpec((1, tq, W), lambda g, b, t: (b, t, g)),
        out_shape=jax.ShapeDtypeStruct((B, T, G * W), BF16),
        scratch_shapes=[pltpu.VMEM((R, LANES), BF16),
                        pltpu.VMEM((R, T), F32), pltpu.VMEM((R, 6 * tq), F32),
                        pltpu.VMEM((R, LANES), F32), pltpu.VMEM((R, LANES), F32),
                        pltpu.VMEM((R, 2 * LANES), F32), pltpu.VMEM((R, LANES), F32)],
        compiler_params=pltpu.CompilerParams(vmem_limit_bytes=52 * MIB),
        name="nsa_attention",
    )(qn, kc, vc, ks, vs, kw, vw, gates, near, cmpb, ovl, gexp)


def _merge_kernel(x_ref, oret_ref, on_ref, ma_ref, mb_ref, wr_ref, wn_ref, wo_ref, o_ref):
    y_ret = _dot(oret_ref[...], wr_ref[...])
    y_nsa = _dot(on_ref[...], wn_ref[...])
    mixed = ma_ref[...].astype(F32) * y_ret + mb_ref[...].astype(F32) * y_nsa
    o_ref[...] = x_ref[...] + _dot(mixed.astype(BF16), wo_ref[...])


def _merge(x2, oret, on, ma, mb, w_o_ret, w_o_nsa, w_out):
    M, D = x2.shape
    tm = min(TM_MERGE, M)
    row = lambda width: pl.BlockSpec((tm, width), lambda i: (i, 0))
    res = lambda a: pl.BlockSpec(a.shape, lambda i: (0, 0), pipeline_mode=pl.Buffered(1))
    w = [a.astype(BF16) for a in (w_o_ret, w_o_nsa, w_out)]
    return pl.pallas_call(
        _merge_kernel,
        grid=(M // tm,),
        in_specs=[row(D), row(oret.shape[1]), row(D), row(D), row(D)] + [res(a) for a in w],
        out_specs=row(D),
        out_shape=jax.ShapeDtypeStruct((M, D), F32),
        compiler_params=pltpu.CompilerParams(vmem_limit_bytes=48 * MIB),
        name="merge_out_projection",
    )(x2, oret, on, ma, mb, *w)


def _rms(x, g):
    return x * lax.rsqrt(jnp.mean(x * x, -1, keepdims=True) + EPS) * g


def _ffn_kernel(x_ref, g_ref, wi_ref, wo_ref, gf_ref, o_ref, h_scr, acc_scr, *, final_norm):
    x = x_ref[...]
    h_scr[...] = _rms(x, g_ref[...]).astype(BF16)
    acc_scr[...] = x
    cw = FFN_HIDDEN // 2
    for c in range(2):
        a = _dot(h_scr[...], wi_ref[:, c * cw:(c + 1) * cw])
        b = _dot(h_scr[...], wi_ref[:, FFN_HIDDEN + c * cw:FFN_HIDDEN + (c + 1) * cw])
        acc_scr[...] += _dot((_silu(a) * b).astype(BF16), wo_ref[c * cw:(c + 1) * cw, :])
    y = acc_scr[...]
    o_ref[...] = _rms(y, gf_ref[...]) if final_norm else y


def _ffn(x2, g, w_in, w_out, g_final, final_norm):
    M, D = x2.shape
    tm = min(TM_MERGE, M)
    row = pl.BlockSpec((tm, D), lambda i: (i, 0))
    vec = pl.BlockSpec((1, D), lambda i: (0, 0))
    res = lambda a: pl.BlockSpec(a.shape, lambda i: (0, 0), pipeline_mode=pl.Buffered(1))
    wi, wo = w_in.astype(BF16), w_out.astype(BF16)
    return pl.pallas_call(
        functools.partial(_ffn_kernel, final_norm=final_norm),
        grid=(M // tm,),
        in_specs=[row, vec, res(wi), res(wo), vec],
        out_specs=row,
        out_shape=jax.ShapeDtypeStruct((M, D), F32),
        scratch_shapes=[pltpu.VMEM((tm, D), BF16), pltpu.VMEM((tm, D), F32)],
        compiler_params=pltpu.CompilerParams(vmem_limit_bytes=52 * MIB),
        name="swiglu_ffn",
    )(x2, g.reshape(1, D), wi, wo, g_final.reshape(1, D))


def kernel(x, norm_mix_g, w_in, cmp_pos_k, cmp_pos_v, cmp_w1_k, cmp_w2_k, cmp_w1_v, cmp_w2_v, w_o_ret, w_o_nsa,
           w_out, norm_ffn_g, w_ffn_in, w_ffn_out, rel_bias, norm_final_g):
    B, T, D = x.shape
    depth = w_in.shape[0]
    G, HG = NSA_KV_GROUPS, NSA_HG
    half = RET_DK // 2
    freqs = ROPE_BASE ** (-jnp.arange(half, dtype=F32) / half)
    ang = jnp.arange(T, dtype=jnp.int32).astype(F32)[:, None] * freqs
    cos, sin = jnp.cos(ang), jnp.sin(ang)
    near, cmpb = _bias_tables(rel_bias, T, min(TQ, T))
    for i in range(depth):
        qr, kr, vr, gr, qn, kvc, ks, vs, kw, vw, gates, ma, mb = _in_projection(
            x, norm_mix_g[i], _pack_w_in(w_in[i]), cos, sin)
        o_ret = _retention(qr, kr, vr, gr)
        kvc16 = kvc.reshape(2, B, G, T // CMP_STRIDE, CMP_STRIDE * NSA_HD)
        kc, vc = _compress(kvc16, cmp_pos_k[i], cmp_pos_v[i], cmp_w1_k[i], cmp_w2_k[i], cmp_w1_v[i], cmp_w2_v[i])
        o_n = _nsa_attention(qn.reshape(B, G, HG, T, NSA_HD), kc, vc, ks, vs, kw, vw, gates, near, cmpb)
        o_n = o_n.reshape(B * T, NSA_HEADS * NSA_HD)
        x2 = _merge(x.reshape(B * T, D), o_ret.reshape(B * T, -1), o_n, ma.reshape(B * T, D), mb.reshape(B * T, D),
                    w_o_ret[i], w_o_nsa[i], w_out[i])
        x2 = _ffn(x2, norm_ffn_g[i], w_ffn_in[i], w_ffn_out[i], norm_final_g, final_norm=(i == depth - 1))
        x = x2.reshape(B, T, D)
    return x
```

```python
import functools
import math

import jax
import jax.numpy as jnp
import numpy as np
from jax import lax
from jax.experimental import pallas as pl
from jax.experimental.pallas import tpu as pltpu

F32 = jnp.float32
BF16 = jnp.bfloat16

D_MODEL = 1024
RET_HEADS = 4
RET_DK = 256
RET_DV = 512
RET_BLOCK = 256
ROPE_BASE = 10000.0
GN_EPS = 1e-5
NSA_HEADS = 16
NSA_KV_GROUPS = 2
NSA_HG = NSA_HEADS // NSA_KV_GROUPS
NSA_HD = 64
CMP_LEN = 32
CMP_STRIDE = 16
CMP_HIDDEN = 256
SEL_BLOCK = 64
SEL_TOPK = 8
WINDOW = 512
FORCED_SCORE = 1e4
REL_BUCKETS = 32
REL_MAX_DIST = 128
FFN_HIDDEN = -(-8 * D_MODEL // (3 * 256)) * 256
EPS = 1e-6
NEG_INF = -1e30
PICKED = -3e38
MASKED_CODE = REL_BUCKETS
NEAR_OFFSETS = (0, 1, 2)
LOG2E = math.log2(math.e)

LANES = 128
MIB = 1024 * 1024
NSA_VMEM_LIMIT = 56 * MIB

_SEG = np.cumsum([0, RET_HEADS * RET_DK, RET_HEADS * RET_DK, RET_HEADS * RET_DV, RET_HEADS * RET_DV,
                  NSA_HEADS * NSA_HD, 6 * NSA_KV_GROUPS * NSA_HD, 3 * NSA_HEADS, D_MODEL, D_MODEL])
N_QR, N_KR, N_VR, N_GR = RET_HEADS * RET_DK, RET_HEADS * RET_DK, RET_HEADS * RET_DV, RET_HEADS * RET_DV
N_QN, N_KV = NSA_HEADS * NSA_HD, 6 * NSA_KV_GROUPS * NSA_HD
N_GATE = NSA_KV_GROUPS * LANES
_P = np.cumsum([0, N_QR, N_KR, N_VR, N_GR, N_QN, N_KV, N_GATE, D_MODEL, D_MODEL])
P_QR, P_KR, P_VR, P_GR, P_QN, P_KV, P_GATE, P_MA, P_MB, P_END = [int(v) for v in _P]

TM_PROJ = 512
TM_MERGE = 512
TQ = 256
NT = (((1,), (1,)), ((), ()))


def _dot(a, b):
    return jnp.dot(a, b, preferred_element_type=F32)


def _dot_nt(a, b):
    return lax.dot_general(a, b, NT, preferred_element_type=F32)


def _sigmoid(x):
    return 1.0 / (1.0 + jnp.exp(-x))


def _silu(x):
    return x * _sigmoid(x)


def _inproj_kernel(x_ref, g_ref, cos_ref, sin_ref, feat_ref, w_ref,
                   qr_o, kr_o, vr_o, gr_o, qn_o, kvc_o, ks_o, vs_o, kw_o, vw_o, gate_o, ma_o, mb_o, h_scr):
    x = x_ref[0]
    h = x * lax.rsqrt(jnp.mean(x * x, -1, keepdims=True) + EPS) * g_ref[...]
    h_scr[...] = h.astype(BF16)
    cos = cos_ref[...]
    sin = sin_ref[...]
    half = RET_DK // 2

    def proj(lo, width):
        return _dot(h_scr[...], w_ref[:, lo:lo + width])

    for hd in range(RET_HEADS):
        for out, base, scale in ((qr_o, P_QR, 1.0), (kr_o, P_KR, RET_DK ** -0.5)):
            y = proj(base + hd * RET_DK, RET_DK)
            x1, x2 = y[:, :half], y[:, half:]
            out[0, :, hd * RET_DK:hd * RET_DK + half] = ((x1 * cos - x2 * sin) * scale).astype(BF16)
            out[0, :, hd * RET_DK + half:(hd + 1) * RET_DK] = ((x1 * sin + x2 * cos) * scale).astype(BF16)
    cw = 512
    for c in range(N_VR // cw):
        vr_o[0, :, c * cw:(c + 1) * cw] = proj(P_VR + c * cw, cw).astype(BF16)
        gr_o[0, :, c * cw:(c + 1) * cw] = _silu(proj(P_GR + c * cw, cw)).astype(BF16)
    for c in range(D_MODEL // cw):
        ma_o[0, :, c * cw:(c + 1) * cw] = _sigmoid(proj(P_MA + c * cw, cw)).astype(BF16)
        mb_o[0, :, c * cw:(c + 1) * cw] = _sigmoid(proj(P_MB + c * cw, cw)).astype(BF16)
    per = cw // NSA_HD
    for c in range(N_QN // cw):
        y = (proj(P_QN + c * cw, cw) * (NSA_HD ** -0.5 * LOG2E)).astype(BF16)
        for j in range(per):
            qn_o[0, c * per + j] = y[:, j * NSA_HD:(j + 1) * NSA_HD]
    y = proj(P_KV, N_KV).astype(BF16)
    ones = jnp.ones((y.shape[0], LANES), BF16)
    for g in range(NSA_KV_GROUPS):
        piece = lambda j: y[:, (j * NSA_KV_GROUPS + g) * NSA_HD:(j * NSA_KV_GROUPS + g + 1) * NSA_HD]
        kvc_o[0, 0, g] = piece(0)
        kvc_o[1, 0, g] = piece(1)
        ks_o[0, g, :, :NSA_HD] = piece(2)
        ks_o[0, g, :, NSA_HD:] = feat_ref[:, NSA_HD:]
        kw_o[0, g] = piece(4)
        for out, j in ((vs_o, 3), (vw_o, 5)):
            out[0, g, :, :NSA_HD] = piece(j)
            out[0, g, :, NSA_HD:LANES] = piece(j)
            out[0, g, :, LANES:] = ones
    y = _sigmoid(proj(P_GATE, N_GATE))
    for g in range(NSA_KV_GROUPS):
        gate_o[0, g] = y[:, g * LANES:(g + 1) * LANES]


def _pack_w_in(w):
    s = [int(v) for v in _SEG]
    gate = w[:, s[6]:s[7]].reshape(D_MODEL, NSA_KV_GROUPS, 3 * NSA_HG)
    gate = jnp.pad(gate, ((0, 0), (0, 0), (0, LANES - 3 * NSA_HG))).reshape(D_MODEL, N_GATE)
    return jnp.concatenate([w[:, s[0]:s[6]], gate, w[:, s[7]:s[9]]], axis=1).astype(BF16)


def _in_projection(x, g, w_packed, cos, sin):
    B, T, D = x.shape
    G = NSA_KV_GROUPS
    tm = min(TM_PROJ, T)
    nt = T // tm
    assert T // SEL_BLOCK <= LANES - NSA_HD
    own_block = (np.arange(T)[:, None] // SEL_BLOCK) == (np.arange(LANES)[None, :] - NSA_HD)
    feat = jnp.asarray(np.where(own_block, NEG_INF, 0.0), BF16)
    row = lambda width: pl.BlockSpec((1, tm, width), lambda b, t: (b, t, 0))
    grp = lambda width: pl.BlockSpec((1, G, tm, width), lambda b, t: (b, 0, t, 0))
    grp_shape = lambda width, dt=BF16: jax.ShapeDtypeStruct((B, G, T, width), dt)
    out_shape = (
        jax.ShapeDtypeStruct((B, T, N_QR), BF16), jax.ShapeDtypeStruct((B, T, N_KR), BF16),
        jax.ShapeDtypeStruct((B, T, N_VR), BF16), jax.ShapeDtypeStruct((B, T, N_GR), BF16),
        jax.ShapeDtypeStruct((B, NSA_HEADS, T, NSA_HD), BF16),
        jax.ShapeDtypeStruct((2, B, G, T, NSA_HD), BF16),
        grp_shape(LANES), grp_shape(2 * LANES), grp_shape(NSA_HD), grp_shape(2 * LANES),
        grp_shape(LANES, F32),
        jax.ShapeDtypeStruct((B, T, D_MODEL), BF16), jax.ShapeDtypeStruct((B, T, D_MODEL), BF16))
    out_specs = (
        row(N_QR), row(N_KR), row(N_VR), row(N_GR),
        pl.BlockSpec((1, NSA_HEADS, tm, NSA_HD), lambda b, t: (b, 0, t, 0)),
        pl.BlockSpec((2, 1, G, tm, NSA_HD), lambda b, t: (0, b, 0, t, 0)),
        grp(LANES), grp(2 * LANES), grp(NSA_HD), grp(2 * LANES),
        grp(LANES),
        row(D_MODEL), row(D_MODEL))
    out_bytes = tm * (2 * (N_QR + N_KR + N_VR + N_GR + N_QN + 2 * D_MODEL) + 2 * G * (3 * NSA_HD + 5 * LANES)
                      + 4 * N_GATE)
    vmem = 2 * out_bytes + 2 * tm * D * 4 + 2 * D * P_END + tm * D * 2 + 8 * MIB
    return pl.pallas_call(
        _inproj_kernel,
        grid=(B, nt),
        in_specs=[pl.BlockSpec((1, tm, D), lambda b, t: (b, t, 0)),
                  pl.BlockSpec((1, D), lambda b, t: (0, 0)),
                  pl.BlockSpec((tm, RET_DK // 2), lambda b, t: (t, 0)),
                  pl.BlockSpec((tm, RET_DK // 2), lambda b, t: (t, 0)),
                  pl.BlockSpec((tm, LANES), lambda b, t: (t, 0)),
                  pl.BlockSpec((D, P_END), lambda b, t: (0, 0), pipeline_mode=pl.Buffered(1))],
        out_specs=out_specs,
        out_shape=out_shape,
        scratch_shapes=[pltpu.VMEM((tm, D), BF16)],
        compiler_params=pltpu.CompilerParams(vmem_limit_bytes=int(vmem)),
        name="in_projection",
    )(x, g.reshape(1, D), cos, sin, feat, w_packed)


def _retention_kernel(q_ref, k_ref, v_ref, g_ref, dmask_ref, xi_ref, zeta_ref, gch_ref, o_ref):
    C = RET_BLOCK
    nc = q_ref.shape[1] // C
    dmask = dmask_ref[0]
    xi = xi_ref[0]
    zeta = zeta_ref[0]
    gch = gch_ref[0, :, 0:1]
    r = jnp.zeros((RET_DK, RET_DV), F32)
    for c in range(nc):
        rows = slice(c * C, (c + 1) * C)
        qc = q_ref[0, rows, :]
        kc = k_ref[0, rows, :]
        vc = v_ref[0, rows, :]
        s = _dot_nt(qc, kc) * dmask
        o = _dot(s.astype(BF16), vc)
        if c > 0:
            o = o + _dot(qc, r.astype(BF16)) * xi
        if c < nc - 1:
            kz = (kc.astype(F32) * zeta).astype(BF16)
            kv = lax.dot_general(kz, vc, (((0,), (0,)), ((), ())), preferred_element_type=F32)
            r = kv if c == 0 else gch * r + kv
        mu = jnp.mean(o, -1, keepdims=True)
        d = o - mu
        var = jnp.mean(d * d, -1, keepdims=True)
        on = d * lax.rsqrt(var + GN_EPS)
        o_ref[0, rows, :] = (g_ref[0, rows, :].astype(F32) * on).astype(BF16)


def _retention(qr, kr, vr, gr):
    B, T, _ = qr.shape
    C = RET_BLOCK
    lg = jnp.log(1.0 - 2.0 ** (-5.0 - jnp.arange(RET_HEADS, dtype=F32)))
    n = jnp.arange(C, dtype=F32)
    diff = n[:, None] - n[None, :]
    dmask = jnp.where(diff >= 0, jnp.exp(jnp.maximum(diff, 0.0)[None] * lg[:, None, None]), 0.0)
    xi = jnp.exp((n + 1.0)[None] * lg[:, None])[:, :, None]
    zeta = jnp.exp((C - 1.0 - n)[None] * lg[:, None])[:, :, None]
    gch = jnp.broadcast_to(jnp.exp(C * lg)[:, None, None], (RET_HEADS, 1, LANES))
    qk_spec = pl.BlockSpec((1, T, RET_DK), lambda b, h: (b, 0, h))
    v_spec = pl.BlockSpec((1, T, RET_DV), lambda b, h: (b, 0, h))
    return pl.pallas_call(
        _retention_kernel,
        grid=(B, RET_HEADS),
        in_specs=[qk_spec, qk_spec, v_spec, v_spec,
                  pl.BlockSpec((1, C, C), lambda b, h: (h, 0, 0)),
                  pl.BlockSpec((1, C, 1), lambda b, h: (h, 0, 0)),
                  pl.BlockSpec((1, C, 1), lambda b, h: (h, 0, 0)),
                  pl.BlockSpec((1, 1, LANES), lambda b, h: (h, 0, 0))],
        out_specs=v_spec,
        out_shape=jax.ShapeDtypeStruct((B, T, RET_HEADS * RET_DV), BF16),
        compiler_params=pltpu.CompilerParams(vmem_limit_bytes=32 * MIB),
        name="retention",
    )(qr, kr, vr, gr, dmask, xi, zeta, gch)


def _gelu_tanh(x):
    return 0.5 * x * (1.0 + jnp.tanh(math.sqrt(2.0 / math.pi) * (x + 0.044715 * (x * x * x))))


def _compress_kernel(x_ref, pk_ref, pv_ref, w1k_ref, w2k_ref, w1v_ref, w2v_ref, ko_ref, vo_ref):
    half = CMP_STRIDE * NSA_HD

    def mlp(s, g, p_ref, w1_ref, w2_ref):
        x = x_ref[s, 0, g].astype(F32)
        a = _dot((x + p_ref[0:1, :]).astype(BF16), w1_ref[0:half, :])
        b = _dot((x + p_ref[1:2, :]).astype(BF16), w1_ref[half:2 * half, :])
        pre = a + pltpu.roll(b, b.shape[0] - 1, 0)
        return _dot(_gelu_tanh(pre).astype(BF16), w2_ref[...]).astype(BF16)

    for g in range(NSA_KV_GROUPS):
        ko_ref[0, g] = mlp(0, g, pk_ref, w1k_ref, w2k_ref)
        v = mlp(1, g, pv_ref, w1v_ref, w2v_ref)
        vo_ref[0, g, :, :NSA_HD] = v
        vo_ref[0, g, :, NSA_HD:] = v


def _compress(kvc16, pos_k, pos_v, w1k, w2k, w1v, w2v):
    _, B, G, NC, F = kvc16.shape
    full = lambda a: pl.BlockSpec(a.shape, lambda b: (0,) * a.ndim)
    o_spec = lambda width: pl.BlockSpec((1, G, NC, width), lambda b: (b, 0, 0, 0))
    o_shape = lambda width: jax.ShapeDtypeStruct((B, G, NC, width), BF16)
    pk = pos_k.reshape(2, F)
    pv = pos_v.reshape(2, F)
    w = [a.astype(BF16) for a in (w1k, w2k, w1v, w2v)]
    return pl.pallas_call(
        _compress_kernel,
        grid=(B,),
        in_specs=[pl.BlockSpec((2, 1, G, NC, F), lambda b: (0, b, 0, 0, 0)), full(pk), full(pv)]
        + [full(a) for a in w],
        out_specs=(o_spec(NSA_HD), o_spec(2 * NSA_HD)),
        out_shape=(o_shape(NSA_HD), o_shape(2 * NSA_HD)),
        compiler_params=pltpu.CompilerParams(vmem_limit_bytes=32 * MIB),
        name="nsa_compress",
    )(kvc16, pk, pv, *w)


def _bias_kernel(tab_ref, nb_ref, cb_ref, near_o, cmp_o):
    h = pl.program_id(0)
    far = tab_ref[REL_BUCKETS - 1, h]

    def build(bk):
        acc = jnp.zeros(bk.shape, F32)
        for k in range(REL_BUCKETS - 1):
            acc = jnp.where(bk == k, (tab_ref[k, h] - far) * LOG2E, acc)
        return jnp.where(bk == MASKED_CODE, NEG_INF, acc)

    near_o[0] = build(nb_ref[...])
    cmp_o[0] = build(cb_ref[...])


def _t5_bucket(dist):
    dist = jnp.maximum(dist, 0)
    max_exact = REL_BUCKETS // 2
    large = max_exact + (jnp.log(jnp.maximum(dist, 1).astype(F32) / max_exact)
                         / math.log(REL_MAX_DIST / max_exact) * (REL_BUCKETS - max_exact)).astype(jnp.int32)
    large = jnp.minimum(large, REL_BUCKETS - 1)
    return jnp.where(dist < max_exact, dist, large)


def _bias_tables(rel_bias, T, tq):
    nc = T // CMP_STRIDE
    i = jnp.arange(tq, dtype=jnp.int32)
    d = jnp.asarray(NEAR_OFFSETS, jnp.int32)[:, None, None] * tq + i[None, :, None] - i[None, None, :]
    near_b = jnp.where((d < 0) | (d >= WINDOW), MASKED_CODE, _t5_bucket(d))
    near_b = jnp.concatenate([near_b, jnp.full((1, tq, tq), MASKED_CODE, jnp.int32)], axis=0)
    cmp_end = jnp.arange(nc, dtype=jnp.int32) * CMP_STRIDE + CMP_LEN - 1
    cmp_b = _t5_bucket(jnp.arange(T, dtype=jnp.int32)[:, None] - cmp_end[None, :])
    return pl.pallas_call(
        _bias_kernel,
        grid=(NSA_HEADS,),
        in_specs=[pl.BlockSpec(memory_space=pltpu.SMEM),
                  pl.BlockSpec(near_b.shape, lambda h: (0, 0, 0)),
                  pl.BlockSpec(cmp_b.shape, lambda h: (0, 0))],
        out_specs=(pl.BlockSpec((1,) + near_b.shape, lambda h: (h, 0, 0, 0)),
                   pl.BlockSpec((1,) + cmp_b.shape, lambda h: (h, 0, 0))),
        out_shape=(jax.ShapeDtypeStruct((NSA_HEADS,) + near_b.shape, F32),
                   jax.ShapeDtypeStruct((NSA_HEADS,) + cmp_b.shape, F32)),
        name="rel_bias_tables",
    )(rel_bias, near_b, cmp_b)


def _nsa_kernel(q_ref, kc_ref, vc_ref, ks_ref, vs_ref, kw_ref, vw_ref, gt_ref,
                near_ref, cb_ref, ovl_ref, gexp_ref, o_ref,
                qa_scr, s_scr, sw_scr, mxs_scr, mxw_scr, accs_scr, accw_scr, *, tq, sel_k):
    HG = NSA_HG
    R = HG * tq
    NC = kc_ref.shape[2]
    NS = ovl_ref.shape[0]
    UW = tq
    ti = pl.program_id(2)
    s0 = ti * tq
    q = q_ref[0, 0].reshape(R, NSA_HD)

    def units(ref, u, n=1):
        return ref[0, 0, pl.ds(pl.multiple_of(u * UW, UW), n * UW), :]

    def cols(slot, n=1):
        return pl.ds(pl.multiple_of(slot * UW, UW), n * UW)

    def tiles(x):
        return [x[:, c * LANES:(c + 1) * LANES] for c in range(x.shape[1] // LANES)]

    def logits(s, s_ref, slot, mx_ref):
        s_ref[:, cols(slot, s.shape[1] // UW)] = s
        mx_ref[...] = functools.reduce(jnp.maximum, tiles(s), mx_ref[...])

    def biased(s, e):
        return (s.reshape(HG, tq, UW) + near_ref[:, e]).reshape(R, UW)

    def row_max(mx_ref):
        mx_ref[...] = jnp.broadcast_to(jnp.max(mx_ref[...], axis=1, keepdims=True), mx_ref.shape)

    def weigh(s_ref, slot, m_ref, v_ref, u, acc_ref, n=1):
        m = m_ref[...]
        p = jnp.concatenate([jnp.exp2((t - m).astype(BF16)) for t in tiles(s_ref[:, cols(slot, n)])], axis=1)
        acc_ref[...] += _dot(p, units(v_ref, u, n))

    def normalized(acc_ref):
        acc = acc_ref[...]
        return acc[:, :LANES] / acc[:, LANES:]

    for ref in (mxs_scr, mxw_scr):
        ref[...] = jnp.full(ref.shape, NEG_INF, F32)
    for ref in (accs_scr, accw_scr):
        ref[...] = jnp.zeros_like(ref)

    masked = len(NEAR_OFFSETS)
    w0 = jnp.maximum(ti - 2, 0)

    def window_logits(slot):
        back = jnp.minimum(ti, 2) - slot
        logits(biased(_dot_nt(q, units(kw_ref, w0 + slot)), jnp.where(back >= 0, back, masked)),
               sw_scr, slot, mxw_scr)

    lc = _dot_nt(q, kc_ref[0, 0]).reshape(HG, tq, NC) + cb_ref[...]
    pos_c = s0 + lax.broadcasted_iota(jnp.int32, (tq, NC), 0)
    cend = lax.broadcasted_iota(jnp.int32, (tq, NC), 1) * CMP_STRIDE + (CMP_LEN - 1)
    valid_c = cend <= pos_c
    z = jnp.where(valid_c[None], lc, NEG_INF)
    z = z - jnp.max(z, -1, keepdims=True)
    e = jnp.exp2(z) * valid_c.astype(F32)[None]
    pc = e / jnp.maximum(jnp.sum(e, -1, keepdims=True), 1e-30)
    oc = _dot(pc.reshape(R, NC).astype(BF16), vc_ref[0, 0])
    psum = jnp.sum(pc, axis=0)

    window_logits(0)

    p_hi = psum.astype(BF16)
    r1 = psum - p_hi.astype(F32)
    p_mid = r1.astype(BF16)
    p_lo = (r1 - p_mid.astype(F32)).astype(BF16)
    ovl = ovl_ref[...]
    imp = _dot_nt(ovl, p_hi) + _dot_nt(ovl, p_mid) + _dot_nt(ovl, p_lo)
    jrow = lax.broadcasted_iota(jnp.int32, (NS, tq), 0)
    posl = s0 + lax.broadcasted_iota(jnp.int32, (NS, tq), 1)
    cur = lax.shift_right_logical(posl, int(math.log2(SEL_BLOCK)))
    forced = (jrow == 0) | (jrow == cur) | (jrow == cur - 1)
    started = jrow * SEL_BLOCK <= posl
    imp = jnp.where(forced, FORCED_SCORE, imp)
    imp = jnp.where(started, imp, NEG_INF)
    window_logits(1)
    sel = jnp.zeros((NS, tq), F32)
    for _ in range(sel_k):
        mx = jnp.max(imp, axis=0, keepdims=True)
        first = jnp.min(jnp.where(imp == mx, jrow, NS), axis=0, keepdims=True)
        hit = jrow == first
        sel = jnp.where(hit, 1.0, sel)
        imp = jnp.where(hit, PICKED, imp)
    window_logits(2)
    notsel = 1.0 - jnp.where(started, sel, 0.0)
    notsel = jnp.concatenate([notsel, jnp.zeros((LANES - NS, tq), F32)], axis=0)
    notsel = pltpu.roll(notsel.T, NSA_HD, 1).astype(BF16)
    for hg in range(HG):
        qa_scr[hg * tq:(hg + 1) * tq, :NSA_HD] = q_ref[0, 0, hg]
        qa_scr[hg * tq:(hg + 1) * tq, NSA_HD:] = notsel[:, NSA_HD:]
    qa = qa_scr[...]

    prev_u = jnp.where(ti >= 1, ti - 1, 1)
    logits(biased(_dot_nt(qa, units(ks_ref, ti)), 0), s_scr, ti, mxs_scr)
    logits(biased(_dot_nt(qa, units(ks_ref, prev_u)), jnp.where(ti >= 1, 1, masked)), s_scr, prev_u, mxs_scr)

    n_plain = jnp.maximum(ti - 1, 0)

    def plain_pair(i, carry):
        logits(_dot_nt(qa, units(ks_ref, 2 * i, 2)), s_scr, 2 * i, mxs_scr)
        return carry

    lax.fori_loop(0, lax.shift_right_logical(n_plain, 1), plain_pair, 0)

    @pl.when(lax.rem(n_plain, 2) == 1)
    def _():
        logits(_dot_nt(qa, units(ks_ref, n_plain - 1)), s_scr, n_plain - 1, mxs_scr)

    row_max(mxs_scr)
    row_max(mxw_scr)
    weigh(sw_scr, 0, mxw_scr, vw_ref, w0, accw_scr, 3)
    weigh(s_scr, ti, mxs_scr, vs_ref, ti, accs_scr)
    weigh(s_scr, prev_u, mxs_scr, vs_ref, prev_u, accs_scr)

    def weigh_pair(i, carry):
        weigh(s_scr, 2 * i, mxs_scr, vs_ref, 2 * i, accs_scr, 2)
        return carry

    lax.fori_loop(0, lax.shift_right_logical(n_plain, 1), weigh_pair, 0)

    @pl.when(lax.rem(n_plain, 2) == 1)
    def _():
        weigh(s_scr, n_plain - 1, mxs_scr, vs_ref, n_plain - 1, accs_scr)

    lane = lax.broadcasted_iota(jnp.int32, (tq, LANES), 1)

    def dense(x):
        return jnp.concatenate([jnp.where(lane < NSA_HD, x[2 * j * tq:(2 * j + 1) * tq],
                                          x[(2 * j + 1) * tq:(2 * j + 2) * tq]) for j in range(HG // 2)], axis=1)

    gt = gt_ref[0, 0]
    g_hi = gt.astype(BF16)
    g_lo = (gt - g_hi.astype(F32)).astype(BF16)
    ge = _dot(jnp.concatenate([g_hi, g_lo], axis=1), gexp_ref[...])
    W = HG * NSA_HD
    o = (ge[:, :W] * dense(oc) + ge[:, W:2 * W] * dense(normalized(accs_scr))
         + ge[:, 2 * W:] * dense(normalized(accw_scr)))
    o_ref[0] = o.astype(BF16)


def _nsa_attention(qn, kc, vc, ks, vs, kw, vw, gates, near, cmpb):
    B, G, HG, T, hd = qn.shape
    tq = TQ
    assert T % tq == 0 and T >= 3 * tq and WINDOW == 2 * tq and tq >= 2 * REL_MAX_DIST and tq % LANES == 0
    nq = T // tq
    NC = kc.shape[2]
    NS = T // SEL_BLOCK
    sel_k = min(SEL_TOPK, NS)
    n = np.arange(NC)
    j = np.arange(NS)
    ovl = ((n[None, :] * CMP_STRIDE < (j[:, None] + 1) * SEL_BLOCK)
           & (n[None, :] * CMP_STRIDE + CMP_LEN - 1 >= j[:, None] * SEL_BLOCK) & (n[None, :] < NC - 1))
    ovl = jnp.asarray(ovl, BF16)
    W = HG * hd
    gexp = np.zeros((2 * LANES, 3 * W), np.float32)
    for hg in range(HG):
        for r in range(3):
            for part in range(2):
                gexp[part * LANES + 3 * hg + r, r * W + hg * hd:r * W + (hg + 1) * hd] = 1.0
    gexp = jnp.asarray(gexp, BF16)
    seq = lambda rows, width: pl.BlockSpec((1, 1, rows, width), lambda g, b, t: (b, g, 0, 0))
    R = HG * tq
    return pl.pallas_call(
        functools.partial(_nsa_kernel, tq=tq, sel_k=sel_k),
        grid=(G, B, nq),
        in_specs=[pl.BlockSpec((1, 1, HG, tq, hd), lambda g, b, t: (b, g, 0, t, 0)),
                  seq(NC, hd), seq(NC, LANES), seq(T, LANES), seq(T, 2 * LANES), seq(T, hd), seq(T, 2 * LANES),
                  pl.BlockSpec((1, 1, tq, LANES), lambda g, b, t: (b, g, t, 0)),
                  pl.BlockSpec((HG, len(NEAR_OFFSETS) + 1, tq, tq), lambda g, b, t: (g, 0, 0, 0),
                               pipeline_mode=pl.Buffered(1)),
                  pl.BlockSpec((HG, tq, NC), lambda g, b, t: (g, t, 0)),
                  pl.BlockSpec(ovl.shape, lambda g, b, t: (0, 0)),
                  pl.BlockSpec(gexp.shape, lambda g, b, t: (0, 0))],
        out_specs=pl.BlockSpec((1, tq, W), lambda g, b, t: (b, t, g)),
        out_shape=jax.ShapeDtypeStruct((B, T, G * W), BF16),
        scratch_shapes=[pltpu.VMEM((R, LANES), BF16),
                        pltpu.VMEM((R, T), F32), pltpu.VMEM((R, 3 * tq), F32),
                        pltpu.VMEM((R, LANES), F32), pltpu.VMEM((R, LANES), F32),
                        pltpu.VMEM((R, 2 * LANES), F32), pltpu.VMEM((R, 2 * LANES), F32)],
        compiler_params=pltpu.CompilerParams(vmem_limit_bytes=NSA_VMEM_LIMIT),
        name="nsa_attention",
    )(qn, kc, vc, ks, vs, kw, vw, gates, near, cmpb, ovl, gexp)


def _merge_kernel(x_ref, oret_ref, on_ref, ma_ref, mb_ref, wr_ref, wn_ref, wo_ref, o_ref):
    y_ret = _dot(oret_ref[...], wr_ref[...])
    y_nsa = _dot(on_ref[...], wn_ref[...])
    mixed = ma_ref[...].astype(F32) * y_ret + mb_ref[...].astype(F32) * y_nsa
    o_ref[...] = x_ref[...] + _dot(mixed.astype(BF16), wo_ref[...])


def _merge(x2, oret, on, ma, mb, w_o_ret, w_o_nsa, w_out):
    M, D = x2.shape
    tm = min(TM_MERGE, M)
    row = lambda width: pl.BlockSpec((tm, width), lambda i: (i, 0))
    res = lambda a: pl.BlockSpec(a.shape, lambda i: (0, 0), pipeline_mode=pl.Buffered(1))
    w = [a.astype(BF16) for a in (w_o_ret, w_o_nsa, w_out)]
    return pl.pallas_call(
        _merge_kernel,
        grid=(M // tm,),
        in_specs=[row(D), row(oret.shape[1]), row(D), row(D), row(D)] + [res(a) for a in w],
        out_specs=row(D),
        out_shape=jax.ShapeDtypeStruct((M, D), F32),
        compiler_params=pltpu.CompilerParams(vmem_limit_bytes=48 * MIB),
        name="merge_out_projection",
    )(x2, oret, on, ma, mb, *w)


def _rms(x, g):
    return x * lax.rsqrt(jnp.mean(x * x, -1, keepdims=True) + EPS) * g


def _ffn_kernel(x_ref, g_ref, wi_ref, wo_ref, gf_ref, o_ref, h_scr, acc_scr, *, final_norm):
    x = x_ref[...]
    h_scr[...] = _rms(x, g_ref[...]).astype(BF16)
    acc_scr[...] = x
    cw = FFN_HIDDEN // 2
    for c in range(2):
        a = _dot(h_scr[...], wi_ref[:, c * cw:(c + 1) * cw])
        b = _dot(h_scr[...], wi_ref[:, FFN_HIDDEN + c * cw:FFN_HIDDEN + (c + 1) * cw])
        acc_scr[...] += _dot((_silu(a) * b).astype(BF16), wo_ref[c * cw:(c + 1) * cw, :])
    y = acc_scr[...]
    o_ref[...] = _rms(y, gf_ref[...]) if final_norm else y


def _ffn(x2, g, w_in, w_out, g_final, final_norm):
    M, D = x2.shape
    tm = min(TM_MERGE, M)
    row = pl.BlockSpec((tm, D), lambda i: (i, 0))
    vec = pl.BlockSpec((1, D), lambda i: (0, 0))
    res = lambda a: pl.BlockSpec(a.shape, lambda i: (0, 0), pipeline_mode=pl.Buffered(1))
    wi, wo = w_in.astype(BF16), w_out.astype(BF16)
    return pl.pallas_call(
        functools.partial(_ffn_kernel, final_norm=final_norm),
        grid=(M // tm,),
        in_specs=[row, vec, res(wi), res(wo), vec],
        out_specs=row,
        out_shape=jax.ShapeDtypeStruct((M, D), F32),
        scratch_shapes=[pltpu.VMEM((tm, D), BF16), pltpu.VMEM((tm, D), F32)],
        compiler_params=pltpu.CompilerParams(vmem_limit_bytes=52 * MIB),
        name="swiglu_ffn",
    )(x2, g.reshape(1, D), wi, wo, g_final.reshape(1, D))


def kernel(x, norm_mix_g, w_in, cmp_pos_k, cmp_pos_v, cmp_w1_k, cmp_w2_k, cmp_w1_v, cmp_w2_v, w_o_ret, w_o_nsa,
           w_out, norm_ffn_g, w_ffn_in, w_ffn_out, rel_bias, norm_final_g):
    B, T, D = x.shape
    depth = w_in.shape[0]
    G, HG = NSA_KV_GROUPS, NSA_HG
    half = RET_DK // 2
    freqs = ROPE_BASE ** (-jnp.arange(half, dtype=F32) / half)
    ang = jnp.arange(T, dtype=jnp.int32).astype(F32)[:, None] * freqs
    cos, sin = jnp.cos(ang), jnp.sin(ang)
    near, cmpb = _bias_tables(rel_bias, T, TQ)
    for i in range(depth):
        qr, kr, vr, gr, qn, kvc, ks, vs, kw, vw, gates, ma, mb = _in_projection(
            x, norm_mix_g[i], _pack_w_in(w_in[i]), cos, sin)
        o_ret = _retention(qr, kr, vr, gr)
        kvc16 = kvc.reshape(2, B, G, T // CMP_STRIDE, CMP_STRIDE * NSA_HD)
        kc, vc = _compress(kvc16, cmp_pos_k[i], cmp_pos_v[i], cmp_w1_k[i], cmp_w2_k[i], cmp_w1_v[i], cmp_w2_v[i])
        o_n = _nsa_attention(qn.reshape(B, G, HG, T, NSA_HD), kc, vc, ks, vs, kw, vw, gates, near, cmpb)
        o_n = o_n.reshape(B * T, NSA_HEADS * NSA_HD)
        x2 = _merge(x.reshape(B * T, D), o_ret.reshape(B * T, -1), o_n, ma.reshape(B * T, D), mb.reshape(B * T, D),
                    w_o_ret[i], w_o_nsa[i], w_out[i])
        x2 = _ffn(x2, norm_ffn_g[i], w_ffn_in[i], w_ffn_out[i], norm_final_g, final_norm=(i == depth - 1))
        x = x2.reshape(B, T, D)
    return x
```

```python
import functools
import math

import jax
import jax.numpy as jnp
import numpy as np
from jax import lax
from jax.experimental import pallas as pl
from jax.experimental.pallas import tpu as pltpu

F32 = jnp.float32
BF16 = jnp.bfloat16

D_MODEL = 1024
RET_HEADS = 4
RET_DK = 256
RET_DV = 512
RET_BLOCK = 256
ROPE_BASE = 10000.0
GN_EPS = 1e-5
NSA_HEADS = 16
NSA_KV_GROUPS = 2
NSA_HG = NSA_HEADS // NSA_KV_GROUPS
NSA_HD = 64
CMP_LEN = 32
CMP_STRIDE = 16
CMP_HIDDEN = 256
SEL_BLOCK = 64
SEL_TOPK = 8
WINDOW = 512
FORCED_SCORE = 1e4
REL_BUCKETS = 32
REL_MAX_DIST = 128
FFN_HIDDEN = -(-8 * D_MODEL // (3 * 256)) * 256
EPS = 1e-6
NEG_INF = -1e30
PICKED = -3e38
MASKED_CODE = REL_BUCKETS
NEAR_OFFSETS = (0, 1, 2)
LOG2E = math.log2(math.e)

LANES = 128
SUBLANES = 8
MIB = 1024 * 1024
NSA_VMEM_LIMIT = 56 * MIB

_SEG = np.cumsum([0, RET_HEADS * RET_DK, RET_HEADS * RET_DK, RET_HEADS * RET_DV, RET_HEADS * RET_DV,
                  NSA_HEADS * NSA_HD, 6 * NSA_KV_GROUPS * NSA_HD, 3 * NSA_HEADS, D_MODEL, D_MODEL])
N_QR, N_KR, N_VR, N_GR = RET_HEADS * RET_DK, RET_HEADS * RET_DK, RET_HEADS * RET_DV, RET_HEADS * RET_DV
N_QN, N_KV = NSA_HEADS * NSA_HD, 6 * NSA_KV_GROUPS * NSA_HD
N_GATE = NSA_KV_GROUPS * LANES
_P = np.cumsum([0, N_QR, N_KR, N_VR, N_GR, N_QN, N_KV, N_GATE, D_MODEL, D_MODEL])
P_QR, P_KR, P_VR, P_GR, P_QN, P_KV, P_GATE, P_MA, P_MB, P_END = [int(v) for v in _P]

TM_PROJ = 512
TM_MERGE = 512
TQ = 256
NT = (((1,), (1,)), ((), ()))


def _dot(a, b):
    return jnp.dot(a, b, preferred_element_type=F32)


def _dot_nt(a, b):
    return lax.dot_general(a, b, NT, preferred_element_type=F32)


def _sigmoid(x):
    return 1.0 / (1.0 + jnp.exp(-x))


def _silu(x):
    return x * _sigmoid(x)


def _inproj_kernel(x_ref, g_ref, cos_ref, sin_ref, feat_ref, w_ref,
                   qr_o, kr_o, vr_o, gr_o, qn_o, kvc_o, ks_o, vst_o, kw_o, vwt_o, gate_o, ma_o, mb_o, h_scr):
    x = x_ref[0]
    h = x * lax.rsqrt(jnp.mean(x * x, -1, keepdims=True) + EPS) * g_ref[...]
    h_scr[...] = h.astype(BF16)
    cos = cos_ref[...]
    sin = sin_ref[...]
    half = RET_DK // 2

    def proj(lo, width):
        return _dot(h_scr[...], w_ref[:, lo:lo + width])

    for hd in range(RET_HEADS):
        for out, base, scale in ((qr_o, P_QR, 1.0), (kr_o, P_KR, RET_DK ** -0.5)):
            y = proj(base + hd * RET_DK, RET_DK)
            x1, x2 = y[:, :half], y[:, half:]
            out[0, :, hd * RET_DK:hd * RET_DK + half] = ((x1 * cos - x2 * sin) * scale).astype(BF16)
            out[0, :, hd * RET_DK + half:(hd + 1) * RET_DK] = ((x1 * sin + x2 * cos) * scale).astype(BF16)
    cw = 512
    for c in range(N_VR // cw):
        vr_o[0, :, c * cw:(c + 1) * cw] = proj(P_VR + c * cw, cw).astype(BF16)
        gr_o[0, :, c * cw:(c + 1) * cw] = _silu(proj(P_GR + c * cw, cw)).astype(BF16)
    for c in range(D_MODEL // cw):
        ma_o[0, :, c * cw:(c + 1) * cw] = _sigmoid(proj(P_MA + c * cw, cw)).astype(BF16)
        mb_o[0, :, c * cw:(c + 1) * cw] = _sigmoid(proj(P_MB + c * cw, cw)).astype(BF16)
    per = cw // NSA_HD
    for c in range(N_QN // cw):
        y = (proj(P_QN + c * cw, cw) * (NSA_HD ** -0.5 * LOG2E)).astype(BF16)
        for j in range(per):
            qn_o[0, c * per + j] = y[:, j * NSA_HD:(j + 1) * NSA_HD]
    y32 = proj(P_KV, N_KV)
    y = y32.astype(BF16)
    y_t = y32.T.astype(BF16)
    tm = y.shape[0]
    ones_row = (lax.broadcasted_iota(jnp.int32, (LANES - NSA_HD, tm), 0) == 0).astype(BF16)
    for g in range(NSA_KV_GROUPS):
        lo = lambda j: (j * NSA_KV_GROUPS + g) * NSA_HD
        kvc_o[0, 0, g] = y[:, lo(0):lo(0) + NSA_HD]
        kvc_o[1, 0, g] = y[:, lo(1):lo(1) + NSA_HD]
        ks_o[0, g, :, :NSA_HD] = y[:, lo(2):lo(2) + NSA_HD]
        ks_o[0, g, :, NSA_HD:] = feat_ref[:, NSA_HD:]
        kw_o[0, g] = y[:, lo(4):lo(4) + NSA_HD]
        for out, j in ((vst_o, 3), (vwt_o, 5)):
            out[0, g, :NSA_HD, :] = y_t[lo(j):lo(j) + NSA_HD, :]
            out[0, g, NSA_HD:, :] = ones_row
    gates_t = _sigmoid(proj(P_GATE, N_GATE)).T
    for g in range(NSA_KV_GROUPS):
        gate_o[0, g] = gates_t[g * LANES:(g + 1) * LANES, :]


def _pack_w_in(w):
    s = [int(v) for v in _SEG]
    gate = w[:, s[6]:s[7]].reshape(D_MODEL, NSA_KV_GROUPS, 3 * NSA_HG)
    gate = jnp.pad(gate, ((0, 0), (0, 0), (0, LANES - 3 * NSA_HG))).reshape(D_MODEL, N_GATE)
    return jnp.concatenate([w[:, s[0]:s[6]], gate, w[:, s[7]:s[9]]], axis=1).astype(BF16)


def _in_projection(x, g, w_packed, cos, sin):
    B, T, D = x.shape
    G = NSA_KV_GROUPS
    tm = min(TM_PROJ, T)
    nt = T // tm
    assert T // SEL_BLOCK <= LANES - NSA_HD
    own_block = (np.arange(T)[:, None] // SEL_BLOCK) == (np.arange(LANES)[None, :] - NSA_HD)
    feat = jnp.asarray(np.where(own_block, NEG_INF, 0.0), BF16)
    row = lambda width: pl.BlockSpec((1, tm, width), lambda b, t: (b, t, 0))
    grp = lambda width: pl.BlockSpec((1, G, tm, width), lambda b, t: (b, 0, t, 0))
    grp_t = pl.BlockSpec((1, G, LANES, tm), lambda b, t: (b, 0, 0, t))
    grp_shape = lambda width: jax.ShapeDtypeStruct((B, G, T, width), BF16)
    grp_t_shape = lambda dt: jax.ShapeDtypeStruct((B, G, LANES, T), dt)
    out_shape = (
        jax.ShapeDtypeStruct((B, T, N_QR), BF16), jax.ShapeDtypeStruct((B, T, N_KR), BF16),
        jax.ShapeDtypeStruct((B, T, N_VR), BF16), jax.ShapeDtypeStruct((B, T, N_GR), BF16),
        jax.ShapeDtypeStruct((B, NSA_HEADS, T, NSA_HD), BF16),
        jax.ShapeDtypeStruct((2, B, G, T, NSA_HD), BF16),
        grp_shape(LANES), grp_t_shape(BF16), grp_shape(NSA_HD), grp_t_shape(BF16),
        grp_t_shape(F32),
        jax.ShapeDtypeStruct((B, T, D_MODEL), BF16), jax.ShapeDtypeStruct((B, T, D_MODEL), BF16))
    out_specs = (
        row(N_QR), row(N_KR), row(N_VR), row(N_GR),
        pl.BlockSpec((1, NSA_HEADS, tm, NSA_HD), lambda b, t: (b, 0, t, 0)),
        pl.BlockSpec((2, 1, G, tm, NSA_HD), lambda b, t: (0, b, 0, t, 0)),
        grp(LANES), grp_t, grp(NSA_HD), grp_t,
        grp_t,
        row(D_MODEL), row(D_MODEL))
    out_bytes = tm * (2 * (N_QR + N_KR + N_VR + N_GR + N_QN + 2 * D_MODEL) + 2 * G * (3 * NSA_HD + 3 * LANES)
                      + 4 * N_GATE)
    vmem = 2 * out_bytes + 2 * tm * D * 4 + 2 * D * P_END + tm * D * 2 + 10 * MIB
    return pl.pallas_call(
        _inproj_kernel,
        grid=(B, nt),
        in_specs=[pl.BlockSpec((1, tm, D), lambda b, t: (b, t, 0)),
                  pl.BlockSpec((1, D), lambda b, t: (0, 0)),
                  pl.BlockSpec((tm, RET_DK // 2), lambda b, t: (t, 0)),
                  pl.BlockSpec((tm, RET_DK // 2), lambda b, t: (t, 0)),
                  pl.BlockSpec((tm, LANES), lambda b, t: (t, 0)),
                  pl.BlockSpec((D, P_END), lambda b, t: (0, 0), pipeline_mode=pl.Buffered(1))],
        out_specs=out_specs,
        out_shape=out_shape,
        scratch_shapes=[pltpu.VMEM((tm, D), BF16)],
        compiler_params=pltpu.CompilerParams(vmem_limit_bytes=int(vmem)),
        name="in_projection",
    )(x, g.reshape(1, D), cos, sin, feat, w_packed)


def _retention_kernel(q_ref, k_ref, v_ref, g_ref, dmask_ref, xi_ref, zeta_ref, gch_ref, o_ref):
    C = RET_BLOCK
    nc = q_ref.shape[1] // C
    dmask = dmask_ref[0]
    xi = xi_ref[0]
    zeta = zeta_ref[0]
    gch = gch_ref[0, :, 0:1]
    r = jnp.zeros((RET_DK, RET_DV), F32)
    for c in range(nc):
        rows = slice(c * C, (c + 1) * C)
        qc = q_ref[0, rows, :]
        kc = k_ref[0, rows, :]
        vc = v_ref[0, rows, :]
        s = _dot_nt(qc, kc) * dmask
        o = _dot(s.astype(BF16), vc)
        if c > 0:
            o = o + _dot(qc, r.astype(BF16)) * xi
        if c < nc - 1:
            kz = (kc.astype(F32) * zeta).astype(BF16)
            kv = lax.dot_general(kz, vc, (((0,), (0,)), ((), ())), preferred_element_type=F32)
            r = kv if c == 0 else gch * r + kv
        mu = jnp.mean(o, -1, keepdims=True)
        d = o - mu
        var = jnp.mean(d * d, -1, keepdims=True)
        on = d * lax.rsqrt(var + GN_EPS)
        o_ref[0, rows, :] = (g_ref[0, rows, :].astype(F32) * on).astype(BF16)


def _retention(qr, kr, vr, gr):
    B, T, _ = qr.shape
    C = RET_BLOCK
    lg = jnp.log(1.0 - 2.0 ** (-5.0 - jnp.arange(RET_HEADS, dtype=F32)))
    n = jnp.arange(C, dtype=F32)
    diff = n[:, None] - n[None, :]
    dmask = jnp.where(diff >= 0, jnp.exp(jnp.maximum(diff, 0.0)[None] * lg[:, None, None]), 0.0)
    xi = jnp.exp((n + 1.0)[None] * lg[:, None])[:, :, None]
    zeta = jnp.exp((C - 1.0 - n)[None] * lg[:, None])[:, :, None]
    gch = jnp.broadcast_to(jnp.exp(C * lg)[:, None, None], (RET_HEADS, 1, LANES))
    qk_spec = pl.BlockSpec((1, T, RET_DK), lambda b, h: (b, 0, h))
    v_spec = pl.BlockSpec((1, T, RET_DV), lambda b, h: (b, 0, h))
    return pl.pallas_call(
        _retention_kernel,
        grid=(B, RET_HEADS),
        in_specs=[qk_spec, qk_spec, v_spec, v_spec,
                  pl.BlockSpec((1, C, C), lambda b, h: (h, 0, 0)),
                  pl.BlockSpec((1, C, 1), lambda b, h: (h, 0, 0)),
                  pl.BlockSpec((1, C, 1), lambda b, h: (h, 0, 0)),
                  pl.BlockSpec((1, 1, LANES), lambda b, h: (h, 0, 0))],
        out_specs=v_spec,
        out_shape=jax.ShapeDtypeStruct((B, T, RET_HEADS * RET_DV), BF16),
        compiler_params=pltpu.CompilerParams(vmem_limit_bytes=32 * MIB),
        name="retention",
    )(qr, kr, vr, gr, dmask, xi, zeta, gch)


def _gelu_tanh(x):
    return 0.5 * x * (1.0 + jnp.tanh(math.sqrt(2.0 / math.pi) * (x + 0.044715 * (x * x * x))))


def _compress_kernel(x_ref, pk_ref, pv_ref, w1k_ref, w2k_ref, w1v_ref, w2vt_ref, ko_ref, vto_ref):
    half = CMP_STRIDE * NSA_HD

    def hidden(s, g, p_ref, w1_ref):
        x = x_ref[s, 0, g].astype(F32)
        a = _dot((x + p_ref[0:1, :]).astype(BF16), w1_ref[0:half, :])
        b = _dot((x + p_ref[1:2, :]).astype(BF16), w1_ref[half:2 * half, :])
        pre = a + pltpu.roll(b, b.shape[0] - 1, 0)
        return _gelu_tanh(pre).astype(BF16)

    for g in range(NSA_KV_GROUPS):
        ko_ref[0, g] = _dot(hidden(0, g, pk_ref, w1k_ref), w2k_ref[...]).astype(BF16)
        vto_ref[0, g] = _dot_nt(w2vt_ref[...], hidden(1, g, pv_ref, w1v_ref)).astype(BF16)


def _compress(kvc16, pos_k, pos_v, w1k, w2k, w1v, w2v):
    _, B, G, NC, F = kvc16.shape
    full = lambda a: pl.BlockSpec(a.shape, lambda b: (0,) * a.ndim)
    pk = pos_k.reshape(2, F)
    pv = pos_v.reshape(2, F)
    w = [a.astype(BF16) for a in (w1k, w2k, w1v, w2v.T)]
    return pl.pallas_call(
        _compress_kernel,
        grid=(B,),
        in_specs=[pl.BlockSpec((2, 1, G, NC, F), lambda b: (0, b, 0, 0, 0)), full(pk), full(pv)]
        + [full(a) for a in w],
        out_specs=(pl.BlockSpec((1, G, NC, NSA_HD), lambda b: (b, 0, 0, 0)),
                   pl.BlockSpec((1, G, NSA_HD, NC), lambda b: (b, 0, 0, 0))),
        out_shape=(jax.ShapeDtypeStruct((B, G, NC, NSA_HD), BF16), jax.ShapeDtypeStruct((B, G, NSA_HD, NC), BF16)),
        compiler_params=pltpu.CompilerParams(vmem_limit_bytes=32 * MIB),
        name="nsa_compress",
    )(kvc16, pk, pv, *w)


def _bias_kernel(tab_ref, nb_ref, cb_ref, near_o, cmp_o):
    h = pl.program_id(0)
    far = tab_ref[REL_BUCKETS - 1, h]

    def build(bk):
        acc = jnp.zeros(bk.shape, F32)
        for k in range(REL_BUCKETS - 1):
            acc = jnp.where(bk == k, (tab_ref[k, h] - far) * LOG2E, acc)
        return jnp.where(bk == MASKED_CODE, NEG_INF, acc)

    near_o[0] = build(nb_ref[...])
    cmp_o[0] = build(cb_ref[...])


def _t5_bucket(dist):
    dist = jnp.maximum(dist, 0)
    max_exact = REL_BUCKETS // 2
    large = max_exact + (jnp.log(jnp.maximum(dist, 1).astype(F32) / max_exact)
                         / math.log(REL_MAX_DIST / max_exact) * (REL_BUCKETS - max_exact)).astype(jnp.int32)
    large = jnp.minimum(large, REL_BUCKETS - 1)
    return jnp.where(dist < max_exact, dist, large)


def _bias_tables(rel_bias, T, tq):
    nc = T // CMP_STRIDE
    i = jnp.arange(tq, dtype=jnp.int32)
    d = jnp.asarray(NEAR_OFFSETS, jnp.int32)[:, None, None] * tq + i[None, None, :] - i[None, :, None]
    near_b = jnp.where((d < 0) | (d >= WINDOW), MASKED_CODE, _t5_bucket(d))
    near_b = jnp.concatenate([near_b, jnp.full((1, tq, tq), MASKED_CODE, jnp.int32)], axis=0)
    cmp_end = jnp.arange(nc, dtype=jnp.int32) * CMP_STRIDE + CMP_LEN - 1
    cmp_b = _t5_bucket(jnp.arange(T, dtype=jnp.int32)[None, :] - cmp_end[:, None])
    return pl.pallas_call(
        _bias_kernel,
        grid=(NSA_HEADS,),
        in_specs=[pl.BlockSpec(memory_space=pltpu.SMEM),
                  pl.BlockSpec(near_b.shape, lambda h: (0, 0, 0)),
                  pl.BlockSpec(cmp_b.shape, lambda h: (0, 0))],
        out_specs=(pl.BlockSpec((1,) + near_b.shape, lambda h: (h, 0, 0, 0)),
                   pl.BlockSpec((1,) + cmp_b.shape, lambda h: (h, 0, 0))),
        out_shape=(jax.ShapeDtypeStruct((NSA_HEADS,) + near_b.shape, F32),
                   jax.ShapeDtypeStruct((NSA_HEADS,) + cmp_b.shape, F32)),
        name="rel_bias_tables",
    )(rel_bias, near_b, cmp_b)


def _nsa_kernel(q_ref, kc_ref, vct_ref, ks_ref, vst_ref, kw_ref, vwt_ref, gt_ref,
                near_ref, cb_ref, ovl_ref, o_ref,
                qa_scr, s_scr, sw_scr, mxs_scr, mxw_scr, accs_scr, accw_scr, *, tq, sel_k):
    HG = NSA_HG
    R = HG * tq
    NC = kc_ref.shape[2]
    NS = ovl_ref.shape[0]
    UW = tq
    ti = pl.program_id(2)
    s0 = ti * tq
    q = q_ref[0, 0].reshape(R, NSA_HD)

    def head(x, h):
        return x[:, h * tq:(h + 1) * tq]

    def keys(ref, u, n=1):
        return ref[0, 0, pl.ds(pl.multiple_of(u * UW, UW), n * UW), :]

    def values_t(ref, u, n=1):
        return ref[0, 0, :, pl.ds(pl.multiple_of(u * UW, UW), n * UW)]

    def rows(slot, n=1):
        return pl.ds(pl.multiple_of(slot * UW, UW), n * UW)

    def logits(s, s_ref, slot, mx_ref):
        s_ref[rows(slot, s.shape[0] // UW), :] = s
        groups = [s[r * SUBLANES:(r + 1) * SUBLANES] for r in range(s.shape[0] // SUBLANES)]
        mx_ref[...] = functools.reduce(jnp.maximum, groups, mx_ref[...])

    def biased(s, e):
        return jnp.concatenate([head(s, h) + near_ref[h, e] for h in range(HG)], axis=1)

    def col_max(mx_ref):
        mx_ref[...] = jnp.broadcast_to(jnp.max(mx_ref[...], axis=0, keepdims=True), mx_ref.shape)

    def weigh(s_ref, slot, m_ref, vt_ref, u, acc_ref, n=1):
        vt = values_t(vt_ref, u, n)
        for h in range(HG):
            c = slice(h * tq, (h + 1) * tq)
            p = jnp.exp2((s_ref[rows(slot, n), c] - m_ref[0:1, c]).astype(BF16))
            acc_ref[:, c] += _dot(vt, p)

    def normalized(acc_ref):
        acc = acc_ref[...]
        return acc[:NSA_HD] / acc[NSA_HD:NSA_HD + 1]

    for ref in (mxs_scr, mxw_scr):
        ref[...] = jnp.full(ref.shape, NEG_INF, F32)
    for ref in (accs_scr, accw_scr):
        ref[...] = jnp.zeros_like(ref)

    masked = len(NEAR_OFFSETS)
    w0 = jnp.maximum(ti - 2, 0)

    def window_logits(slot):
        back = jnp.minimum(ti, 2) - slot
        logits(biased(_dot_nt(keys(kw_ref, w0 + slot), q), jnp.where(back >= 0, back, masked)),
               sw_scr, slot, mxw_scr)

    lc = _dot_nt(kc_ref[0, 0], q)
    lc = jnp.concatenate([head(lc, h) + cb_ref[h] for h in range(HG)], axis=1)
    pos_c = s0 + lax.broadcasted_iota(jnp.int32, (NC, tq), 1)
    cend = lax.broadcasted_iota(jnp.int32, (NC, tq), 0) * CMP_STRIDE + (CMP_LEN - 1)
    valid_c = jnp.concatenate([cend <= pos_c] * HG, axis=1)
    z = jnp.where(valid_c, lc, NEG_INF)
    z = z - jnp.max(z, 0, keepdims=True)
    e = jnp.exp2(z) * valid_c.astype(F32)
    pc = e / jnp.maximum(jnp.sum(e, 0, keepdims=True), 1e-30)
    oc = _dot(vct_ref[0, 0], pc.astype(BF16))
    psum = functools.reduce(jnp.add, [head(pc, h) for h in range(HG)])
    window_logits(0)

    p_hi = psum.astype(BF16)
    r1 = psum - p_hi.astype(F32)
    p_mid = r1.astype(BF16)
    p_lo = (r1 - p_mid.astype(F32)).astype(BF16)
    ovl = ovl_ref[...]
    imp = _dot(ovl, p_hi) + _dot(ovl, p_mid) + _dot(ovl, p_lo)
    jrow = lax.broadcasted_iota(jnp.int32, (NS, tq), 0)
    posl = s0 + lax.broadcasted_iota(jnp.int32, (NS, tq), 1)
    cur = lax.shift_right_logical(posl, int(math.log2(SEL_BLOCK)))
    forced = (jrow == 0) | (jrow == cur) | (jrow == cur - 1)
    started = jrow * SEL_BLOCK <= posl
    imp = jnp.where(forced, FORCED_SCORE, imp)
    imp = jnp.where(started, imp, NEG_INF)
    window_logits(1)
    sel = jnp.zeros((NS, tq), F32)
    for _ in range(sel_k):
        mx = jnp.max(imp, axis=0, keepdims=True)
        first = jnp.min(jnp.where(imp == mx, jrow, NS), axis=0, keepdims=True)
        hit = jrow == first
        sel = jnp.where(hit, 1.0, sel)
        imp = jnp.where(hit, PICKED, imp)
    window_logits(2)
    notsel = 1.0 - jnp.where(started, sel, 0.0)
    notsel = jnp.concatenate([notsel, jnp.zeros((LANES - NS, tq), F32)], axis=0)
    notsel = pltpu.roll(notsel.T, NSA_HD, 1).astype(BF16)
    for hg in range(HG):
        qa_scr[hg * tq:(hg + 1) * tq, :NSA_HD] = q_ref[0, 0, hg]
        qa_scr[hg * tq:(hg + 1) * tq, NSA_HD:] = notsel[:, NSA_HD:]
    qa = qa_scr[...]

    prev_u = jnp.where(ti >= 1, ti - 1, 1)
    logits(biased(_dot_nt(keys(ks_ref, ti), qa), 0), s_scr, ti, mxs_scr)
    logits(biased(_dot_nt(keys(ks_ref, prev_u), qa), jnp.where(ti >= 1, 1, masked)), s_scr, prev_u, mxs_scr)

    n_plain = jnp.maximum(ti - 1, 0)

    def plain_pair(i, carry):
        logits(_dot_nt(keys(ks_ref, 2 * i, 2), qa), s_scr, 2 * i, mxs_scr)
        return carry

    lax.fori_loop(0, lax.shift_right_logical(n_plain, 1), plain_pair, 0)

    @pl.when(lax.rem(n_plain, 2) == 1)
    def _():
        logits(_dot_nt(keys(ks_ref, n_plain - 1), qa), s_scr, n_plain - 1, mxs_scr)

    col_max(mxs_scr)
    col_max(mxw_scr)
    weigh(sw_scr, 0, mxw_scr, vwt_ref, w0, accw_scr, 3)
    weigh(s_scr, ti, mxs_scr, vst_ref, ti, accs_scr)
    weigh(s_scr, prev_u, mxs_scr, vst_ref, prev_u, accs_scr)

    def weigh_pair(i, carry):
        weigh(s_scr, 2 * i, mxs_scr, vst_ref, 2 * i, accs_scr, 2)
        return carry

    lax.fori_loop(0, lax.shift_right_logical(n_plain, 1), weigh_pair, 0)

    @pl.when(lax.rem(n_plain, 2) == 1)
    def _():
        weigh(s_scr, n_plain - 1, mxs_scr, vst_ref, n_plain - 1, accs_scr)

    gt = gt_ref[0, 0]
    o_s = normalized(accs_scr)
    o_w = normalized(accw_scr)

    def gated(h):
        return (gt[3 * h:3 * h + 1] * head(oc, h) + gt[3 * h + 1:3 * h + 2] * head(o_s, h)
                + gt[3 * h + 2:3 * h + 3] * head(o_w, h))

    pairs = [jnp.concatenate([gated(2 * j), gated(2 * j + 1)], axis=0).T for j in range(HG // 2)]
    o_ref[0] = jnp.concatenate(pairs, axis=1).astype(BF16)


def _nsa_attention(qn, kc, vct, ks, vst, kw, vwt, gates_t, near, cmpb):
    B, G, HG, T, hd = qn.shape
    tq = TQ
    assert T % tq == 0 and T >= 3 * tq and WINDOW == 2 * tq and tq >= 2 * REL_MAX_DIST and tq % LANES == 0
    nq = T // tq
    NC = kc.shape[2]
    NS = T // SEL_BLOCK
    sel_k = min(SEL_TOPK, NS)
    n = np.arange(NC)
    j = np.arange(NS)
    ovl = ((n[None, :] * CMP_STRIDE < (j[:, None] + 1) * SEL_BLOCK)
           & (n[None, :] * CMP_STRIDE + CMP_LEN - 1 >= j[:, None] * SEL_BLOCK) & (n[None, :] < NC - 1))
    ovl = jnp.asarray(ovl, BF16)
    seq = lambda rows, width: pl.BlockSpec((1, 1, rows, width), lambda g, b, t: (b, g, 0, 0))
    R = HG * tq
    W = HG * hd
    return pl.pallas_call(
        functools.partial(_nsa_kernel, tq=tq, sel_k=sel_k),
        grid=(G, B, nq),
        in_specs=[pl.BlockSpec((1, 1, HG, tq, hd), lambda g, b, t: (b, g, 0, t, 0)),
                  seq(NC, hd), seq(hd, NC), seq(T, LANES), seq(LANES, T), seq(T, hd), seq(LANES, T),
                  pl.BlockSpec((1, 1, LANES, tq), lambda g, b, t: (b, g, 0, t)),
                  pl.BlockSpec((HG, len(NEAR_OFFSETS) + 1, tq, tq), lambda g, b, t: (g, 0, 0, 0),
                               pipeline_mode=pl.Buffered(1)),
                  pl.BlockSpec((HG, NC, tq), lambda g, b, t: (g, 0, t)),
                  pl.BlockSpec(ovl.shape, lambda g, b, t: (0, 0))],
        out_specs=pl.BlockSpec((1, tq, W), lambda g, b, t: (b, t, g)),
        out_shape=jax.ShapeDtypeStruct((B, T, G * W), BF16),
        scratch_shapes=[pltpu.VMEM((R, LANES), BF16),
                        pltpu.VMEM((T, R), F32), pltpu.VMEM((3 * tq, R), F32),
                        pltpu.VMEM((SUBLANES, R), F32), pltpu.VMEM((SUBLANES, R), F32),
                        pltpu.VMEM((LANES, R), F32), pltpu.VMEM((LANES, R), F32)],
        compiler_params=pltpu.CompilerParams(vmem_limit_bytes=NSA_VMEM_LIMIT),
        name="nsa_attention",
    )(qn, kc, vct, ks, vst, kw, vwt, gates_t, near, cmpb, ovl)


def _merge_kernel(x_ref, oret_ref, on_ref, ma_ref, mb_ref, wr_ref, wn_ref, wo_ref, o_ref):
    y_ret = _dot(oret_ref[...], wr_ref[...])
    y_nsa = _dot(on_ref[...], wn_ref[...])
    mixed = ma_ref[...].astype(F32) * y_ret + mb_ref[...].astype(F32) * y_nsa
    o_ref[...] = x_ref[...] + _dot(mixed.astype(BF16), wo_ref[...])


def _merge(x2, oret, on, ma, mb, w_o_ret, w_o_nsa, w_out):
    M, D = x2.shape
    tm = min(TM_MERGE, M)
    row = lambda width: pl.BlockSpec((tm, width), lambda i: (i, 0))
    res = lambda a: pl.BlockSpec(a.shape, lambda i: (0, 0), pipeline_mode=pl.Buffered(1))
    w = [a.astype(BF16) for a in (w_o_ret, w_o_nsa, w_out)]
    return pl.pallas_call(
        _merge_kernel,
        grid=(M // tm,),
        in_specs=[row(D), row(oret.shape[1]), row(D), row(D), row(D)] + [res(a) for a in w],
        out_specs=row(D),
        out_shape=jax.ShapeDtypeStruct((M, D), F32),
        compiler_params=pltpu.CompilerParams(vmem_limit_bytes=48 * MIB),
        name="merge_out_projection",
    )(x2, oret, on, ma, mb, *w)


def _rms(x, g):
    return x * lax.rsqrt(jnp.mean(x * x, -1, keepdims=True) + EPS) * g


def _ffn_kernel(x_ref, g_ref, wi_ref, wo_ref, gf_ref, o_ref, h_scr, acc_scr, *, final_norm):
    x = x_ref[...]
    h_scr[...] = _rms(x, g_ref[...]).astype(BF16)
    acc_scr[...] = x
    cw = FFN_HIDDEN // 2
    for c in range(2):
        a = _dot(h_scr[...], wi_ref[:, c * cw:(c + 1) * cw])
        b = _dot(h_scr[...], wi_ref[:, FFN_HIDDEN + c * cw:FFN_HIDDEN + (c + 1) * cw])
        acc_scr[...] += _dot((_silu(a) * b).astype(BF16), wo_ref[c * cw:(c + 1) * cw, :])
    y = acc_scr[...]
    o_ref[...] = _rms(y, gf_ref[...]) if final_norm else y


def _ffn(x2, g, w_in, w_out, g_final, final_norm):
    M, D = x2.shape
    tm = min(TM_MERGE, M)
    row = pl.BlockSpec((tm, D), lambda i: (i, 0))
    vec = pl.BlockSpec((1, D), lambda i: (0, 0))
    res = lambda a: pl.BlockSpec(a.shape, lambda i: (0, 0), pipeline_mode=pl.Buffered(1))
    wi, wo = w_in.astype(BF16), w_out.astype(BF16)
    return pl.pallas_call(
        functools.partial(_ffn_kernel, final_norm=final_norm),
        grid=(M // tm,),
        in_specs=[row, vec, res(wi), res(wo), vec],
        out_specs=row,
        out_shape=jax.ShapeDtypeStruct((M, D), F32),
        scratch_shapes=[pltpu.VMEM((tm, D), BF16), pltpu.VMEM((tm, D), F32)],
        compiler_params=pltpu.CompilerParams(vmem_limit_bytes=52 * MIB),
        name="swiglu_ffn",
    )(x2, g.reshape(1, D), wi, wo, g_final.reshape(1, D))


def kernel(x, norm_mix_g, w_in, cmp_pos_k, cmp_pos_v, cmp_w1_k, cmp_w2_k, cmp_w1_v, cmp_w2_v, w_o_ret, w_o_nsa,
           w_out, norm_ffn_g, w_ffn_in, w_ffn_out, rel_bias, norm_final_g):
    B, T, D = x.shape
    depth = w_in.shape[0]
    G, HG = NSA_KV_GROUPS, NSA_HG
    half = RET_DK // 2
    freqs = ROPE_BASE ** (-jnp.arange(half, dtype=F32) / half)
    ang = jnp.arange(T, dtype=jnp.int32).astype(F32)[:, None] * freqs
    cos, sin = jnp.cos(ang), jnp.sin(ang)
    near, cmpb = _bias_tables(rel_bias, T, TQ)
    for i in range(depth):
        qr, kr, vr, gr, qn, kvc, ks, vst, kw, vwt, gates_t, ma, mb = _in_projection(
            x, norm_mix_g[i], _pack_w_in(w_in[i]), cos, sin)
        o_ret = _retention(qr, kr, vr, gr)
        kvc16 = kvc.reshape(2, B, G, T // CMP_STRIDE, CMP_STRIDE * NSA_HD)
        kc, vct = _compress(kvc16, cmp_pos_k[i], cmp_pos_v[i], cmp_w1_k[i], cmp_w2_k[i], cmp_w1_v[i], cmp_w2_v[i])
        o_n = _nsa_attention(qn.reshape(B, G, HG, T, NSA_HD), kc, vct, ks, vst, kw, vwt, gates_t, near, cmpb)
        o_n = o_n.reshape(B * T, NSA_HEADS * NSA_HD)
        x2 = _merge(x.reshape(B * T, D), o_ret.reshape(B * T, -1), o_n, ma.reshape(B * T, D), mb.reshape(B * T, D),
                    w_o_ret[i], w_o_nsa[i], w_out[i])
        x2 = _ffn(x2, norm_ffn_g[i], w_ffn_in[i], w_ffn_out[i], norm_final_g, final_norm=(i == depth - 1))
        x = x2.reshape(B, T, D)
    return x
```

```python
import functools
import math

import jax
import jax.numpy as jnp
import numpy as np
from jax import lax
from jax.experimental import pallas as pl
from jax.experimental.pallas import tpu as pltpu

F32 = jnp.float32
BF16 = jnp.bfloat16

D_MODEL = 1024
RET_HEADS = 4
RET_DK = 256
RET_DV = 512
RET_BLOCK = 256
ROPE_BASE = 10000.0
GN_EPS = 1e-5
NSA_HEADS = 16
NSA_KV_GROUPS = 2
NSA_HG = NSA_HEADS // NSA_KV_GROUPS
NSA_HD = 64
CMP_LEN = 32
CMP_STRIDE = 16
CMP_HIDDEN = 256
SEL_BLOCK = 64
SEL_TOPK = 8
WINDOW = 512
FORCED_SCORE = 1e4
REL_BUCKETS = 32
REL_MAX_DIST = 128
FFN_HIDDEN = -(-8 * D_MODEL // (3 * 256)) * 256
EPS = 1e-6
NEG_INF = -1e30
PICKED = -3e38
MASKED_CODE = REL_BUCKETS
NEAR_OFFSETS = (0, 1, 2)
LOG2E = math.log2(math.e)

LANES = 128
SUBLANES = 8
MIB = 1024 * 1024
NSA_VMEM_LIMIT = 56 * MIB

_SEG = np.cumsum([0, RET_HEADS * RET_DK, RET_HEADS * RET_DK, RET_HEADS * RET_DV, RET_HEADS * RET_DV,
                  NSA_HEADS * NSA_HD, 6 * NSA_KV_GROUPS * NSA_HD, 3 * NSA_HEADS, D_MODEL, D_MODEL])
N_QR, N_KR, N_VR, N_GR = RET_HEADS * RET_DK, RET_HEADS * RET_DK, RET_HEADS * RET_DV, RET_HEADS * RET_DV
N_QN, N_KV = NSA_HEADS * NSA_HD, 6 * NSA_KV_GROUPS * NSA_HD
N_GATE = NSA_KV_GROUPS * LANES
_P = np.cumsum([0, N_QR, N_KR, N_VR, N_GR, N_QN, N_KV, N_GATE, D_MODEL, D_MODEL])
P_QR, P_KR, P_VR, P_GR, P_QN, P_KV, P_GATE, P_MA, P_MB, P_END = [int(v) for v in _P]

TM_PROJ = 512
TM_MERGE = 512
TQ = 256
NT = (((1,), (1,)), ((), ()))


def _dot(a, b):
    return jnp.dot(a, b, preferred_element_type=F32)


def _dot_nt(a, b):
    return lax.dot_general(a, b, NT, preferred_element_type=F32)


def _sigmoid(x):
    return 1.0 / (1.0 + jnp.exp(-x))


def _silu(x):
    return x * _sigmoid(x)


def _inproj_kernel(x_ref, g_ref, cos_ref, sin_ref, feat_ref, w_ref,
                   qr_o, kr_o, vr_o, gr_o, qn_o, kvc_o, ks_o, vst_o, kw_o, vwt_o, gate_o, ma_o, mb_o, h_scr, kvc_scr):
    x = x_ref[0]
    h = x * lax.rsqrt(jnp.mean(x * x, -1, keepdims=True) + EPS) * g_ref[...]
    h_scr[...] = h.astype(BF16)
    cos = cos_ref[...]
    sin = sin_ref[...]
    half = RET_DK // 2

    def proj(lo, width):
        return _dot(h_scr[...], w_ref[:, lo:lo + width])

    for hd in range(RET_HEADS):
        for out, base, scale in ((qr_o, P_QR, 1.0), (kr_o, P_KR, RET_DK ** -0.5)):
            y = proj(base + hd * RET_DK, RET_DK)
            x1, x2 = y[:, :half], y[:, half:]
            out[0, :, hd * RET_DK:hd * RET_DK + half] = ((x1 * cos - x2 * sin) * scale).astype(BF16)
            out[0, :, hd * RET_DK + half:(hd + 1) * RET_DK] = ((x1 * sin + x2 * cos) * scale).astype(BF16)
    cw = 512
    for c in range(N_VR // cw):
        vr_o[0, :, c * cw:(c + 1) * cw] = proj(P_VR + c * cw, cw).astype(BF16)
        gr_o[0, :, c * cw:(c + 1) * cw] = _silu(proj(P_GR + c * cw, cw)).astype(BF16)
    for c in range(D_MODEL // cw):
        ma_o[0, :, c * cw:(c + 1) * cw] = _sigmoid(proj(P_MA + c * cw, cw)).astype(BF16)
        mb_o[0, :, c * cw:(c + 1) * cw] = _sigmoid(proj(P_MB + c * cw, cw)).astype(BF16)
    per = cw // NSA_HD
    for c in range(N_QN // cw):
        y = (proj(P_QN + c * cw, cw) * (NSA_HD ** -0.5 * LOG2E)).astype(BF16)
        for j in range(per):
            qn_o[0, c * per + j] = y[:, j * NSA_HD:(j + 1) * NSA_HD]
    y32 = proj(P_KV, N_KV)
    y = y32.astype(BF16)
    y_t = y32.T.astype(BF16)
    tm = y.shape[0]
    ones_row = (lax.broadcasted_iota(jnp.int32, (LANES - NSA_HD, tm), 0) == 0).astype(BF16)
    for g in range(NSA_KV_GROUPS):
        lo = lambda j: (j * NSA_KV_GROUPS + g) * NSA_HD
        ks_o[0, g, :, :NSA_HD] = y[:, lo(2):lo(2) + NSA_HD]
        ks_o[0, g, :, NSA_HD:] = feat_ref[:, NSA_HD:]
        kw_o[0, g] = y[:, lo(4):lo(4) + NSA_HD]
        for out, j in ((vst_o, 3), (vwt_o, 5)):
            out[0, g, :NSA_HD, :] = y_t[lo(j):lo(j) + NSA_HD, :]
            out[0, g, NSA_HD:, :] = ones_row
    for s in range(2):
        kvc_scr[s] = y32[:, s * LANES:(s + 1) * LANES]
        for l in range(CMP_STRIDE):
            rows = kvc_scr[s, pl.ds(l, tm // CMP_STRIDE, stride=CMP_STRIDE), :].astype(BF16)
            for g in range(NSA_KV_GROUPS):
                kvc_o[s, 0, g, :, l * NSA_HD:(l + 1) * NSA_HD] = rows[:, g * NSA_HD:(g + 1) * NSA_HD]
    gates_t = _sigmoid(proj(P_GATE, N_GATE)).T
    for g in range(NSA_KV_GROUPS):
        gate_o[0, g] = gates_t[g * LANES:(g + 1) * LANES, :]


def _pack_w_in(w):
    s = [int(v) for v in _SEG]
    gate = w[:, s[6]:s[7]].reshape(D_MODEL, NSA_KV_GROUPS, 3 * NSA_HG)
    gate = jnp.pad(gate, ((0, 0), (0, 0), (0, LANES - 3 * NSA_HG))).reshape(D_MODEL, N_GATE)
    return jnp.concatenate([w[:, s[0]:s[6]], gate, w[:, s[7]:s[9]]], axis=1).astype(BF16)


def _in_projection(x, g, w_packed, cos, sin):
    B, T, D = x.shape
    G = NSA_KV_GROUPS
    tm = min(TM_PROJ, T)
    nt = T // tm
    assert T // SEL_BLOCK <= LANES - NSA_HD
    own_block = (np.arange(T)[:, None] // SEL_BLOCK) == (np.arange(LANES)[None, :] - NSA_HD)
    feat = jnp.asarray(np.where(own_block, NEG_INF, 0.0), BF16)
    row = lambda width: pl.BlockSpec((1, tm, width), lambda b, t: (b, t, 0))
    grp = lambda width: pl.BlockSpec((1, G, tm, width), lambda b, t: (b, 0, t, 0))
    grp_t = pl.BlockSpec((1, G, LANES, tm), lambda b, t: (b, 0, 0, t))
    grp_shape = lambda width: jax.ShapeDtypeStruct((B, G, T, width), BF16)
    grp_t_shape = lambda dt: jax.ShapeDtypeStruct((B, G, LANES, T), dt)
    out_shape = (
        jax.ShapeDtypeStruct((B, T, N_QR), BF16), jax.ShapeDtypeStruct((B, T, N_KR), BF16),
        jax.ShapeDtypeStruct((B, T, N_VR), BF16), jax.ShapeDtypeStruct((B, T, N_GR), BF16),
        jax.ShapeDtypeStruct((B, NSA_HEADS, T, NSA_HD), BF16),
        jax.ShapeDtypeStruct((2, B, G, T // CMP_STRIDE, CMP_STRIDE * NSA_HD), BF16),
        grp_shape(LANES), grp_t_shape(BF16), grp_shape(NSA_HD), grp_t_shape(BF16),
        grp_t_shape(F32),
        jax.ShapeDtypeStruct((B, T, D_MODEL), BF16), jax.ShapeDtypeStruct((B, T, D_MODEL), BF16))
    out_specs = (
        row(N_QR), row(N_KR), row(N_VR), row(N_GR),
        pl.BlockSpec((1, NSA_HEADS, tm, NSA_HD), lambda b, t: (b, 0, t, 0)),
        pl.BlockSpec((2, 1, G, tm // CMP_STRIDE, CMP_STRIDE * NSA_HD), lambda b, t: (0, b, 0, t, 0)),
        grp(LANES), grp_t, grp(NSA_HD), grp_t,
        grp_t,
        row(D_MODEL), row(D_MODEL))
    out_bytes = tm * (2 * (N_QR + N_KR + N_VR + N_GR + N_QN + 2 * D_MODEL) + 2 * G * (3 * NSA_HD + 3 * LANES)
                      + 4 * N_GATE)
    vmem = 2 * out_bytes + 2 * tm * D * 4 + 2 * D * P_END + tm * D * 2 + 10 * MIB
    return pl.pallas_call(
        _inproj_kernel,
        grid=(B, nt),
        in_specs=[pl.BlockSpec((1, tm, D), lambda b, t: (b, t, 0)),
                  pl.BlockSpec((1, D), lambda b, t: (0, 0)),
                  pl.BlockSpec((tm, RET_DK // 2), lambda b, t: (t, 0)),
                  pl.BlockSpec((tm, RET_DK // 2), lambda b, t: (t, 0)),
                  pl.BlockSpec((tm, LANES), lambda b, t: (t, 0)),
                  pl.BlockSpec((D, P_END), lambda b, t: (0, 0), pipeline_mode=pl.Buffered(1))],
        out_specs=out_specs,
        out_shape=out_shape,
        scratch_shapes=[pltpu.VMEM((tm, D), BF16), pltpu.VMEM((2, tm, G * NSA_HD), F32)],
        compiler_params=pltpu.CompilerParams(vmem_limit_bytes=int(vmem)),
        name="in_projection",
    )(x, g.reshape(1, D), cos, sin, feat, w_packed)


def _retention_kernel(q_ref, k_ref, v_ref, g_ref, dmask_ref, xi_ref, zeta_ref, gch_ref, o_ref):
    C = RET_BLOCK
    nc = q_ref.shape[1] // C
    dmask = dmask_ref[0]
    xi = xi_ref[0]
    zeta = zeta_ref[0]
    gch = gch_ref[0, :, 0:1]
    r = jnp.zeros((RET_DK, RET_DV), F32)
    for c in range(nc):
        rows = slice(c * C, (c + 1) * C)
        qc = q_ref[0, rows, :]
        kc = k_ref[0, rows, :]
        vc = v_ref[0, rows, :]
        s = _dot_nt(qc, kc) * dmask
        o = _dot(s.astype(BF16), vc)
        if c > 0:
            o = o + _dot(qc, r.astype(BF16)) * xi
        if c < nc - 1:
            kz = (kc.astype(F32) * zeta).astype(BF16)
            kv = lax.dot_general(kz, vc, (((0,), (0,)), ((), ())), preferred_element_type=F32)
            r = kv if c == 0 else gch * r + kv
        mu = jnp.mean(o, -1, keepdims=True)
        d = o - mu
        var = jnp.mean(d * d, -1, keepdims=True)
        on = d * lax.rsqrt(var + GN_EPS)
        o_ref[0, rows, :] = (g_ref[0, rows, :].astype(F32) * on).astype(BF16)


def _retention(qr, kr, vr, gr):
    B, T, _ = qr.shape
    C = RET_BLOCK
    lg = jnp.log(1.0 - 2.0 ** (-5.0 - jnp.arange(RET_HEADS, dtype=F32)))
    n = jnp.arange(C, dtype=F32)
    diff = n[:, None] - n[None, :]
    dmask = jnp.where(diff >= 0, jnp.exp(jnp.maximum(diff, 0.0)[None] * lg[:, None, None]), 0.0)
    xi = jnp.exp((n + 1.0)[None] * lg[:, None])[:, :, None]
    zeta = jnp.exp((C - 1.0 - n)[None] * lg[:, None])[:, :, None]
    gch = jnp.broadcast_to(jnp.exp(C * lg)[:, None, None], (RET_HEADS, 1, LANES))
    qk_spec = pl.BlockSpec((1, T, RET_DK), lambda b, h: (b, 0, h))
    v_spec = pl.BlockSpec((1, T, RET_DV), lambda b, h: (b, 0, h))
    return pl.pallas_call(
        _retention_kernel,
        grid=(B, RET_HEADS),
        in_specs=[qk_spec, qk_spec, v_spec, v_spec,
                  pl.BlockSpec((1, C, C), lambda b, h: (h, 0, 0)),
                  pl.BlockSpec((1, C, 1), lambda b, h: (h, 0, 0)),
                  pl.BlockSpec((1, C, 1), lambda b, h: (h, 0, 0)),
                  pl.BlockSpec((1, 1, LANES), lambda b, h: (h, 0, 0))],
        out_specs=v_spec,
        out_shape=jax.ShapeDtypeStruct((B, T, RET_HEADS * RET_DV), BF16),
        compiler_params=pltpu.CompilerParams(vmem_limit_bytes=32 * MIB),
        name="retention",
    )(qr, kr, vr, gr, dmask, xi, zeta, gch)


def _gelu_tanh(x):
    return 0.5 * x * (1.0 + jnp.tanh(math.sqrt(2.0 / math.pi) * (x + 0.044715 * (x * x * x))))


def _compress_kernel(x_ref, pk_ref, pv_ref, w1k_ref, w2k_ref, w1v_ref, w2vt_ref, ko_ref, vto_ref):
    half = CMP_STRIDE * NSA_HD

    def hidden(s, g, p_ref, w1_ref):
        x = x_ref[s, 0, g].astype(F32)
        a = _dot((x + p_ref[0:1, :]).astype(BF16), w1_ref[0:half, :])
        b = _dot((x + p_ref[1:2, :]).astype(BF16), w1_ref[half:2 * half, :])
        pre = a + pltpu.roll(b, b.shape[0] - 1, 0)
        return _gelu_tanh(pre).astype(BF16)

    for g in range(NSA_KV_GROUPS):
        ko_ref[0, g] = _dot(hidden(0, g, pk_ref, w1k_ref), w2k_ref[...]).astype(BF16)
        vto_ref[0, g] = _dot_nt(w2vt_ref[...], hidden(1, g, pv_ref, w1v_ref)).astype(BF16)


def _compress(kvc16, pos_k, pos_v, w1k, w2k, w1v, w2v):
    _, B, G, NC, F = kvc16.shape
    full = lambda a: pl.BlockSpec(a.shape, lambda b: (0,) * a.ndim)
    pk = pos_k.reshape(2, F)
    pv = pos_v.reshape(2, F)
    w = [a.astype(BF16) for a in (w1k, w2k, w1v, w2v.T)]
    return pl.pallas_call(
        _compress_kernel,
        grid=(B,),
        in_specs=[pl.BlockSpec((2, 1, G, NC, F), lambda b: (0, b, 0, 0, 0)), full(pk), full(pv)]
        + [full(a) for a in w],
        out_specs=(pl.BlockSpec((1, G, NC, NSA_HD), lambda b: (b, 0, 0, 0)),
                   pl.BlockSpec((1, G, NSA_HD, NC), lambda b: (b, 0, 0, 0))),
        out_shape=(jax.ShapeDtypeStruct((B, G, NC, NSA_HD), BF16), jax.ShapeDtypeStruct((B, G, NSA_HD, NC), BF16)),
        compiler_params=pltpu.CompilerParams(vmem_limit_bytes=32 * MIB),
        name="nsa_compress",
    )(kvc16, pk, pv, *w)


def _bias_kernel(tab_ref, nb_ref, cb_ref, near_o, cmp_o):
    h = pl.program_id(0)
    far = tab_ref[REL_BUCKETS - 1, h]

    def build(bk):
        acc = jnp.zeros(bk.shape, F32)
        for k in range(REL_BUCKETS - 1):
            acc = jnp.where(bk == k, (tab_ref[k, h] - far) * LOG2E, acc)
        return jnp.where(bk == MASKED_CODE, NEG_INF, acc)

    near_o[0] = build(nb_ref[...])
    cmp_o[0] = build(cb_ref[...])


def _t5_bucket(dist):
    dist = jnp.maximum(dist, 0)
    max_exact = REL_BUCKETS // 2
    large = max_exact + (jnp.log(jnp.maximum(dist, 1).astype(F32) / max_exact)
                         / math.log(REL_MAX_DIST / max_exact) * (REL_BUCKETS - max_exact)).astype(jnp.int32)
    large = jnp.minimum(large, REL_BUCKETS - 1)
    return jnp.where(dist < max_exact, dist, large)


def _bias_tables(rel_bias, T, tq):
    nc = T // CMP_STRIDE
    i = jnp.arange(tq, dtype=jnp.int32)
    d = jnp.asarray(NEAR_OFFSETS, jnp.int32)[:, None, None] * tq + i[None, None, :] - i[None, :, None]
    near_b = jnp.where((d < 0) | (d >= WINDOW), MASKED_CODE, _t5_bucket(d))
    near_b = jnp.concatenate([near_b, jnp.full((1, tq, tq), MASKED_CODE, jnp.int32)], axis=0)
    cmp_end = jnp.arange(nc, dtype=jnp.int32) * CMP_STRIDE + CMP_LEN - 1
    cmp_b = _t5_bucket(jnp.arange(T, dtype=jnp.int32)[None, :] - cmp_end[:, None])
    return pl.pallas_call(
        _bias_kernel,
        grid=(NSA_HEADS,),
        in_specs=[pl.BlockSpec(memory_space=pltpu.SMEM),
                  pl.BlockSpec(near_b.shape, lambda h: (0, 0, 0)),
                  pl.BlockSpec(cmp_b.shape, lambda h: (0, 0))],
        out_specs=(pl.BlockSpec((1,) + near_b.shape, lambda h: (h, 0, 0, 0)),
                   pl.BlockSpec((1,) + cmp_b.shape, lambda h: (h, 0, 0))),
        out_shape=(jax.ShapeDtypeStruct((NSA_HEADS,) + near_b.shape, F32),
                   jax.ShapeDtypeStruct((NSA_HEADS,) + cmp_b.shape, F32)),
        name="rel_bias_tables",
    )(rel_bias, near_b, cmp_b)


def _nsa_kernel(q_ref, kc_ref, vct_ref, ks_ref, vst_ref, kw_ref, vwt_ref, gt_ref,
                near_ref, cb_ref, ovl_ref, o_ref,
                qa_scr, s_scr, sw_scr, mxs_scr, mxw_scr, accs_scr, accw_scr, *, tq, sel_k):
    HG = NSA_HG
    R = HG * tq
    NC = kc_ref.shape[2]
    NS = ovl_ref.shape[0]
    UW = tq
    ti = pl.program_id(2)
    s0 = ti * tq
    q = q_ref[0, 0].reshape(R, NSA_HD)

    def head(x, h):
        return x[:, h * tq:(h + 1) * tq]

    def keys(ref, u, n=1):
        return ref[0, 0, pl.ds(pl.multiple_of(u * UW, UW), n * UW), :]

    def values_t(ref, u, n=1):
        return ref[0, 0, :, pl.ds(pl.multiple_of(u * UW, UW), n * UW)]

    def rows(slot, n=1):
        return pl.ds(pl.multiple_of(slot * UW, UW), n * UW)

    def logits(s, s_ref, slot, mx_ref):
        s_ref[rows(slot, s.shape[0] // UW), :] = s
        groups = [s[r * SUBLANES:(r + 1) * SUBLANES] for r in range(s.shape[0] // SUBLANES)]
        mx_ref[...] = functools.reduce(jnp.maximum, groups, mx_ref[...])

    def biased(s, e):
        return jnp.concatenate([head(s, h) + near_ref[h, e] for h in range(HG)], axis=1)

    def col_max(mx_ref):
        mx_ref[...] = jnp.broadcast_to(jnp.max(mx_ref[...], axis=0, keepdims=True), mx_ref.shape)

    def weigh(s_ref, slot, m_ref, vt_ref, u, acc_ref, n=1):
        vt = values_t(vt_ref, u, n)
        for h in range(HG):
            c = slice(h * tq, (h + 1) * tq)
            p = jnp.exp2((s_ref[rows(slot, n), c] - m_ref[0:1, c]).astype(BF16))
            acc_ref[:, c] += _dot(vt, p)

    def normalized(acc_ref):
        acc = acc_ref[...]
        return acc[:NSA_HD] / acc[NSA_HD:NSA_HD + 1]

    for ref in (mxs_scr, mxw_scr):
        ref[...] = jnp.full(ref.shape, NEG_INF, F32)
    for ref in (accs_scr, accw_scr):
        ref[...] = jnp.zeros_like(ref)

    masked = len(NEAR_OFFSETS)
    w0 = jnp.maximum(ti - 2, 0)

    def window_logits(slot):
        back = jnp.minimum(ti, 2) - slot
        logits(biased(_dot_nt(keys(kw_ref, w0 + slot), q), jnp.where(back >= 0, back, masked)),
               sw_scr, slot, mxw_scr)

    lc = _dot_nt(kc_ref[0, 0], q)
    lc = jnp.concatenate([head(lc, h) + cb_ref[h] for h in range(HG)], axis=1)
    pos_c = s0 + lax.broadcasted_iota(jnp.int32, (NC, tq), 1)
    cend = lax.broadcasted_iota(jnp.int32, (NC, tq), 0) * CMP_STRIDE + (CMP_LEN - 1)
    valid_c = jnp.concatenate([cend <= pos_c] * HG, axis=1)
    z = jnp.where(valid_c, lc, NEG_INF)
    z = z - jnp.max(z, 0, keepdims=True)
    e = jnp.exp2(z) * valid_c.astype(F32)
    pc = e / jnp.maximum(jnp.sum(e, 0, keepdims=True), 1e-30)
    oc = _dot(vct_ref[0, 0], pc.astype(BF16))
    psum = functools.reduce(jnp.add, [head(pc, h) for h in range(HG)])
    window_logits(0)

    p_hi = psum.astype(BF16)
    r1 = psum - p_hi.astype(F32)
    p_mid = r1.astype(BF16)
    p_lo = (r1 - p_mid.astype(F32)).astype(BF16)
    ovl = ovl_ref[...]
    imp = _dot(ovl, p_hi) + _dot(ovl, p_mid) + _dot(ovl, p_lo)
    jrow = lax.broadcasted_iota(jnp.int32, (NS, tq), 0)
    posl = s0 + lax.broadcasted_iota(jnp.int32, (NS, tq), 1)
    cur = lax.shift_right_logical(posl, int(math.log2(SEL_BLOCK)))
    forced = (jrow == 0) | (jrow == cur) | (jrow == cur - 1)
    started = jrow * SEL_BLOCK <= posl
    imp = jnp.where(forced, FORCED_SCORE, imp)
    imp = jnp.where(started, imp, NEG_INF)
    window_logits(1)
    sel = jnp.zeros((NS, tq), F32)
    for _ in range(sel_k):
        mx = jnp.max(imp, axis=0, keepdims=True)
        first = jnp.min(jnp.where(imp == mx, jrow, NS), axis=0, keepdims=True)
        hit = jrow == first
        sel = jnp.where(hit, 1.0, sel)
        imp = jnp.where(hit, PICKED, imp)
    window_logits(2)
    notsel = 1.0 - jnp.where(started, sel, 0.0)
    notsel = jnp.concatenate([notsel, jnp.zeros((LANES - NS, tq), F32)], axis=0)
    notsel = pltpu.roll(notsel.T, NSA_HD, 1).astype(BF16)
    for hg in range(HG):
        qa_scr[hg * tq:(hg + 1) * tq, :NSA_HD] = q_ref[0, 0, hg]
        qa_scr[hg * tq:(hg + 1) * tq, NSA_HD:] = notsel[:, NSA_HD:]
    qa = qa_scr[...]

    prev_u = jnp.where(ti >= 1, ti - 1, 1)
    logits(biased(_dot_nt(keys(ks_ref, ti), qa), 0), s_scr, ti, mxs_scr)
    logits(biased(_dot_nt(keys(ks_ref, prev_u), qa), jnp.where(ti >= 1, 1, masked)), s_scr, prev_u, mxs_scr)

    n_plain = jnp.maximum(ti - 1, 0)

    def plain_pair(i, carry):
        logits(_dot_nt(keys(ks_ref, 2 * i, 2), qa), s_scr, 2 * i, mxs_scr)
        return carry

    lax.fori_loop(0, lax.shift_right_logical(n_plain, 1), plain_pair, 0)

    @pl.when(lax.rem(n_plain, 2) == 1)
    def _():
        logits(_dot_nt(keys(ks_ref, n_plain - 1), qa), s_scr, n_plain - 1, mxs_scr)

    col_max(mxs_scr)
    col_max(mxw_scr)
    weigh(sw_scr, 0, mxw_scr, vwt_ref, w0, accw_scr, 3)
    weigh(s_scr, ti, mxs_scr, vst_ref, ti, accs_scr)
    weigh(s_scr, prev_u, mxs_scr, vst_ref, prev_u, accs_scr)

    def weigh_pair(i, carry):
        weigh(s_scr, 2 * i, mxs_scr, vst_ref, 2 * i, accs_scr, 2)
        return carry

    lax.fori_loop(0, lax.shift_right_logical(n_plain, 1), weigh_pair, 0)

    @pl.when(lax.rem(n_plain, 2) == 1)
    def _():
        weigh(s_scr, n_plain - 1, mxs_scr, vst_ref, n_plain - 1, accs_scr)

    gt = gt_ref[0, 0]
    o_s = normalized(accs_scr)
    o_w = normalized(accw_scr)

    def gated(h):
        return (gt[3 * h:3 * h + 1] * head(oc, h) + gt[3 * h + 1:3 * h + 2] * head(o_s, h)
                + gt[3 * h + 2:3 * h + 3] * head(o_w, h))

    pairs = [jnp.concatenate([gated(2 * j), gated(2 * j + 1)], axis=0).T for j in range(HG // 2)]
    o_ref[0] = jnp.concatenate(pairs, axis=1).astype(BF16)


def _nsa_attention(qn, kc, vct, ks, vst, kw, vwt, gates_t, near, cmpb):
    B, G, HG, T, hd = qn.shape
    tq = TQ
    assert T % tq == 0 and T >= 3 * tq and WINDOW == 2 * tq and tq >= 2 * REL_MAX_DIST and tq % LANES == 0
    nq = T // tq
    NC = kc.shape[2]
    NS = T // SEL_BLOCK
    sel_k = min(SEL_TOPK, NS)
    n = np.arange(NC)
    j = np.arange(NS)
    ovl = ((n[None, :] * CMP_STRIDE < (j[:, None] + 1) * SEL_BLOCK)
           & (n[None, :] * CMP_STRIDE + CMP_LEN - 1 >= j[:, None] * SEL_BLOCK) & (n[None, :] < NC - 1))
    ovl = jnp.asarray(ovl, BF16)
    seq = lambda rows, width: pl.BlockSpec((1, 1, rows, width), lambda g, b, t: (b, g, 0, 0))
    R = HG * tq
    W = HG * hd
    return pl.pallas_call(
        functools.partial(_nsa_kernel, tq=tq, sel_k=sel_k),
        grid=(G, B, nq),
        in_specs=[pl.BlockSpec((1, 1, HG, tq, hd), lambda g, b, t: (b, g, 0, t, 0)),
                  seq(NC, hd), seq(hd, NC), seq(T, LANES), seq(LANES, T), seq(T, hd), seq(LANES, T),
                  pl.BlockSpec((1, 1, LANES, tq), lambda g, b, t: (b, g, 0, t)),
                  pl.BlockSpec((HG, len(NEAR_OFFSETS) + 1, tq, tq), lambda g, b, t: (g, 0, 0, 0),
                               pipeline_mode=pl.Buffered(1)),
                  pl.BlockSpec((HG, NC, tq), lambda g, b, t: (g, 0, t)),
                  pl.BlockSpec(ovl.shape, lambda g, b, t: (0, 0))],
        out_specs=pl.BlockSpec((1, tq, W), lambda g, b, t: (b, t, g)),
        out_shape=jax.ShapeDtypeStruct((B, T, G * W), BF16),
        scratch_shapes=[pltpu.VMEM((R, LANES), BF16),
                        pltpu.VMEM((T, R), F32), pltpu.VMEM((3 * tq, R), F32),
                        pltpu.VMEM((SUBLANES, R), F32), pltpu.VMEM((SUBLANES, R), F32),
                        pltpu.VMEM((LANES, R), F32), pltpu.VMEM((LANES, R), F32)],
        compiler_params=pltpu.CompilerParams(vmem_limit_bytes=NSA_VMEM_LIMIT),
        name="nsa_attention",
    )(qn, kc, vct, ks, vst, kw, vwt, gates_t, near, cmpb, ovl)


def _merge_kernel(x_ref, oret_ref, on_ref, ma_ref, mb_ref, wr_ref, wn_ref, wo_ref, o_ref):
    y_ret = _dot(oret_ref[...], wr_ref[...])
    y_nsa = _dot(on_ref[...], wn_ref[...])
    mixed = ma_ref[...].astype(F32) * y_ret + mb_ref[...].astype(F32) * y_nsa
    o_ref[...] = x_ref[...] + _dot(mixed.astype(BF16), wo_ref[...])


def _merge(x2, oret, on, ma, mb, w_o_ret, w_o_nsa, w_out):
    M, D = x2.shape
    tm = min(TM_MERGE, M)
    row = lambda width: pl.BlockSpec((tm, width), lambda i: (i, 0))
    res = lambda a: pl.BlockSpec(a.shape, lambda i: (0, 0), pipeline_mode=pl.Buffered(1))
    w = [a.astype(BF16) for a in (w_o_ret, w_o_nsa, w_out)]
    return pl.pallas_call(
        _merge_kernel,
        grid=(M // tm,),
        in_specs=[row(D), row(oret.shape[1]), row(D), row(D), row(D)] + [res(a) for a in w],
        out_specs=row(D),
        out_shape=jax.ShapeDtypeStruct((M, D), F32),
        compiler_params=pltpu.CompilerParams(vmem_limit_bytes=48 * MIB),
        name="merge_out_projection",
    )(x2, oret, on, ma, mb, *w)


def _rms(x, g):
    return x * lax.rsqrt(jnp.mean(x * x, -1, keepdims=True) + EPS) * g


def _ffn_kernel(x_ref, g_ref, wi_ref, wo_ref, gf_ref, o_ref, h_scr, acc_scr, *, final_norm):
    x = x_ref[...]
    h_scr[...] = _rms(x, g_ref[...]).astype(BF16)
    acc_scr[...] = x
    cw = FFN_HIDDEN // 2
    for c in range(2):
        a = _dot(h_scr[...], wi_ref[:, c * cw:(c + 1) * cw])
        b = _dot(h_scr[...], wi_ref[:, FFN_HIDDEN + c * cw:FFN_HIDDEN + (c + 1) * cw])
        acc_scr[...] += _dot((_silu(a) * b).astype(BF16), wo_ref[c * cw:(c + 1) * cw, :])
    y = acc_scr[...]
    o_ref[...] = _rms(y, gf_ref[...]) if final_norm else y


def _ffn(x2, g, w_in, w_out, g_final, final_norm):
    M, D = x2.shape
    tm = min(TM_MERGE, M)
    row = pl.BlockSpec((tm, D), lambda i: (i, 0))
    vec = pl.BlockSpec((1, D), lambda i: (0, 0))
    res = lambda a: pl.BlockSpec(a.shape, lambda i: (0, 0), pipeline_mode=pl.Buffered(1))
    wi, wo = w_in.astype(BF16), w_out.astype(BF16)
    return pl.pallas_call(
        functools.partial(_ffn_kernel, final_norm=final_norm),
        grid=(M // tm,),
        in_specs=[row, vec, res(wi), res(wo), vec],
        out_specs=row,
        out_shape=jax.ShapeDtypeStruct((M, D), F32),
        scratch_shapes=[pltpu.VMEM((tm, D), BF16), pltpu.VMEM((tm, D), F32)],
        compiler_params=pltpu.CompilerParams(vmem_limit_bytes=52 * MIB),
        name="swiglu_ffn",
    )(x2, g.reshape(1, D), wi, wo, g_final.reshape(1, D))


def kernel(x, norm_mix_g, w_in, cmp_pos_k, cmp_pos_v, cmp_w1_k, cmp_w2_k, cmp_w1_v, cmp_w2_v, w_o_ret, w_o_nsa,
           w_out, norm_ffn_g, w_ffn_in, w_ffn_out, rel_bias, norm_final_g):
    B, T, D = x.shape
    depth = w_in.shape[0]
    G, HG = NSA_KV_GROUPS, NSA_HG
    half = RET_DK // 2
    freqs = ROPE_BASE ** (-jnp.arange(half, dtype=F32) / half)
    ang = jnp.arange(T, dtype=jnp.int32).astype(F32)[:, None] * freqs
    cos, sin = jnp.cos(ang), jnp.sin(ang)
    near, cmpb = _bias_tables(rel_bias, T, TQ)
    for i in range(depth):
        qr, kr, vr, gr, qn, kvc16, ks, vst, kw, vwt, gates_t, ma, mb = _in_projection(
            x, norm_mix_g[i], _pack_w_in(w_in[i]), cos, sin)
        o_ret = _retention(qr, kr, vr, gr)
        kc, vct = _compress(kvc16,cmp_pos_k[i], cmp_pos_v[i], cmp_w1_k[i], cmp_w2_k[i], cmp_w1_v[i], cmp_w2_v[i])
        o_n = _nsa_attention(qn.reshape(B, G, HG, T, NSA_HD), kc, vct, ks, vst, kw, vwt, gates_t, near, cmpb)
        o_n = o_n.reshape(B * T, NSA_HEADS * NSA_HD)
        x2 = _merge(x.reshape(B * T, D), o_ret.reshape(B * T, -1), o_n, ma.reshape(B * T, D), mb.reshape(B * T, D),
                    w_o_ret[i], w_o_nsa[i], w_out[i])
        x2 = _ffn(x2, norm_ffn_g[i], w_ffn_in[i], w_ffn_out[i], norm_final_g, final_norm=(i == depth - 1))
        x = x2.reshape(B, T, D)
    return x
```

```python
import functools
import math

import jax
import jax.numpy as jnp
import numpy as np
from jax import lax
from jax.experimental import pallas as pl
from jax.experimental.pallas import tpu as pltpu

F32 = jnp.float32
BF16 = jnp.bfloat16

D_MODEL = 1024
RET_HEADS = 4
RET_DK = 256
RET_DV = 512
RET_BLOCK = 256
ROPE_BASE = 10000.0
GN_EPS = 1e-5
NSA_HEADS = 16
NSA_KV_GROUPS = 2
NSA_HG = NSA_HEADS // NSA_KV_GROUPS
NSA_HD = 64
CMP_LEN = 32
CMP_STRIDE = 16
CMP_HIDDEN = 256
SEL_BLOCK = 64
SEL_TOPK = 8
WINDOW = 512
FORCED_SCORE = 1e4
REL_BUCKETS = 32
REL_MAX_DIST = 128
FFN_HIDDEN = -(-8 * D_MODEL // (3 * 256)) * 256
EPS = 1e-6
NEG_INF = -1e30
PICKED = -3e38
MASKED_CODE = REL_BUCKETS
NEAR_OFFSETS = (0, 1, 2)
LOG2E = math.log2(math.e)

LANES = 128
SUBLANES = 8
MIB = 1024 * 1024
NSA_VMEM_LIMIT = 56 * MIB

_SEG = np.cumsum([0, RET_HEADS * RET_DK, RET_HEADS * RET_DK, RET_HEADS * RET_DV, RET_HEADS * RET_DV,
                  NSA_HEADS * NSA_HD, 6 * NSA_KV_GROUPS * NSA_HD, 3 * NSA_HEADS, D_MODEL, D_MODEL])
N_QR, N_KR, N_VR, N_GR = RET_HEADS * RET_DK, RET_HEADS * RET_DK, RET_HEADS * RET_DV, RET_HEADS * RET_DV
N_QN, N_KV = NSA_HEADS * NSA_HD, 6 * NSA_KV_GROUPS * NSA_HD
N_GATE = NSA_KV_GROUPS * LANES
_P = np.cumsum([0, N_QR, N_KR, N_VR, N_GR, N_QN, N_KV, N_GATE, D_MODEL, D_MODEL])
P_QR, P_KR, P_VR, P_GR, P_QN, P_KV, P_GATE, P_MA, P_MB, P_END = [int(v) for v in _P]

TM_PROJ = 512
TM_MERGE = 512
TQ = 256
NT = (((1,), (1,)), ((), ()))


def _dot(a, b):
    return jnp.dot(a, b, preferred_element_type=F32)


def _dot_nt(a, b):
    return lax.dot_general(a, b, NT, preferred_element_type=F32)


def _sigmoid(x):
    return 1.0 / (1.0 + jnp.exp(-x))


def _silu(x):
    return x * _sigmoid(x)


def _inproj_kernel(x_ref, g_ref, cos_ref, sin_ref, feat_ref, w_ref,
                   qr_o, kr_o, vr_o, gr_o, qn_o, kvc_o, ks_o, vst_o, kw_o, vwt_o, gate_o, ma_o, mb_o, h_scr, kvc_scr):
    x = x_ref[0]
    h = x * lax.rsqrt(jnp.mean(x * x, -1, keepdims=True) + EPS) * g_ref[...]
    h_scr[...] = h.astype(BF16)
    cos = cos_ref[...]
    sin = sin_ref[...]
    half = RET_DK // 2

    def proj(lo, width):
        return _dot(h_scr[...], w_ref[:, lo:lo + width])

    for hd in range(RET_HEADS):
        for out, base, scale in ((qr_o, P_QR, 1.0), (kr_o, P_KR, RET_DK ** -0.5)):
            y = proj(base + hd * RET_DK, RET_DK)
            x1, x2 = y[:, :half], y[:, half:]
            out[0, :, hd * RET_DK:hd * RET_DK + half] = ((x1 * cos - x2 * sin) * scale).astype(BF16)
            out[0, :, hd * RET_DK + half:(hd + 1) * RET_DK] = ((x1 * sin + x2 * cos) * scale).astype(BF16)
    cw = 512
    for c in range(N_VR // cw):
        vr_o[0, :, c * cw:(c + 1) * cw] = proj(P_VR + c * cw, cw).astype(BF16)
        gr_o[0, :, c * cw:(c + 1) * cw] = _silu(proj(P_GR + c * cw, cw)).astype(BF16)
    for c in range(D_MODEL // cw):
        ma_o[0, :, c * cw:(c + 1) * cw] = _sigmoid(proj(P_MA + c * cw, cw)).astype(BF16)
        mb_o[0, :, c * cw:(c + 1) * cw] = _sigmoid(proj(P_MB + c * cw, cw)).astype(BF16)
    per = cw // NSA_HD
    for c in range(N_QN // cw):
        y = (proj(P_QN + c * cw, cw) * (NSA_HD ** -0.5 * LOG2E)).astype(BF16)
        for j in range(per):
            qn_o[0, c * per + j] = y[:, j * NSA_HD:(j + 1) * NSA_HD]
    y32 = proj(P_KV, N_KV)
    y = y32.astype(BF16)
    y_t = y32.T.astype(BF16)
    tm = y.shape[0]
    ones_row = (lax.broadcasted_iota(jnp.int32, (LANES - NSA_HD, tm), 0) == 0).astype(BF16)
    for g in range(NSA_KV_GROUPS):
        lo = lambda j: (j * NSA_KV_GROUPS + g) * NSA_HD
        ks_o[0, g, :, :NSA_HD] = y[:, lo(2):lo(2) + NSA_HD]
        ks_o[0, g, :, NSA_HD:] = feat_ref[:, NSA_HD:]
        kw_o[0, g] = y[:, lo(4):lo(4) + NSA_HD]
        for out, j in ((vst_o, 3), (vwt_o, 5)):
            out[0, g, :NSA_HD, :] = y_t[lo(j):lo(j) + NSA_HD, :]
            out[0, g, NSA_HD:, :] = ones_row
    for s in range(2):
        kvc_scr[s] = y32[:, s * LANES:(s + 1) * LANES]
        for l in range(CMP_STRIDE):
            rows = kvc_scr[s, pl.ds(l, tm // CMP_STRIDE, stride=CMP_STRIDE), :].astype(BF16)
            for g in range(NSA_KV_GROUPS):
                kvc_o[s, 0, g, :, l * NSA_HD:(l + 1) * NSA_HD] = rows[:, g * NSA_HD:(g + 1) * NSA_HD]
    gates_t = _sigmoid(proj(P_GATE, N_GATE)).T
    for g in range(NSA_KV_GROUPS):
        gate_o[0, g] = gates_t[g * LANES:(g + 1) * LANES, :]


def _pack_w_in(w):
    s = [int(v) for v in _SEG]
    gate = w[:, s[6]:s[7]].reshape(D_MODEL, NSA_KV_GROUPS, 3 * NSA_HG)
    gate = jnp.pad(gate, ((0, 0), (0, 0), (0, LANES - 3 * NSA_HG))).reshape(D_MODEL, N_GATE)
    return jnp.concatenate([w[:, s[0]:s[6]], gate, w[:, s[7]:s[9]]], axis=1).astype(BF16)


def _in_projection(x, g, w_packed, cos, sin):
    B, T, D = x.shape
    G = NSA_KV_GROUPS
    tm = min(TM_PROJ, T)
    nt = T // tm
    assert T // SEL_BLOCK <= LANES - NSA_HD
    own_block = (np.arange(T)[:, None] // SEL_BLOCK) == (np.arange(LANES)[None, :] - NSA_HD)
    feat = jnp.asarray(np.where(own_block, NEG_INF, 0.0), BF16)
    row = lambda width: pl.BlockSpec((1, tm, width), lambda b, t: (b, t, 0))
    grp = lambda width: pl.BlockSpec((1, G, tm, width), lambda b, t: (b, 0, t, 0))
    grp_t = pl.BlockSpec((1, G, LANES, tm), lambda b, t: (b, 0, 0, t))
    grp_shape = lambda width: jax.ShapeDtypeStruct((B, G, T, width), BF16)
    grp_t_shape = lambda dt: jax.ShapeDtypeStruct((B, G, LANES, T), dt)
    out_shape = (
        jax.ShapeDtypeStruct((B, T, N_QR), BF16), jax.ShapeDtypeStruct((B, T, N_KR), BF16),
        jax.ShapeDtypeStruct((B, T, N_VR), BF16), jax.ShapeDtypeStruct((B, T, N_GR), BF16),
        jax.ShapeDtypeStruct((B, NSA_HEADS, T, NSA_HD), BF16),
        jax.ShapeDtypeStruct((2, B, G, T // CMP_STRIDE, CMP_STRIDE * NSA_HD), BF16),
        grp_shape(LANES), grp_t_shape(BF16), grp_shape(NSA_HD), grp_t_shape(BF16),
        grp_t_shape(F32),
        jax.ShapeDtypeStruct((B, T, D_MODEL), BF16), jax.ShapeDtypeStruct((B, T, D_MODEL), BF16))
    out_specs = (
        row(N_QR), row(N_KR), row(N_VR), row(N_GR),
        pl.BlockSpec((1, NSA_HEADS, tm, NSA_HD), lambda b, t: (b, 0, t, 0)),
        pl.BlockSpec((2, 1, G, tm // CMP_STRIDE, CMP_STRIDE * NSA_HD), lambda b, t: (0, b, 0, t, 0)),
        grp(LANES), grp_t, grp(NSA_HD), grp_t,
        grp_t,
        row(D_MODEL), row(D_MODEL))
    out_bytes = tm * (2 * (N_QR + N_KR + N_VR + N_GR + N_QN + 2 * D_MODEL) + 2 * G * (3 * NSA_HD + 3 * LANES)
                      + 4 * N_GATE)
    vmem = 2 * out_bytes + 2 * tm * D * 4 + 2 * D * P_END + tm * D * 2 + 10 * MIB
    return pl.pallas_call(
        _inproj_kernel,
        grid=(B, nt),
        in_specs=[pl.BlockSpec((1, tm, D), lambda b, t: (b, t, 0)),
                  pl.BlockSpec((1, D), lambda b, t: (0, 0)),
                  pl.BlockSpec((tm, RET_DK // 2), lambda b, t: (t, 0)),
                  pl.BlockSpec((tm, RET_DK // 2), lambda b, t: (t, 0)),
                  pl.BlockSpec((tm, LANES), lambda b, t: (t, 0)),
                  pl.BlockSpec((D, P_END), lambda b, t: (0, 0), pipeline_mode=pl.Buffered(1))],
        out_specs=out_specs,
        out_shape=out_shape,
        scratch_shapes=[pltpu.VMEM((tm, D), BF16), pltpu.VMEM((2, tm, G * NSA_HD), F32)],
        compiler_params=pltpu.CompilerParams(vmem_limit_bytes=int(vmem)),
        name="in_projection",
    )(x, g.reshape(1, D), cos, sin, feat, w_packed)


def _retention_kernel(q_ref, k_ref, v_ref, g_ref, dmask_ref, xi_ref, zeta_ref, gch_ref, o_ref):
    C = RET_BLOCK
    nc = q_ref.shape[1] // C
    dmask = dmask_ref[0]
    xi = xi_ref[0]
    zeta = zeta_ref[0]
    gch = gch_ref[0, :, 0:1]
    r = jnp.zeros((RET_DK, RET_DV), F32)
    for c in range(nc):
        rows = slice(c * C, (c + 1) * C)
        qc = q_ref[0, rows, :]
        kc = k_ref[0, rows, :]
        vc = v_ref[0, rows, :]
        s = _dot_nt(qc, kc) * dmask
        o = _dot(s.astype(BF16), vc)
        if c > 0:
            o = o + _dot(qc, r.astype(BF16)) * xi
        if c < nc - 1:
            kz = (kc.astype(F32) * zeta).astype(BF16)
            kv = lax.dot_general(kz, vc, (((0,), (0,)), ((), ())), preferred_element_type=F32)
            r = kv if c == 0 else gch * r + kv
        mu = jnp.mean(o, -1, keepdims=True)
        d = o - mu
        var = jnp.mean(d * d, -1, keepdims=True)
        on = d * lax.rsqrt(var + GN_EPS)
        o_ref[0, rows, :] = (g_ref[0, rows, :].astype(F32) * on).astype(BF16)


def _retention(qr, kr, vr, gr):
    B, T, _ = qr.shape
    C = RET_BLOCK
    lg = jnp.log(1.0 - 2.0 ** (-5.0 - jnp.arange(RET_HEADS, dtype=F32)))
    n = jnp.arange(C, dtype=F32)
    diff = n[:, None] - n[None, :]
    dmask = jnp.where(diff >= 0, jnp.exp(jnp.maximum(diff, 0.0)[None] * lg[:, None, None]), 0.0)
    xi = jnp.exp((n + 1.0)[None] * lg[:, None])[:, :, None]
    zeta = jnp.exp((C - 1.0 - n)[None] * lg[:, None])[:, :, None]
    gch = jnp.broadcast_to(jnp.exp(C * lg)[:, None, None], (RET_HEADS, 1, LANES))
    qk_spec = pl.BlockSpec((1, T, RET_DK), lambda b, h: (b, 0, h))
    v_spec = pl.BlockSpec((1, T, RET_DV), lambda b, h: (b, 0, h))
    return pl.pallas_call(
        _retention_kernel,
        grid=(B, RET_HEADS),
        in_specs=[qk_spec, qk_spec, v_spec, v_spec,
                  pl.BlockSpec((1, C, C), lambda b, h: (h, 0, 0)),
                  pl.BlockSpec((1, C, 1), lambda b, h: (h, 0, 0)),
                  pl.BlockSpec((1, C, 1), lambda b, h: (h, 0, 0)),
                  pl.BlockSpec((1, 1, LANES), lambda b, h: (h, 0, 0))],
        out_specs=v_spec,
        out_shape=jax.ShapeDtypeStruct((B, T, RET_HEADS * RET_DV), BF16),
        compiler_params=pltpu.CompilerParams(vmem_limit_bytes=32 * MIB),
        name="retention",
    )(qr, kr, vr, gr, dmask, xi, zeta, gch)


def _gelu_tanh(x):
    return 0.5 * x * (1.0 + jnp.tanh(math.sqrt(2.0 / math.pi) * (x + 0.044715 * (x * x * x))))


def _compress_kernel(x_ref, pk_ref, pv_ref, w1k_ref, w2k_ref, w1v_ref, w2vt_ref, ko_ref, vto_ref):
    half = CMP_STRIDE * NSA_HD

    def hidden(s, g, p_ref, w1_ref):
        x = x_ref[s, 0, g].astype(F32)
        a = _dot((x + p_ref[0:1, :]).astype(BF16), w1_ref[0:half, :])
        b = _dot((x + p_ref[1:2, :]).astype(BF16), w1_ref[half:2 * half, :])
        pre = a + pltpu.roll(b, b.shape[0] - 1, 0)
        return _gelu_tanh(pre).astype(BF16)

    for g in range(NSA_KV_GROUPS):
        ko_ref[0, g] = _dot(hidden(0, g, pk_ref, w1k_ref), w2k_ref[...]).astype(BF16)
        vto_ref[0, g] = _dot_nt(w2vt_ref[...], hidden(1, g, pv_ref, w1v_ref)).astype(BF16)


def _compress(kvc16, pos_k, pos_v, w1k, w2k, w1v, w2v):
    _, B, G, NC, F = kvc16.shape
    full = lambda a: pl.BlockSpec(a.shape, lambda b: (0,) * a.ndim)
    pk = pos_k.reshape(2, F)
    pv = pos_v.reshape(2, F)
    w = [a.astype(BF16) for a in (w1k, w2k, w1v, w2v.T)]
    return pl.pallas_call(
        _compress_kernel,
        grid=(B,),
        in_specs=[pl.BlockSpec((2, 1, G, NC, F), lambda b: (0, b, 0, 0, 0)), full(pk), full(pv)]
        + [full(a) for a in w],
        out_specs=(pl.BlockSpec((1, G, NC, NSA_HD), lambda b: (b, 0, 0, 0)),
                   pl.BlockSpec((1, G, NSA_HD, NC), lambda b: (b, 0, 0, 0))),
        out_shape=(jax.ShapeDtypeStruct((B, G, NC, NSA_HD), BF16), jax.ShapeDtypeStruct((B, G, NSA_HD, NC), BF16)),
        compiler_params=pltpu.CompilerParams(vmem_limit_bytes=32 * MIB),
        name="nsa_compress",
    )(kvc16, pk, pv, *w)


def _bias_kernel(tab_ref, nb_ref, cb_ref, near_o, cmp_o):
    h = pl.program_id(0)
    far = tab_ref[REL_BUCKETS - 1, h]

    def build(bk):
        acc = jnp.zeros(bk.shape, F32)
        for k in range(REL_BUCKETS - 1):
            acc = jnp.where(bk == k, (tab_ref[k, h] - far) * LOG2E, acc)
        return jnp.where(bk == MASKED_CODE, NEG_INF, acc)

    near_o[0] = build(nb_ref[...])
    cmp_o[0] = build(cb_ref[...])


def _t5_bucket(dist):
    dist = jnp.maximum(dist, 0)
    max_exact = REL_BUCKETS // 2
    large = max_exact + (jnp.log(jnp.maximum(dist, 1).astype(F32) / max_exact)
                         / math.log(REL_MAX_DIST / max_exact) * (REL_BUCKETS - max_exact)).astype(jnp.int32)
    large = jnp.minimum(large, REL_BUCKETS - 1)
    return jnp.where(dist < max_exact, dist, large)


def _bias_tables(rel_bias, T, tq):
    nc = T // CMP_STRIDE
    i = jnp.arange(tq, dtype=jnp.int32)
    d = jnp.asarray(NEAR_OFFSETS, jnp.int32)[:, None, None] * tq + i[None, None, :] - i[None, :, None]
    near_b = jnp.where((d < 0) | (d >= WINDOW), MASKED_CODE, _t5_bucket(d))
    near_b = jnp.concatenate([near_b, jnp.full((1, tq, tq), MASKED_CODE, jnp.int32)], axis=0)
    cmp_end = jnp.arange(nc, dtype=jnp.int32) * CMP_STRIDE + CMP_LEN - 1
    cmp_b = _t5_bucket(jnp.arange(T, dtype=jnp.int32)[None, :] - cmp_end[:, None])
    return pl.pallas_call(
        _bias_kernel,
        grid=(NSA_HEADS,),
        in_specs=[pl.BlockSpec(memory_space=pltpu.SMEM),
                  pl.BlockSpec(near_b.shape, lambda h: (0, 0, 0)),
                  pl.BlockSpec(cmp_b.shape, lambda h: (0, 0))],
        out_specs=(pl.BlockSpec((1,) + near_b.shape, lambda h: (h, 0, 0, 0)),
                   pl.BlockSpec((1,) + cmp_b.shape, lambda h: (h, 0, 0))),
        out_shape=(jax.ShapeDtypeStruct((NSA_HEADS,) + near_b.shape, F32),
                   jax.ShapeDtypeStruct((NSA_HEADS,) + cmp_b.shape, F32)),
        name="rel_bias_tables",
    )(rel_bias, near_b, cmp_b)


def _nsa_kernel(q_ref, kc_ref, vct_ref, ks_ref, vst_ref, kw_ref, vwt_ref, gt_ref,
                near_ref, cb_ref, ovl_ref, o_ref,
                qa_scr, sa_scr, sb_scr, cma_scr, cmb_scr, ms_scr, mw_scr, accs_scr, accw_scr, *, tq, sel_k):
    HG = NSA_HG
    R = HG * tq
    NC = kc_ref.shape[2]
    NS = ovl_ref.shape[0]
    UW = tq
    ti = pl.program_id(2)
    s0 = ti * tq
    q = q_ref[0, 0].reshape(R, NSA_HD)

    def head(x, h):
        return x[:, h * tq:(h + 1) * tq]

    def keys(ref, u, n=1):
        return ref[0, 0, pl.ds(pl.multiple_of(u * UW, UW), n * UW), :]

    def values_t(ref, u, n=1):
        return ref[0, 0, :, pl.ds(pl.multiple_of(u * UW, UW), n * UW)]

    cmax = lambda s_ref: cma_scr if s_ref is sa_scr else cmb_scr

    def score(s_ref, k_ref, u, n, queries, entries=None):
        def produce(h):
            s = _dot_nt(keys(k_ref, u, n), queries(h))
            if entries is not None:
                s = jnp.concatenate([s[i * UW:(i + 1) * UW] + near_ref[h, e] for i, e in enumerate(entries)], axis=0)
            s_ref[0:n * UW, h * tq:(h + 1) * tq] = s
            groups = [s[r * SUBLANES:(r + 1) * SUBLANES] for r in range(n * UW // SUBLANES)]
            cmax(s_ref)[:, h * tq:(h + 1) * tq] = functools.reduce(jnp.maximum, groups)
        return produce

    def fold(s_ref, n, m_ref, vt_ref, u, acc_ref):
        def consume(h):
            c = slice(h * tq, (h + 1) * tq)
            m_old = m_ref[0:1, c]
            m_new = jnp.maximum(m_old, jnp.max(cmax(s_ref)[:, c], axis=0, keepdims=True))
            part = [_dot(values_t(vt_ref, u + i), jnp.exp2((s_ref[i * UW:(i + 1) * UW, c] - m_new).astype(BF16)))
                    for i in range(n)]
            acc_ref[:, c] = jnp.exp2(m_old - m_new) * acc_ref[:, c] + functools.reduce(jnp.add, part)
            m_ref[:, c] = jnp.broadcast_to(m_new, (SUBLANES, tq))
        return consume

    def stage(produce=None, consume=None):
        for h in range(HG):
            if produce is not None:
                produce(h)
            if consume is not None:
                consume(h)

    q_head = lambda h: q_ref[0, 0, h]
    qa_head = lambda h: qa_scr[h * tq:(h + 1) * tq, :]

    def normalized(acc_ref):
        acc = acc_ref[...]
        return acc[:NSA_HD] / acc[NSA_HD:NSA_HD + 1]

    for ref in (ms_scr, mw_scr):
        ref[...] = jnp.full(ref.shape, NEG_INF, F32)
    for ref in (accs_scr, accw_scr):
        ref[...] = jnp.zeros_like(ref)

    masked = len(NEAR_OFFSETS)
    w0 = jnp.maximum(ti - 2, 0)

    def window_entry(slot):
        back = jnp.minimum(ti, 2) - slot
        return jnp.where(back >= 0, back, masked)

    win_a = (sa_scr, 2, mw_scr, vwt_ref, w0, accw_scr)
    win_b = (sb_scr, 1, mw_scr, vwt_ref, w0 + 2, accw_scr)
    stage(score(sa_scr, kw_ref, w0, 2, q_head, (window_entry(0), window_entry(1))))

    lc = _dot_nt(kc_ref[0, 0], q)
    lc = jnp.concatenate([head(lc, h) + cb_ref[h] for h in range(HG)], axis=1)
    pos_c = s0 + lax.broadcasted_iota(jnp.int32, (NC, tq), 1)
    cend = lax.broadcasted_iota(jnp.int32, (NC, tq), 0) * CMP_STRIDE + (CMP_LEN - 1)
    valid_c = jnp.concatenate([cend <= pos_c] * HG, axis=1)
    z = jnp.where(valid_c, lc, NEG_INF)
    z = z - jnp.max(z, 0, keepdims=True)
    e = jnp.exp2(z) * valid_c.astype(F32)
    pc = e / jnp.maximum(jnp.sum(e, 0, keepdims=True), 1e-30)
    oc = _dot(vct_ref[0, 0], pc.astype(BF16))
    psum = functools.reduce(jnp.add, [head(pc, h) for h in range(HG)])
    stage(score(sb_scr, kw_ref, w0 + 2, 1, q_head, (window_entry(2),)), fold(*win_a))

    p_hi = psum.astype(BF16)
    r1 = psum - p_hi.astype(F32)
    p_mid = r1.astype(BF16)
    p_lo = (r1 - p_mid.astype(F32)).astype(BF16)
    ovl = ovl_ref[...]
    imp = _dot(ovl, p_hi) + _dot(ovl, p_mid) + _dot(ovl, p_lo)
    jrow = lax.broadcasted_iota(jnp.int32, (NS, tq), 0)
    posl = s0 + lax.broadcasted_iota(jnp.int32, (NS, tq), 1)
    cur = lax.shift_right_logical(posl, int(math.log2(SEL_BLOCK)))
    forced = (jrow == 0) | (jrow == cur) | (jrow == cur - 1)
    started = jrow * SEL_BLOCK <= posl
    imp = jnp.where(forced, FORCED_SCORE, imp)
    imp = jnp.where(started, imp, NEG_INF)
    sel = jnp.zeros((NS, tq), F32)
    for _ in range(sel_k):
        mx = jnp.max(imp, axis=0, keepdims=True)
        first = jnp.min(jnp.where(imp == mx, jrow, NS), axis=0, keepdims=True)
        hit = jrow == first
        sel = jnp.where(hit, 1.0, sel)
        imp = jnp.where(hit, PICKED, imp)
    stage(None, fold(*win_b))
    notsel = 1.0 - jnp.where(started, sel, 0.0)
    notsel = jnp.concatenate([notsel, jnp.zeros((LANES - NS, tq), F32)], axis=0)
    notsel = pltpu.roll(notsel.T, NSA_HD, 1).astype(BF16)
    for hg in range(HG):
        qa_scr[hg * tq:(hg + 1) * tq, :NSA_HD] = q_ref[0, 0, hg]
        qa_scr[hg * tq:(hg + 1) * tq, NSA_HD:] = notsel[:, NSA_HD:]

    prev_u = jnp.where(ti >= 1, ti - 1, 1)
    sel_job = lambda s_ref, u, n: (s_ref, n, ms_scr, vst_ref, u, accs_scr)
    stage(score(sa_scr, ks_ref, ti, 1, qa_head, (0,)))
    stage(score(sb_scr, ks_ref, prev_u, 1, qa_head, (jnp.where(ti >= 1, 1, masked),)), fold(*sel_job(sa_scr, ti, 1)))
    stage(None, fold(*sel_job(sb_scr, prev_u, 1)))

    n_plain = jnp.maximum(ti - 1, 0)
    n_quads = lax.shift_right_logical(n_plain, 2)

    def plain_quad(j, carry):
        stage(score(sa_scr, ks_ref, 4 * j, 2, qa_head))
        stage(score(sb_scr, ks_ref, 4 * j + 2, 2, qa_head), fold(*sel_job(sa_scr, 4 * j, 2)))
        stage(None, fold(*sel_job(sb_scr, 4 * j + 2, 2)))
        return carry

    lax.fori_loop(0, n_quads, plain_quad, 0)
    left = n_plain - 4 * n_quads

    @pl.when(left >= 2)
    def _():
        stage(score(sa_scr, ks_ref, 4 * n_quads, 2, qa_head))
        stage(None, fold(*sel_job(sa_scr, 4 * n_quads, 2)))

    @pl.when(lax.rem(left, 2) == 1)
    def _():
        stage(score(sa_scr, ks_ref, n_plain - 1, 1, qa_head))
        stage(None, fold(*sel_job(sa_scr, n_plain - 1, 1)))

    gt = gt_ref[0, 0]
    o_s = normalized(accs_scr)
    o_w = normalized(accw_scr)

    def gated(h):
        return (gt[3 * h:3 * h + 1] * head(oc, h) + gt[3 * h + 1:3 * h + 2] * head(o_s, h)
                + gt[3 * h + 2:3 * h + 3] * head(o_w, h))

    pairs = [jnp.concatenate([gated(2 * j), gated(2 * j + 1)], axis=0).T for j in range(HG // 2)]
    o_ref[0] = jnp.concatenate(pairs, axis=1).astype(BF16)


def _nsa_attention(qn, kc, vct, ks, vst, kw, vwt, gates_t, near, cmpb):
    B, G, HG, T, hd = qn.shape
    tq = TQ
    assert T % tq == 0 and T >= 3 * tq and WINDOW == 2 * tq and tq >= 2 * REL_MAX_DIST and tq % LANES == 0
    nq = T // tq
    NC = kc.shape[2]
    NS = T // SEL_BLOCK
    sel_k = min(SEL_TOPK, NS)
    n = np.arange(NC)
    j = np.arange(NS)
    ovl = ((n[None, :] * CMP_STRIDE < (j[:, None] + 1) * SEL_BLOCK)
           & (n[None, :] * CMP_STRIDE + CMP_LEN - 1 >= j[:, None] * SEL_BLOCK) & (n[None, :] < NC - 1))
    ovl = jnp.asarray(ovl, BF16)
    seq = lambda rows, width: pl.BlockSpec((1, 1, rows, width), lambda g, b, t: (b, g, 0, 0))
    R = HG * tq
    W = HG * hd
    return pl.pallas_call(
        functools.partial(_nsa_kernel, tq=tq, sel_k=sel_k),
        grid=(G, B, nq),
        in_specs=[pl.BlockSpec((1, 1, HG, tq, hd), lambda g, b, t: (b, g, 0, t, 0)),
                  seq(NC, hd), seq(hd, NC), seq(T, LANES), seq(LANES, T), seq(T, hd), seq(LANES, T),
                  pl.BlockSpec((1, 1, LANES, tq), lambda g, b, t: (b, g, 0, t)),
                  pl.BlockSpec((HG, len(NEAR_OFFSETS) + 1, tq, tq), lambda g, b, t: (g, 0, 0, 0),
                               pipeline_mode=pl.Buffered(1)),
                  pl.BlockSpec((HG, NC, tq), lambda g, b, t: (g, 0, t)),
                  pl.BlockSpec(ovl.shape, lambda g, b, t: (0, 0))],
        out_specs=pl.BlockSpec((1, tq, W), lambda g, b, t: (b, t, g)),
        out_shape=jax.ShapeDtypeStruct((B, T, G * W), BF16),
        scratch_shapes=[pltpu.VMEM((R, LANES), BF16),
                        pltpu.VMEM((2 * tq, R), F32), pltpu.VMEM((2 * tq, R), F32),
                        pltpu.VMEM((SUBLANES, R), F32), pltpu.VMEM((SUBLANES, R), F32),
                        pltpu.VMEM((SUBLANES, R), F32), pltpu.VMEM((SUBLANES, R), F32),
                        pltpu.VMEM((LANES, R), F32), pltpu.VMEM((LANES, R), F32)],
        compiler_params=pltpu.CompilerParams(vmem_limit_bytes=NSA_VMEM_LIMIT),
        name="nsa_attention",
    )(qn, kc, vct, ks, vst, kw, vwt, gates_t, near, cmpb, ovl)


def _merge_kernel(x_ref, oret_ref, on_ref, ma_ref, mb_ref, wr_ref, wn_ref, wo_ref, o_ref):
    y_ret = _dot(oret_ref[...], wr_ref[...])
    y_nsa = _dot(on_ref[...], wn_ref[...])
    mixed = ma_ref[...].astype(F32) * y_ret + mb_ref[...].astype(F32) * y_nsa
    o_ref[...] = x_ref[...] + _dot(mixed.astype(BF16), wo_ref[...])


def _merge(x2, oret, on, ma, mb, w_o_ret, w_o_nsa, w_out):
    M, D = x2.shape
    tm = min(TM_MERGE, M)
    row = lambda width: pl.BlockSpec((tm, width), lambda i: (i, 0))
    res = lambda a: pl.BlockSpec(a.shape, lambda i: (0, 0), pipeline_mode=pl.Buffered(1))
    w = [a.astype(BF16) for a in (w_o_ret, w_o_nsa, w_out)]
    return pl.pallas_call(
        _merge_kernel,
        grid=(M // tm,),
        in_specs=[row(D), row(oret.shape[1]), row(D), row(D), row(D)] + [res(a) for a in w],
        out_specs=row(D),
        out_shape=jax.ShapeDtypeStruct((M, D), F32),
        compiler_params=pltpu.CompilerParams(vmem_limit_bytes=48 * MIB),
        name="merge_out_projection",
    )(x2, oret, on, ma, mb, *w)


def _rms(x, g):
    return x * lax.rsqrt(jnp.mean(x * x, -1, keepdims=True) + EPS) * g


def _ffn_kernel(x_ref, g_ref, wi_ref, wo_ref, gf_ref, o_ref, h_scr, acc_scr, *, final_norm):
    x = x_ref[...]
    h_scr[...] = _rms(x, g_ref[...]).astype(BF16)
    acc_scr[...] = x
    cw = FFN_HIDDEN // 2
    for c in range(2):
        a = _dot(h_scr[...], wi_ref[:, c * cw:(c + 1) * cw])
        b = _dot(h_scr[...], wi_ref[:, FFN_HIDDEN + c * cw:FFN_HIDDEN + (c + 1) * cw])
        acc_scr[...] += _dot((_silu(a) * b).astype(BF16), wo_ref[c * cw:(c + 1) * cw, :])
    y = acc_scr[...]
    o_ref[...] = _rms(y, gf_ref[...]) if final_norm else y


def _ffn(x2, g, w_in, w_out, g_final, final_norm):
    M, D = x2.shape
    tm = min(TM_MERGE, M)
    row = pl.BlockSpec((tm, D), lambda i: (i, 0))
    vec = pl.BlockSpec((1, D), lambda i: (0, 0))
    res = lambda a: pl.BlockSpec(a.shape, lambda i: (0, 0), pipeline_mode=pl.Buffered(1))
    wi, wo = w_in.astype(BF16), w_out.astype(BF16)
    return pl.pallas_call(
        functools.partial(_ffn_kernel, final_norm=final_norm),
        grid=(M // tm,),
        in_specs=[row, vec, res(wi), res(wo), vec],
        out_specs=row,
        out_shape=jax.ShapeDtypeStruct((M, D), F32),
        scratch_shapes=[pltpu.VMEM((tm, D), BF16), pltpu.VMEM((tm, D), F32)],
        compiler_params=pltpu.CompilerParams(vmem_limit_bytes=52 * MIB),
        name="swiglu_ffn",
    )(x2, g.reshape(1, D), wi, wo, g_final.reshape(1, D))


def kernel(x, norm_mix_g, w_in, cmp_pos_k, cmp_pos_v, cmp_w1_k, cmp_w2_k, cmp_w1_v, cmp_w2_v, w_o_ret, w_o_nsa,
           w_out, norm_ffn_g, w_ffn_in, w_ffn_out, rel_bias, norm_final_g):
    B, T, D = x.shape
    depth = w_in.shape[0]
    G, HG = NSA_KV_GROUPS, NSA_HG
    half = RET_DK // 2
    freqs = ROPE_BASE ** (-jnp.arange(half, dtype=F32) / half)
    ang = jnp.arange(T, dtype=jnp.int32).astype(F32)[:, None] * freqs
    cos, sin = jnp.cos(ang), jnp.sin(ang)
    near, cmpb = _bias_tables(rel_bias, T, TQ)
    for i in range(depth):
        qr, kr, vr, gr, qn, kvc16, ks, vst, kw, vwt, gates_t, ma, mb = _in_projection(
            x, norm_mix_g[i], _pack_w_in(w_in[i]), cos, sin)
        o_ret = _retention(qr, kr, vr, gr)
        kc, vct = _compress(kvc16,cmp_pos_k[i], cmp_pos_v[i], cmp_w1_k[i], cmp_w2_k[i], cmp_w1_v[i], cmp_w2_v[i])
        o_n = _nsa_attention(qn.reshape(B, G, HG, T, NSA_HD), kc, vct, ks, vst, kw, vwt, gates_t, near, cmpb)
        o_n = o_n.reshape(B * T, NSA_HEADS * NSA_HD)
        x2 = _merge(x.reshape(B * T, D), o_ret.reshape(B * T, -1), o_n, ma.reshape(B * T, D), mb.reshape(B * T, D),
                    w_o_ret[i], w_o_nsa[i], w_out[i])
        x2 = _ffn(x2, norm_ffn_g[i], w_ffn_in[i], w_ffn_out[i], norm_final_g, final_norm=(i == depth - 1))
        x = x2.reshape(B, T, D)
    return x
```

```python
import functools
import math

import jax
import jax.numpy as jnp
import numpy as np
from jax import lax
from jax.experimental import pallas as pl
from jax.experimental.pallas import tpu as pltpu

F32 = jnp.float32
BF16 = jnp.bfloat16

D_MODEL = 1024
RET_HEADS = 4
RET_DK = 256
RET_DV = 512
RET_BLOCK = 256
ROPE_BASE = 10000.0
GN_EPS = 1e-5
NSA_HEADS = 16
NSA_KV_GROUPS = 2
NSA_HG = NSA_HEADS // NSA_KV_GROUPS
NSA_HD = 64
CMP_LEN = 32
CMP_STRIDE = 16
CMP_HIDDEN = 256
SEL_BLOCK = 64
SEL_TOPK = 8
WINDOW = 512
FORCED_SCORE = 1e4
REL_BUCKETS = 32
REL_MAX_DIST = 128
FFN_HIDDEN = -(-8 * D_MODEL // (3 * 256)) * 256
EPS = 1e-6
NEG_INF = -1e30
PICKED = -3e38
MASKED_CODE = REL_BUCKETS
NEAR_OFFSETS = (0, 1, 2)
LOG2E = math.log2(math.e)

LANES = 128
SUBLANES = 8
MIB = 1024 * 1024
NSA_VMEM_LIMIT = 56 * MIB

_SEG = np.cumsum([0, RET_HEADS * RET_DK, RET_HEADS * RET_DK, RET_HEADS * RET_DV, RET_HEADS * RET_DV,
                  NSA_HEADS * NSA_HD, 6 * NSA_KV_GROUPS * NSA_HD, 3 * NSA_HEADS, D_MODEL, D_MODEL])
N_QR, N_KR, N_VR, N_GR = RET_HEADS * RET_DK, RET_HEADS * RET_DK, RET_HEADS * RET_DV, RET_HEADS * RET_DV
N_QN, N_KV = NSA_HEADS * NSA_HD, 6 * NSA_KV_GROUPS * NSA_HD
N_GATE = NSA_KV_GROUPS * LANES
_P = np.cumsum([0, N_QR, N_KR, N_VR, N_GR, N_QN, N_KV, N_GATE, D_MODEL, D_MODEL])
P_QR, P_KR, P_VR, P_GR, P_QN, P_KV, P_GATE, P_MA, P_MB, P_END = [int(v) for v in _P]

TM_PROJ = 512
TM_MERGE = 512
TQ = 256
NT = (((1,), (1,)), ((), ()))


def _dot(a, b):
    return jnp.dot(a, b, preferred_element_type=F32)


def _dot_nt(a, b):
    return lax.dot_general(a, b, NT, preferred_element_type=F32)


def _sigmoid(x):
    return 1.0 / (1.0 + jnp.exp(-x))


def _silu(x):
    return x * _sigmoid(x)


def _inproj_kernel(x_ref, g_ref, cos_ref, sin_ref, feat_ref, w_ref,
                   qr_o, kr_o, vr_o, gr_o, qn_o, kvc_o, ks_o, vst_o, kw_o, vwt_o, gate_o, ma_o, mb_o, h_scr, kvc_scr):
    x = x_ref[0]
    h = x * lax.rsqrt(jnp.mean(x * x, -1, keepdims=True) + EPS) * g_ref[...]
    h_scr[...] = h.astype(BF16)
    cos = cos_ref[...]
    sin = sin_ref[...]
    half = RET_DK // 2

    def proj(lo, width):
        return _dot(h_scr[...], w_ref[:, lo:lo + width])

    y32 = proj(P_KV, N_KV)
    y = y32.astype(BF16)
    y_t = y32.T.astype(BF16)
    tm = y.shape[0]
    ones_row = (lax.broadcasted_iota(jnp.int32, (LANES - NSA_HD, tm), 0) == 0).astype(BF16)
    for g in range(NSA_KV_GROUPS):
        lo = lambda j: (j * NSA_KV_GROUPS + g) * NSA_HD
        ks_o[0, g, :, :NSA_HD] = y[:, lo(2):lo(2) + NSA_HD]
        ks_o[0, g, :, NSA_HD:] = feat_ref[:, NSA_HD:]
        kw_o[0, g] = y[:, lo(4):lo(4) + NSA_HD]
        for out, j in ((vst_o, 3), (vwt_o, 5)):
            out[0, g, :NSA_HD, :] = y_t[lo(j):lo(j) + NSA_HD, :]
            out[0, g, NSA_HD:, :] = ones_row
    for s in range(2):
        kvc_scr[s] = y32[:, s * LANES:(s + 1) * LANES]
        for l in range(CMP_STRIDE):
            rows = kvc_scr[s, pl.ds(l, tm // CMP_STRIDE, stride=CMP_STRIDE), :].astype(BF16)
            for g in range(NSA_KV_GROUPS):
                kvc_o[s, 0, g, :, l * NSA_HD:(l + 1) * NSA_HD] = rows[:, g * NSA_HD:(g + 1) * NSA_HD]
    gates_t = _sigmoid(proj(P_GATE, N_GATE)).T
    for g in range(NSA_KV_GROUPS):
        gate_o[0, g] = gates_t[g * LANES:(g + 1) * LANES, :]
    cw = 512
    per = cw // NSA_HD
    for c in range(N_QN // cw):
        y = (proj(P_QN + c * cw, cw) * (NSA_HD ** -0.5 * LOG2E)).astype(BF16)
        for j in range(per):
            qn_o[0, c * per + j] = y[:, j * NSA_HD:(j + 1) * NSA_HD]
    for hd in range(RET_HEADS):
        for out, base, scale in ((qr_o, P_QR, 1.0), (kr_o, P_KR, RET_DK ** -0.5)):
            y = proj(base + hd * RET_DK, RET_DK)
            x1, x2 = y[:, :half], y[:, half:]
            out[0, :, hd * RET_DK:hd * RET_DK + half] = ((x1 * cos - x2 * sin) * scale).astype(BF16)
            out[0, :, hd * RET_DK + half:(hd + 1) * RET_DK] = ((x1 * sin + x2 * cos) * scale).astype(BF16)
    for c in range(N_VR // cw):
        vr_o[0, :, c * cw:(c + 1) * cw] = proj(P_VR + c * cw, cw).astype(BF16)
        gr_o[0, :, c * cw:(c + 1) * cw] = _silu(proj(P_GR + c * cw, cw)).astype(BF16)
    for c in range(D_MODEL // cw):
        ma_o[0, :, c * cw:(c + 1) * cw] = _sigmoid(proj(P_MA + c * cw, cw)).astype(BF16)
        mb_o[0, :, c * cw:(c + 1) * cw] = _sigmoid(proj(P_MB + c * cw, cw)).astype(BF16)


def _pack_w_in(w):
    s = [int(v) for v in _SEG]
    gate = w[:, s[6]:s[7]].reshape(D_MODEL, NSA_KV_GROUPS, 3 * NSA_HG)
    gate = jnp.pad(gate, ((0, 0), (0, 0), (0, LANES - 3 * NSA_HG))).reshape(D_MODEL, N_GATE)
    return jnp.concatenate([w[:, s[0]:s[6]], gate, w[:, s[7]:s[9]]], axis=1).astype(BF16)


def _in_projection(x, g, w_packed, cos, sin):
    B, T, D = x.shape
    G = NSA_KV_GROUPS
    tm = min(TM_PROJ, T)
    nt = T // tm
    assert T // SEL_BLOCK <= LANES - NSA_HD
    own_block = (np.arange(T)[:, None] // SEL_BLOCK) == (np.arange(LANES)[None, :] - NSA_HD)
    feat = jnp.asarray(np.where(own_block, NEG_INF, 0.0), BF16)
    row = lambda width: pl.BlockSpec((1, tm, width), lambda b, t: (b, t, 0))
    grp = lambda width: pl.BlockSpec((1, G, tm, width), lambda b, t: (b, 0, t, 0))
    grp_t = pl.BlockSpec((1, G, LANES, tm), lambda b, t: (b, 0, 0, t))
    grp_shape = lambda width: jax.ShapeDtypeStruct((B, G, T, width), BF16)
    grp_t_shape = lambda dt: jax.ShapeDtypeStruct((B, G, LANES, T), dt)
    out_shape = (
        jax.ShapeDtypeStruct((B, T, N_QR), BF16), jax.ShapeDtypeStruct((B, T, N_KR), BF16),
        jax.ShapeDtypeStruct((B, T, N_VR), BF16), jax.ShapeDtypeStruct((B, T, N_GR), BF16),
        jax.ShapeDtypeStruct((B, NSA_HEADS, T, NSA_HD), BF16),
        jax.ShapeDtypeStruct((2, B, G, T // CMP_STRIDE, CMP_STRIDE * NSA_HD), BF16),
        grp_shape(LANES), grp_t_shape(BF16), grp_shape(NSA_HD), grp_t_shape(BF16),
        grp_t_shape(F32),
        jax.ShapeDtypeStruct((B, T, D_MODEL), BF16), jax.ShapeDtypeStruct((B, T, D_MODEL), BF16))
    out_specs = (
        row(N_QR), row(N_KR), row(N_VR), row(N_GR),
        pl.BlockSpec((1, NSA_HEADS, tm, NSA_HD), lambda b, t: (b, 0, t, 0)),
        pl.BlockSpec((2, 1, G, tm // CMP_STRIDE, CMP_STRIDE * NSA_HD), lambda b, t: (0, b, 0, t, 0)),
        grp(LANES), grp_t, grp(NSA_HD), grp_t,
        grp_t,
        row(D_MODEL), row(D_MODEL))
    out_bytes = tm * (2 * (N_QR + N_KR + N_VR + N_GR + N_QN + 2 * D_MODEL) + 2 * G * (3 * NSA_HD + 3 * LANES)
                      + 4 * N_GATE)
    vmem = 2 * out_bytes + 2 * tm * D * 4 + 2 * D * P_END + tm * D * 2 + 10 * MIB
    return pl.pallas_call(
        _inproj_kernel,
        grid=(B, nt),
        in_specs=[pl.BlockSpec((1, tm, D), lambda b, t: (b, t, 0)),
                  pl.BlockSpec((1, D), lambda b, t: (0, 0)),
                  pl.BlockSpec((tm, RET_DK // 2), lambda b, t: (t, 0)),
                  pl.BlockSpec((tm, RET_DK // 2), lambda b, t: (t, 0)),
                  pl.BlockSpec((tm, LANES), lambda b, t: (t, 0)),
                  pl.BlockSpec((D, P_END), lambda b, t: (0, 0), pipeline_mode=pl.Buffered(1))],
        out_specs=out_specs,
        out_shape=out_shape,
        scratch_shapes=[pltpu.VMEM((tm, D), BF16), pltpu.VMEM((2, tm, G * NSA_HD), F32)],
        compiler_params=pltpu.CompilerParams(vmem_limit_bytes=int(vmem)),
        name="in_projection",
    )(x, g.reshape(1, D), cos, sin, feat, w_packed)


def _retention_kernel(q_ref, k_ref, v_ref, g_ref, dmask_ref, xi_ref, zeta_ref, gch_ref, o_ref):
    C = RET_BLOCK
    nc = q_ref.shape[1] // C
    dmask = dmask_ref[0]
    xi = xi_ref[0]
    zeta = zeta_ref[0]
    gch = gch_ref[0, :, 0:1]
    r = None
    for c in range(nc):
        rows = slice(c * C, (c + 1) * C)
        qc = q_ref[0, rows, :]
        kc = k_ref[0, rows, :]
        vc = v_ref[0, rows, :]
        if c < nc - 1:
            kz = (kc.astype(F32) * zeta).astype(BF16)
            kv = lax.dot_general(kz, vc, (((0,), (0,)), ((), ())), preferred_element_type=F32)
            r_next = kv if c == 0 else gch * r + kv
        s = _dot_nt(qc, kc) * dmask
        o = _dot(s.astype(BF16), vc)
        if c > 0:
            o = o + _dot(qc, r.astype(BF16)) * xi
        if c < nc - 1:
            r = r_next
        mu = jnp.mean(o, -1, keepdims=True)
        d = o - mu
        var = jnp.mean(d * d, -1, keepdims=True)
        on = d * lax.rsqrt(var + GN_EPS)
        o_ref[0, rows, :] = (g_ref[0, rows, :].astype(F32) * on).astype(BF16)


def _retention(qr, kr, vr, gr):
    B, T, _ = qr.shape
    C = RET_BLOCK
    lg = jnp.log(1.0 - 2.0 ** (-5.0 - jnp.arange(RET_HEADS, dtype=F32)))
    n = jnp.arange(C, dtype=F32)
    diff = n[:, None] - n[None, :]
    dmask = jnp.where(diff >= 0, jnp.exp(jnp.maximum(diff, 0.0)[None] * lg[:, None, None]), 0.0)
    xi = jnp.exp((n + 1.0)[None] * lg[:, None])[:, :, None]
    zeta = jnp.exp((C - 1.0 - n)[None] * lg[:, None])[:, :, None]
    gch = jnp.broadcast_to(jnp.exp(C * lg)[:, None, None], (RET_HEADS, 1, LANES))
    qk_spec = pl.BlockSpec((1, T, RET_DK), lambda b, h: (b, 0, h))
    v_spec = pl.BlockSpec((1, T, RET_DV), lambda b, h: (b, 0, h))
    return pl.pallas_call(
        _retention_kernel,
        grid=(B, RET_HEADS),
        in_specs=[qk_spec, qk_spec, v_spec, v_spec,
                  pl.BlockSpec((1, C, C), lambda b, h: (h, 0, 0)),
                  pl.BlockSpec((1, C, 1), lambda b, h: (h, 0, 0)),
                  pl.BlockSpec((1, C, 1), lambda b, h: (h, 0, 0)),
                  pl.BlockSpec((1, 1, LANES), lambda b, h: (h, 0, 0))],
        out_specs=v_spec,
        out_shape=jax.ShapeDtypeStruct((B, T, RET_HEADS * RET_DV), BF16),
        compiler_params=pltpu.CompilerParams(vmem_limit_bytes=32 * MIB),
        name="retention",
    )(qr, kr, vr, gr, dmask, xi, zeta, gch)


def _gelu_tanh(x):
    return 0.5 * x * (1.0 + jnp.tanh(math.sqrt(2.0 / math.pi) * (x + 0.044715 * (x * x * x))))


def _compress_kernel(x_ref, pk_ref, pv_ref, w1k_ref, w2k_ref, w1v_ref, w2vt_ref, ko_ref, vto_ref):
    half = CMP_STRIDE * NSA_HD

    def hidden(s, g, p_ref, w1_ref):
        x = x_ref[s, 0, g].astype(F32)
        a = _dot((x + p_ref[0:1, :]).astype(BF16), w1_ref[0:half, :])
        b = _dot((x + p_ref[1:2, :]).astype(BF16), w1_ref[half:2 * half, :])
        pre = a + pltpu.roll(b, b.shape[0] - 1, 0)
        return _gelu_tanh(pre).astype(BF16)

    for g in range(NSA_KV_GROUPS):
        ko_ref[0, g] = _dot(hidden(0, g, pk_ref, w1k_ref), w2k_ref[...]).astype(BF16)
        vto_ref[0, g] = _dot_nt(w2vt_ref[...], hidden(1, g, pv_ref, w1v_ref)).astype(BF16)


def _compress(kvc16, pos_k, pos_v, w1k, w2k, w1v, w2v):
    _, B, G, NC, F = kvc16.shape
    full = lambda a: pl.BlockSpec(a.shape, lambda b: (0,) * a.ndim)
    pk = pos_k.reshape(2, F)
    pv = pos_v.reshape(2, F)
    w = [a.astype(BF16) for a in (w1k, w2k, w1v, w2v.T)]
    return pl.pallas_call(
        _compress_kernel,
        grid=(B,),
        in_specs=[pl.BlockSpec((2, 1, G, NC, F), lambda b: (0, b, 0, 0, 0)), full(pk), full(pv)]
        + [full(a) for a in w],
        out_specs=(pl.BlockSpec((1, G, NC, NSA_HD), lambda b: (b, 0, 0, 0)),
                   pl.BlockSpec((1, G, NSA_HD, NC), lambda b: (b, 0, 0, 0))),
        out_shape=(jax.ShapeDtypeStruct((B, G, NC, NSA_HD), BF16), jax.ShapeDtypeStruct((B, G, NSA_HD, NC), BF16)),
        compiler_params=pltpu.CompilerParams(vmem_limit_bytes=32 * MIB),
        name="nsa_compress",
    )(kvc16, pk, pv, *w)


def _bias_kernel(tab_ref, nb_ref, cb_ref, near_o, cmp_o):
    h = pl.program_id(0)
    far = tab_ref[REL_BUCKETS - 1, h]

    def build(bk):
        acc = jnp.zeros(bk.shape, F32)
        for k in range(REL_BUCKETS - 1):
            acc = jnp.where(bk == k, (tab_ref[k, h] - far) * LOG2E, acc)
        return jnp.where(bk == MASKED_CODE, NEG_INF, acc)

    near_o[0] = build(nb_ref[...])
    cmp_o[0] = build(cb_ref[...])


def _t5_bucket(dist):
    dist = jnp.maximum(dist, 0)
    max_exact = REL_BUCKETS // 2
    large = max_exact + (jnp.log(jnp.maximum(dist, 1).astype(F32) / max_exact)
                         / math.log(REL_MAX_DIST / max_exact) * (REL_BUCKETS - max_exact)).astype(jnp.int32)
    large = jnp.minimum(large, REL_BUCKETS - 1)
    return jnp.where(dist < max_exact, dist, large)


def _bias_tables(rel_bias, T, tq):
    nc = T // CMP_STRIDE
    i = jnp.arange(tq, dtype=jnp.int32)
    d = jnp.asarray(NEAR_OFFSETS, jnp.int32)[:, None, None] * tq + i[None, None, :] - i[None, :, None]
    near_b = jnp.where((d < 0) | (d >= WINDOW), MASKED_CODE, _t5_bucket(d))
    near_b = jnp.concatenate([near_b, jnp.full((1, tq, tq), MASKED_CODE, jnp.int32)], axis=0)
    cmp_end = jnp.arange(nc, dtype=jnp.int32) * CMP_STRIDE + CMP_LEN - 1
    cmp_b = _t5_bucket(jnp.arange(T, dtype=jnp.int32)[None, :] - cmp_end[:, None])
    return pl.pallas_call(
        _bias_kernel,
        grid=(NSA_HEADS,),
        in_specs=[pl.BlockSpec(memory_space=pltpu.SMEM),
                  pl.BlockSpec(near_b.shape, lambda h: (0, 0, 0)),
                  pl.BlockSpec(cmp_b.shape, lambda h: (0, 0))],
        out_specs=(pl.BlockSpec((1,) + near_b.shape, lambda h: (h, 0, 0, 0)),
                   pl.BlockSpec((1,) + cmp_b.shape, lambda h: (h, 0, 0))),
        out_shape=(jax.ShapeDtypeStruct((NSA_HEADS,) + near_b.shape, F32),
                   jax.ShapeDtypeStruct((NSA_HEADS,) + cmp_b.shape, F32)),
        name="rel_bias_tables",
    )(rel_bias, near_b, cmp_b)


def _nsa_kernel(q_ref, kc_ref, vct_ref, ks_ref, vst_ref, kw_ref, vwt_ref, gt_ref,
                near_ref, cb_ref, ovl_ref, o_ref,
                qa_scr, sa_scr, sb_scr, cma_scr, cmb_scr, ms_scr, mw_scr, accs_scr, accw_scr, *, tq, sel_k):
    HG = NSA_HG
    R = HG * tq
    NC = kc_ref.shape[2]
    NS = ovl_ref.shape[0]
    UW = tq
    ti = pl.program_id(2)
    s0 = ti * tq

    def head(x, h):
        return x[:, h * tq:(h + 1) * tq]

    def keys(ref, u, n=1):
        return ref[0, 0, pl.ds(pl.multiple_of(u * UW, UW), n * UW), :]

    def values_t(ref, u, n=1):
        return ref[0, 0, :, pl.ds(pl.multiple_of(u * UW, UW), n * UW)]

    cmax = lambda s_ref: cma_scr if s_ref is sa_scr else cmb_scr

    def score(s_ref, k_ref, u, n, queries, entries=None):
        def produce(h):
            s = _dot_nt(keys(k_ref, u, n), queries(h))
            if entries is not None:
                s = jnp.concatenate([s[i * UW:(i + 1) * UW] + near_ref[h, e] for i, e in enumerate(entries)], axis=0)
            s_ref[0:n * UW, h * tq:(h + 1) * tq] = s
            groups = [s[r * SUBLANES:(r + 1) * SUBLANES] for r in range(n * UW // SUBLANES)]
            cmax(s_ref)[:, h * tq:(h + 1) * tq] = functools.reduce(jnp.maximum, groups)
        return produce

    def fold(s_ref, n, m_ref, vt_ref, u, acc_ref):
        def consume(h):
            c = slice(h * tq, (h + 1) * tq)
            m_old = m_ref[0:1, c]
            m_new = jnp.maximum(m_old, jnp.max(cmax(s_ref)[:, c], axis=0, keepdims=True))
            part = [_dot(values_t(vt_ref, u + i), jnp.exp2((s_ref[i * UW:(i + 1) * UW, c] - m_new).astype(BF16)))
                    for i in range(n)]
            acc_ref[:, c] = jnp.exp2(m_old - m_new) * acc_ref[:, c] + functools.reduce(jnp.add, part)
            m_ref[:, c] = jnp.broadcast_to(m_new, (SUBLANES, tq))
        return consume

    def stage(produce=None, consume=None):
        for h in range(HG):
            if produce is not None:
                produce(h)
            if consume is not None:
                consume(h)

    q_head = lambda h: q_ref[0, 0, h]
    qa_head = lambda h: qa_scr[h * tq:(h + 1) * tq, :]

    def normalized(acc_ref):
        acc = acc_ref[...]
        return acc[:NSA_HD] / acc[NSA_HD:NSA_HD + 1]

    for ref in (ms_scr, mw_scr):
        ref[...] = jnp.full(ref.shape, NEG_INF, F32)
    for ref in (accs_scr, accw_scr):
        ref[...] = jnp.zeros_like(ref)

    masked = len(NEAR_OFFSETS)
    w0 = jnp.maximum(ti - 2, 0)

    def window_entry(slot):
        back = jnp.minimum(ti, 2) - slot
        return jnp.where(back >= 0, back, masked)

    win_a = (sa_scr, 2, mw_scr, vwt_ref, w0, accw_scr)
    win_b = (sb_scr, 1, mw_scr, vwt_ref, w0 + 2, accw_scr)
    window_a = score(sa_scr, kw_ref, w0, 2, q_head, (window_entry(0), window_entry(1)))
    window_b = score(sb_scr, kw_ref, w0 + 2, 1, q_head, (window_entry(2),))

    pos_c = s0 + lax.broadcasted_iota(jnp.int32, (NC, tq), 1)
    cend = lax.broadcasted_iota(jnp.int32, (NC, tq), 0) * CMP_STRIDE + (CMP_LEN - 1)
    valid_c = cend <= pos_c
    valid_f = valid_c.astype(F32)
    oc = []
    psum = jnp.zeros((NC, tq), F32)
    for h in range(HG):
        window_a(h)
        z = jnp.where(valid_c, _dot_nt(kc_ref[0, 0], q_head(h)) + cb_ref[h], NEG_INF)
        z = z - jnp.max(z, 0, keepdims=True)
        e = jnp.exp2(z) * valid_f
        pc = e / jnp.maximum(jnp.sum(e, 0, keepdims=True), 1e-30)
        oc.append(_dot(vct_ref[0, 0], pc.astype(BF16)))
        psum = psum + pc

    p_hi = psum.astype(BF16)
    r1 = psum - p_hi.astype(F32)
    p_mid = r1.astype(BF16)
    p_lo = (r1 - p_mid.astype(F32)).astype(BF16)
    ovl = ovl_ref[...]
    imp = _dot(ovl, p_hi) + _dot(ovl, p_mid) + _dot(ovl, p_lo)
    jrow = lax.broadcasted_iota(jnp.int32, (NS, tq), 0)
    posl = s0 + lax.broadcasted_iota(jnp.int32, (NS, tq), 1)
    cur = lax.shift_right_logical(posl, int(math.log2(SEL_BLOCK)))
    forced = (jrow == 0) | (jrow == cur) | (jrow == cur - 1)
    started = jrow * SEL_BLOCK <= posl
    imp = jnp.where(forced, FORCED_SCORE, imp)
    imp = jnp.where(started, imp, NEG_INF)
    sel = jnp.zeros((NS, tq), F32)
    consume_a = fold(*win_a)
    for it in range(max(sel_k, HG)):
        if it < sel_k:
            mx = jnp.max(imp, axis=0, keepdims=True)
            first = jnp.min(jnp.where(imp == mx, jrow, NS), axis=0, keepdims=True)
            hit = jrow == first
            sel = jnp.where(hit, 1.0, sel)
            imp = jnp.where(hit, PICKED, imp)
        if it < HG:
            window_b(it)
            consume_a(it)
    notsel = 1.0 - jnp.where(started, sel, 0.0)
    notsel = jnp.concatenate([notsel, jnp.zeros((LANES - NS, tq), F32)], axis=0)
    notsel = pltpu.roll(notsel.T, NSA_HD, 1).astype(BF16)
    for hg in range(HG):
        qa_scr[hg * tq:(hg + 1) * tq, :NSA_HD] = q_ref[0, 0, hg]
        qa_scr[hg * tq:(hg + 1) * tq, NSA_HD:] = notsel[:, NSA_HD:]

    prev_u = jnp.where(ti >= 1, ti - 1, 1)
    sel_job = lambda s_ref, u, n: (s_ref, n, ms_scr, vst_ref, u, accs_scr)
    stage(score(sa_scr, ks_ref, ti, 1, qa_head, (0,)), fold(*win_b))
    stage(score(sb_scr, ks_ref, prev_u, 1, qa_head, (jnp.where(ti >= 1, 1, masked),)), fold(*sel_job(sa_scr, ti, 1)))
    stage(None, fold(*sel_job(sb_scr, prev_u, 1)))

    n_plain = jnp.maximum(ti - 1, 0)
    n_quads = lax.shift_right_logical(n_plain, 2)

    def plain_quad(j, carry):
        stage(score(sa_scr, ks_ref, 4 * j, 2, qa_head))
        stage(score(sb_scr, ks_ref, 4 * j + 2, 2, qa_head), fold(*sel_job(sa_scr, 4 * j, 2)))
        stage(None, fold(*sel_job(sb_scr, 4 * j + 2, 2)))
        return carry

    lax.fori_loop(0, n_quads, plain_quad, 0)
    left = n_plain - 4 * n_quads

    @pl.when(left >= 2)
    def _():
        stage(score(sa_scr, ks_ref, 4 * n_quads, 2, qa_head))
        stage(None, fold(*sel_job(sa_scr, 4 * n_quads, 2)))

    @pl.when(lax.rem(left, 2) == 1)
    def _():
        stage(score(sa_scr, ks_ref, n_plain - 1, 1, qa_head))
        stage(None, fold(*sel_job(sa_scr, n_plain - 1, 1)))

    gt = gt_ref[0, 0]
    o_s = normalized(accs_scr)
    o_w = normalized(accw_scr)

    def gated(h):
        return (gt[3 * h:3 * h + 1] * oc[h] + gt[3 * h + 1:3 * h + 2] * head(o_s, h)
                + gt[3 * h + 2:3 * h + 3] * head(o_w, h))

    pairs = [jnp.concatenate([gated(2 * j), gated(2 * j + 1)], axis=0).T for j in range(HG // 2)]
    o_ref[0] = jnp.concatenate(pairs, axis=1).astype(BF16)


def _nsa_attention(qn, kc, vct, ks, vst, kw, vwt, gates_t, near, cmpb):
    B, G, HG, T, hd = qn.shape
    tq = TQ
    assert T % tq == 0 and T >= 3 * tq and WINDOW == 2 * tq and tq >= 2 * REL_MAX_DIST and tq % LANES == 0
    nq = T // tq
    NC = kc.shape[2]
    NS = T // SEL_BLOCK
    sel_k = min(SEL_TOPK, NS)
    n = np.arange(NC)
    j = np.arange(NS)
    ovl = ((n[None, :] * CMP_STRIDE < (j[:, None] + 1) * SEL_BLOCK)
           & (n[None, :] * CMP_STRIDE + CMP_LEN - 1 >= j[:, None] * SEL_BLOCK) & (n[None, :] < NC - 1))
    ovl = jnp.asarray(ovl, BF16)
    seq = lambda rows, width: pl.BlockSpec((1, 1, rows, width), lambda g, b, t: (b, g, 0, 0))
    R = HG * tq
    W = HG * hd
    return pl.pallas_call(
        functools.partial(_nsa_kernel, tq=tq, sel_k=sel_k),
        grid=(G, B, nq),
        in_specs=[pl.BlockSpec((1, 1, HG, tq, hd), lambda g, b, t: (b, g, 0, t, 0)),
                  seq(NC, hd), seq(hd, NC), seq(T, LANES), seq(LANES, T), seq(T, hd), seq(LANES, T),
                  pl.BlockSpec((1, 1, LANES, tq), lambda g, b, t: (b, g, 0, t)),
                  pl.BlockSpec((HG, len(NEAR_OFFSETS) + 1, tq, tq), lambda g, b, t: (g, 0, 0, 0),
                               pipeline_mode=pl.Buffered(1)),
                  pl.BlockSpec((HG, NC, tq), lambda g, b, t: (g, 0, t)),
                  pl.BlockSpec(ovl.shape, lambda g, b, t: (0, 0))],
        out_specs=pl.BlockSpec((1, tq, W), lambda g, b, t: (b, t, g)),
        out_shape=jax.ShapeDtypeStruct((B, T, G * W), BF16),
        scratch_shapes=[pltpu.VMEM((R, LANES), BF16),
                        pltpu.VMEM((2 * tq, R), F32), pltpu.VMEM((2 * tq, R), F32),
                        pltpu.VMEM((SUBLANES, R), F32), pltpu.VMEM((SUBLANES, R), F32),
                        pltpu.VMEM((SUBLANES, R), F32), pltpu.VMEM((SUBLANES, R), F32),
                        pltpu.VMEM((LANES, R), F32), pltpu.VMEM((LANES, R), F32)],
        compiler_params=pltpu.CompilerParams(vmem_limit_bytes=NSA_VMEM_LIMIT),
        name="nsa_attention",
    )(qn, kc, vct, ks, vst, kw, vwt, gates_t, near, cmpb, ovl)


def _merge_kernel(x_ref, oret_ref, on_ref, ma_ref, mb_ref, wr_ref, wn_ref, wo_ref, o_ref):
    y_ret = _dot(oret_ref[...], wr_ref[...])
    y_nsa = _dot(on_ref[...], wn_ref[...])
    mixed = ma_ref[...].astype(F32) * y_ret + mb_ref[...].astype(F32) * y_nsa
    o_ref[...] = x_ref[...] + _dot(mixed.astype(BF16), wo_ref[...])


def _merge(x2, oret, on, ma, mb, w_o_ret, w_o_nsa, w_out):
    M, D = x2.shape
    tm = min(TM_MERGE, M)
    row = lambda width: pl.BlockSpec((tm, width), lambda i: (i, 0))
    res = lambda a: pl.BlockSpec(a.shape, lambda i: (0, 0), pipeline_mode=pl.Buffered(1))
    w = [a.astype(BF16) for a in (w_o_ret, w_o_nsa, w_out)]
    return pl.pallas_call(
        _merge_kernel,
        grid=(M // tm,),
        in_specs=[row(D), row(oret.shape[1]), row(D), row(D), row(D)] + [res(a) for a in w],
        out_specs=row(D),
        out_shape=jax.ShapeDtypeStruct((M, D), F32),
        compiler_params=pltpu.CompilerParams(vmem_limit_bytes=48 * MIB),
        name="merge_out_projection",
    )(x2, oret, on, ma, mb, *w)


def _rms(x, g):
    return x * lax.rsqrt(jnp.mean(x * x, -1, keepdims=True) + EPS) * g


def _ffn_kernel(x_ref, g_ref, wi_ref, wo_ref, gf_ref, o_ref, h_scr, acc_scr, *, final_norm):
    x = x_ref[...]
    h_scr[...] = _rms(x, g_ref[...]).astype(BF16)
    acc_scr[...] = x
    cw = FFN_HIDDEN // 2
    for c in range(2):
        a = _dot(h_scr[...], wi_ref[:, c * cw:(c + 1) * cw])
        b = _dot(h_scr[...], wi_ref[:, FFN_HIDDEN + c * cw:FFN_HIDDEN + (c + 1) * cw])
        acc_scr[...] += _dot((_silu(a) * b).astype(BF16), wo_ref[c * cw:(c + 1) * cw, :])
    y = acc_scr[...]
    o_ref[...] = _rms(y, gf_ref[...]) if final_norm else y


def _ffn(x2, g, w_in, w_out, g_final, final_norm):
    M, D = x2.shape
    tm = min(TM_MERGE, M)
    row = pl.BlockSpec((tm, D), lambda i: (i, 0))
    vec = pl.BlockSpec((1, D), lambda i: (0, 0))
    res = lambda a: pl.BlockSpec(a.shape, lambda i: (0, 0), pipeline_mode=pl.Buffered(1))
    wi, wo = w_in.astype(BF16), w_out.astype(BF16)
    return pl.pallas_call(
        functools.partial(_ffn_kernel, final_norm=final_norm),
        grid=(M // tm,),
        in_specs=[row, vec, res(wi), res(wo), vec],
        out_specs=row,
        out_shape=jax.ShapeDtypeStruct((M, D), F32),
        scratch_shapes=[pltpu.VMEM((tm, D), BF16), pltpu.VMEM((tm, D), F32)],
        compiler_params=pltpu.CompilerParams(vmem_limit_bytes=52 * MIB),
        name="swiglu_ffn",
    )(x2, g.reshape(1, D), wi, wo, g_final.reshape(1, D))


def kernel(x, norm_mix_g, w_in, cmp_pos_k, cmp_pos_v, cmp_w1_k, cmp_w2_k, cmp_w1_v, cmp_w2_v, w_o_ret, w_o_nsa,
           w_out, norm_ffn_g, w_ffn_in, w_ffn_out, rel_bias, norm_final_g):
    B, T, D = x.shape
    depth = w_in.shape[0]
    G, HG = NSA_KV_GROUPS, NSA_HG
    half = RET_DK // 2
    freqs = ROPE_BASE ** (-jnp.arange(half, dtype=F32) / half)
    ang = jnp.arange(T, dtype=jnp.int32).astype(F32)[:, None] * freqs
    cos, sin = jnp.cos(ang), jnp.sin(ang)
    near, cmpb = _bias_tables(rel_bias, T, TQ)
    for i in range(depth):
        qr, kr, vr, gr, qn, kvc16, ks, vst, kw, vwt, gates_t, ma, mb = _in_projection(
            x, norm_mix_g[i], _pack_w_in(w_in[i]), cos, sin)
        o_ret = _retention(qr, kr, vr, gr)
        kc, vct = _compress(kvc16,cmp_pos_k[i], cmp_pos_v[i], cmp_w1_k[i], cmp_w2_k[i], cmp_w1_v[i], cmp_w2_v[i])
        o_n = _nsa_attention(qn.reshape(B, G, HG, T, NSA_HD), kc, vct, ks, vst, kw, vwt, gates_t, near, cmpb)
        o_n = o_n.reshape(B * T, NSA_HEADS * NSA_HD)
        x2 = _merge(x.reshape(B * T, D), o_ret.reshape(B * T, -1), o_n, ma.reshape(B * T, D), mb.reshape(B * T, D),
                    w_o_ret[i], w_o_nsa[i], w_out[i])
        x2 = _ffn(x2, norm_ffn_g[i], w_ffn_in[i], w_ffn_out[i], norm_final_g, final_norm=(i == depth - 1))
        x = x2.reshape(B, T, D)
    return x
```

```python
import functools
import math

import jax
import jax.numpy as jnp
import numpy as np
from jax import lax
from jax.experimental import pallas as pl
from jax.experimental.pallas import tpu as pltpu

F32 = jnp.float32
BF16 = jnp.bfloat16

D_MODEL = 1024
RET_HEADS = 4
RET_DK = 256
RET_DV = 512
RET_BLOCK = 256
ROPE_BASE = 10000.0
GN_EPS = 1e-5
NSA_HEADS = 16
NSA_KV_GROUPS = 2
NSA_HG = NSA_HEADS // NSA_KV_GROUPS
NSA_HD = 64
CMP_LEN = 32
CMP_STRIDE = 16
CMP_HIDDEN = 256
SEL_BLOCK = 64
SEL_TOPK = 8
WINDOW = 512
FORCED_SCORE = 1e4
REL_BUCKETS = 32
REL_MAX_DIST = 128
FFN_HIDDEN = -(-8 * D_MODEL // (3 * 256)) * 256
EPS = 1e-6
NEG_INF = -1e30
PICKED = -3e38
MASKED_CODE = REL_BUCKETS
NEAR_OFFSETS = (0, 1, 2)
LOG2E = math.log2(math.e)

LANES = 128
SUBLANES = 8
MIB = 1024 * 1024
NSA_VMEM_LIMIT = 56 * MIB

_SEG = np.cumsum([0, RET_HEADS * RET_DK, RET_HEADS * RET_DK, RET_HEADS * RET_DV, RET_HEADS * RET_DV,
                  NSA_HEADS * NSA_HD, 6 * NSA_KV_GROUPS * NSA_HD, 3 * NSA_HEADS, D_MODEL, D_MODEL])
N_QR, N_KR, N_VR, N_GR = RET_HEADS * RET_DK, RET_HEADS * RET_DK, RET_HEADS * RET_DV, RET_HEADS * RET_DV
N_QN, N_KV = NSA_HEADS * NSA_HD, 6 * NSA_KV_GROUPS * NSA_HD
N_GATE = NSA_KV_GROUPS * LANES
_P = np.cumsum([0, N_QR, N_KR, N_VR, N_GR, N_QN, N_KV, N_GATE, D_MODEL, D_MODEL])
P_QR, P_KR, P_VR, P_GR, P_QN, P_KV, P_GATE, P_MA, P_MB, P_END = [int(v) for v in _P]

TM_PROJ = 512
TM_MERGE = 512
TQ = 256
NT = (((1,), (1,)), ((), ()))


def _dot(a, b):
    return jnp.dot(a, b, preferred_element_type=F32)


def _dot_nt(a, b):
    return lax.dot_general(a, b, NT, preferred_element_type=F32)


def _sigmoid(x):
    return 1.0 / (1.0 + jnp.exp(-x))


def _silu(x):
    return x * _sigmoid(x)


def _inproj_kernel(x_ref, g_ref, cos_ref, sin_ref, feat_ref, w_ref,
                   qr_o, kr_o, vr_o, gr_o, qn_o, kvc_o, ks_o, vst_o, kw_o, vwt_o, gate_o, ma_o, mb_o, h_scr, kvc_scr):
    x = x_ref[0]
    h = x * lax.rsqrt(jnp.mean(x * x, -1, keepdims=True) + EPS) * g_ref[...]
    h_scr[...] = h.astype(BF16)
    cos = cos_ref[...]
    sin = sin_ref[...]
    half = RET_DK // 2

    def proj(lo, width):
        return _dot(h_scr[...], w_ref[:, lo:lo + width])

    y32 = proj(P_KV, N_KV)
    y = y32.astype(BF16)
    y_t = y32.T.astype(BF16)
    tm = y.shape[0]
    ones_row = (lax.broadcasted_iota(jnp.int32, (LANES - NSA_HD, tm), 0) == 0).astype(BF16)
    for g in range(NSA_KV_GROUPS):
        lo = lambda j: (j * NSA_KV_GROUPS + g) * NSA_HD
        ks_o[0, g, :, :NSA_HD] = y[:, lo(2):lo(2) + NSA_HD]
        ks_o[0, g, :, NSA_HD:] = feat_ref[:, NSA_HD:]
        kw_o[0, g] = y[:, lo(4):lo(4) + NSA_HD]
        for out, j in ((vst_o, 3), (vwt_o, 5)):
            out[0, g, :NSA_HD, :] = y_t[lo(j):lo(j) + NSA_HD, :]
            out[0, g, NSA_HD:, :] = ones_row
    for s in range(2):
        kvc_scr[s] = y32[:, s * LANES:(s + 1) * LANES]
        for l in range(CMP_STRIDE):
            rows = kvc_scr[s, pl.ds(l, tm // CMP_STRIDE, stride=CMP_STRIDE), :].astype(BF16)
            for g in range(NSA_KV_GROUPS):
                kvc_o[s, 0, g, :, l * NSA_HD:(l + 1) * NSA_HD] = rows[:, g * NSA_HD:(g + 1) * NSA_HD]
    gates_t = _sigmoid(proj(P_GATE, N_GATE)).T
    for g in range(NSA_KV_GROUPS):
        gate_o[0, g] = gates_t[g * LANES:(g + 1) * LANES, :]
    cw = 512
    per = cw // NSA_HD
    for c in range(N_QN // cw):
        y = (proj(P_QN + c * cw, cw) * (NSA_HD ** -0.5 * LOG2E)).astype(BF16)
        for j in range(per):
            qn_o[0, c * per + j] = y[:, j * NSA_HD:(j + 1) * NSA_HD]
    for hd in range(RET_HEADS):
        for out, base, scale in ((qr_o, P_QR, 1.0), (kr_o, P_KR, RET_DK ** -0.5)):
            y = proj(base + hd * RET_DK, RET_DK)
            x1, x2 = y[:, :half], y[:, half:]
            out[0, :, hd * RET_DK:hd * RET_DK + half] = ((x1 * cos - x2 * sin) * scale).astype(BF16)
            out[0, :, hd * RET_DK + half:(hd + 1) * RET_DK] = ((x1 * sin + x2 * cos) * scale).astype(BF16)
    for c in range(N_VR // cw):
        vr_o[0, :, c * cw:(c + 1) * cw] = proj(P_VR + c * cw, cw).astype(BF16)
        gr_o[0, :, c * cw:(c + 1) * cw] = _silu(proj(P_GR + c * cw, cw)).astype(BF16)
    for c in range(D_MODEL // cw):
        ma_o[0, :, c * cw:(c + 1) * cw] = _sigmoid(proj(P_MA + c * cw, cw)).astype(BF16)
        mb_o[0, :, c * cw:(c + 1) * cw] = _sigmoid(proj(P_MB + c * cw, cw)).astype(BF16)


def _pack_w_in(w):
    s = [int(v) for v in _SEG]
    gate = w[:, s[6]:s[7]].reshape(D_MODEL, NSA_KV_GROUPS, 3 * NSA_HG)
    gate = jnp.pad(gate, ((0, 0), (0, 0), (0, LANES - 3 * NSA_HG))).reshape(D_MODEL, N_GATE)
    return jnp.concatenate([w[:, s[0]:s[6]], gate, w[:, s[7]:s[9]]], axis=1).astype(BF16)


def _in_projection(x, g, w_packed, cos, sin):
    B, T, D = x.shape
    G = NSA_KV_GROUPS
    tm = min(TM_PROJ, T)
    nt = T // tm
    assert T // SEL_BLOCK <= LANES - NSA_HD
    own_block = (np.arange(T)[:, None] // SEL_BLOCK) == (np.arange(LANES)[None, :] - NSA_HD)
    feat = jnp.asarray(np.where(own_block, NEG_INF, 0.0), BF16)
    row = lambda width: pl.BlockSpec((1, tm, width), lambda b, t: (b, t, 0))
    grp = lambda width: pl.BlockSpec((1, G, tm, width), lambda b, t: (b, 0, t, 0))
    grp_t = pl.BlockSpec((1, G, LANES, tm), lambda b, t: (b, 0, 0, t))
    grp_shape = lambda width: jax.ShapeDtypeStruct((B, G, T, width), BF16)
    grp_t_shape = lambda dt: jax.ShapeDtypeStruct((B, G, LANES, T), dt)
    out_shape = (
        jax.ShapeDtypeStruct((B, T, N_QR), BF16), jax.ShapeDtypeStruct((B, T, N_KR), BF16),
        jax.ShapeDtypeStruct((B, T, N_VR), BF16), jax.ShapeDtypeStruct((B, T, N_GR), BF16),
        jax.ShapeDtypeStruct((B, NSA_HEADS, T, NSA_HD), BF16),
        jax.ShapeDtypeStruct((2, B, G, T // CMP_STRIDE, CMP_STRIDE * NSA_HD), BF16),
        grp_shape(LANES), grp_t_shape(BF16), grp_shape(NSA_HD), grp_t_shape(BF16),
        grp_t_shape(F32),
        jax.ShapeDtypeStruct((B, T, D_MODEL), BF16), jax.ShapeDtypeStruct((B, T, D_MODEL), BF16))
    out_specs = (
        row(N_QR), row(N_KR), row(N_VR), row(N_GR),
        pl.BlockSpec((1, NSA_HEADS, tm, NSA_HD), lambda b, t: (b, 0, t, 0)),
        pl.BlockSpec((2, 1, G, tm // CMP_STRIDE, CMP_STRIDE * NSA_HD), lambda b, t: (0, b, 0, t, 0)),
        grp(LANES), grp_t, grp(NSA_HD), grp_t,
        grp_t,
        row(D_MODEL), row(D_MODEL))
    out_bytes = tm * (2 * (N_QR + N_KR + N_VR + N_GR + N_QN + 2 * D_MODEL) + 2 * G * (3 * NSA_HD + 3 * LANES)
                      + 4 * N_GATE)
    vmem = 2 * out_bytes + 2 * tm * D * 4 + 2 * D * P_END + tm * D * 2 + 10 * MIB
    return pl.pallas_call(
        _inproj_kernel,
        grid=(B, nt),
        in_specs=[pl.BlockSpec((1, tm, D), lambda b, t: (b, t, 0)),
                  pl.BlockSpec((1, D), lambda b, t: (0, 0)),
                  pl.BlockSpec((tm, RET_DK // 2), lambda b, t: (t, 0)),
                  pl.BlockSpec((tm, RET_DK // 2), lambda b, t: (t, 0)),
                  pl.BlockSpec((tm, LANES), lambda b, t: (t, 0)),
                  pl.BlockSpec((D, P_END), lambda b, t: (0, 0), pipeline_mode=pl.Buffered(1))],
        out_specs=out_specs,
        out_shape=out_shape,
        scratch_shapes=[pltpu.VMEM((tm, D), BF16), pltpu.VMEM((2, tm, G * NSA_HD), F32)],
        compiler_params=pltpu.CompilerParams(vmem_limit_bytes=int(vmem)),
        name="in_projection",
    )(x, g.reshape(1, D), cos, sin, feat, w_packed)


def _retention_kernel(q_ref, k_ref, v_ref, g_ref, dmask_ref, xi_ref, zeta_ref, gch_ref, o_ref):
    C = RET_BLOCK
    nc = q_ref.shape[1] // C
    dmask = dmask_ref[0]
    xi = xi_ref[0]
    zeta = zeta_ref[0]
    gch = gch_ref[0, :, 0:1]
    r = None
    for c in range(nc):
        rows = slice(c * C, (c + 1) * C)
        qc = q_ref[0, rows, :]
        kc = k_ref[0, rows, :]
        vc = v_ref[0, rows, :]
        if c < nc - 1:
            kz = (kc.astype(F32) * zeta).astype(BF16)
            kv = lax.dot_general(kz, vc, (((0,), (0,)), ((), ())), preferred_element_type=F32)
            r_next = kv if c == 0 else gch * r + kv
        s = _dot_nt(qc, kc) * dmask
        o = _dot(s.astype(BF16), vc)
        if c > 0:
            o = o + _dot(qc, r.astype(BF16)) * xi
        if c < nc - 1:
            r = r_next
        mu = jnp.mean(o, -1, keepdims=True)
        d = o - mu
        var = jnp.mean(d * d, -1, keepdims=True)
        on = d * lax.rsqrt(var + GN_EPS)
        o_ref[0, rows, :] = (g_ref[0, rows, :].astype(F32) * on).astype(BF16)


def _retention(qr, kr, vr, gr):
    B, T, _ = qr.shape
    C = RET_BLOCK
    lg = jnp.log(1.0 - 2.0 ** (-5.0 - jnp.arange(RET_HEADS, dtype=F32)))
    n = jnp.arange(C, dtype=F32)
    diff = n[:, None] - n[None, :]
    dmask = jnp.where(diff >= 0, jnp.exp(jnp.maximum(diff, 0.0)[None] * lg[:, None, None]), 0.0)
    xi = jnp.exp((n + 1.0)[None] * lg[:, None])[:, :, None]
    zeta = jnp.exp((C - 1.0 - n)[None] * lg[:, None])[:, :, None]
    gch = jnp.broadcast_to(jnp.exp(C * lg)[:, None, None], (RET_HEADS, 1, LANES))
    qk_spec = pl.BlockSpec((1, T, RET_DK), lambda b, h: (b, 0, h))
    v_spec = pl.BlockSpec((1, T, RET_DV), lambda b, h: (b, 0, h))
    return pl.pallas_call(
        _retention_kernel,
        grid=(B, RET_HEADS),
        in_specs=[qk_spec, qk_spec, v_spec, v_spec,
                  pl.BlockSpec((1, C, C), lambda b, h: (h, 0, 0)),
                  pl.BlockSpec((1, C, 1), lambda b, h: (h, 0, 0)),
                  pl.BlockSpec((1, C, 1), lambda b, h: (h, 0, 0)),
                  pl.BlockSpec((1, 1, LANES), lambda b, h: (h, 0, 0))],
        out_specs=v_spec,
        out_shape=jax.ShapeDtypeStruct((B, T, RET_HEADS * RET_DV), BF16),
        compiler_params=pltpu.CompilerParams(vmem_limit_bytes=32 * MIB),
        name="retention",
    )(qr, kr, vr, gr, dmask, xi, zeta, gch)


def _gelu_tanh(x):
    return 0.5 * x * (1.0 + jnp.tanh(math.sqrt(2.0 / math.pi) * (x + 0.044715 * (x * x * x))))


def _compress_kernel(x_ref, pk_ref, pv_ref, w1k_ref, w2k_ref, w1v_ref, w2vt_ref, ko_ref, vto_ref):
    half = CMP_STRIDE * NSA_HD

    def hidden(s, g, p_ref, w1_ref):
        x = x_ref[s, 0, g].astype(F32)
        a = _dot((x + p_ref[0:1, :]).astype(BF16), w1_ref[0:half, :])
        b = _dot((x + p_ref[1:2, :]).astype(BF16), w1_ref[half:2 * half, :])
        pre = a + pltpu.roll(b, b.shape[0] - 1, 0)
        return _gelu_tanh(pre).astype(BF16)

    for g in range(NSA_KV_GROUPS):
        ko_ref[0, g] = _dot(hidden(0, g, pk_ref, w1k_ref), w2k_ref[...]).astype(BF16)
        vto_ref[0, g] = _dot_nt(w2vt_ref[...], hidden(1, g, pv_ref, w1v_ref)).astype(BF16)


def _compress(kvc16, pos_k, pos_v, w1k, w2k, w1v, w2v):
    _, B, G, NC, F = kvc16.shape
    full = lambda a: pl.BlockSpec(a.shape, lambda b: (0,) * a.ndim)
    pk = pos_k.reshape(2, F)
    pv = pos_v.reshape(2, F)
    w = [a.astype(BF16) for a in (w1k, w2k, w1v, w2v.T)]
    return pl.pallas_call(
        _compress_kernel,
        grid=(B,),
        in_specs=[pl.BlockSpec((2, 1, G, NC, F), lambda b: (0, b, 0, 0, 0)), full(pk), full(pv)]
        + [full(a) for a in w],
        out_specs=(pl.BlockSpec((1, G, NC, NSA_HD), lambda b: (b, 0, 0, 0)),
                   pl.BlockSpec((1, G, NSA_HD, NC), lambda b: (b, 0, 0, 0))),
        out_shape=(jax.ShapeDtypeStruct((B, G, NC, NSA_HD), BF16), jax.ShapeDtypeStruct((B, G, NSA_HD, NC), BF16)),
        compiler_params=pltpu.CompilerParams(vmem_limit_bytes=32 * MIB),
        name="nsa_compress",
    )(kvc16, pk, pv, *w)


def _bias_kernel(tab_ref, nb_ref, cb_ref, near_o, cmp_o):
    h = pl.program_id(0)
    far = tab_ref[REL_BUCKETS - 1, h]

    def build(bk):
        acc = jnp.zeros(bk.shape, F32)
        for k in range(REL_BUCKETS - 1):
            acc = jnp.where(bk == k, (tab_ref[k, h] - far) * LOG2E, acc)
        return jnp.where(bk == MASKED_CODE, NEG_INF, acc)

    near_o[0] = build(nb_ref[...])
    cmp_o[0] = build(cb_ref[...])


def _t5_bucket(dist):
    dist = jnp.maximum(dist, 0)
    max_exact = REL_BUCKETS // 2
    large = max_exact + (jnp.log(jnp.maximum(dist, 1).astype(F32) / max_exact)
                         / math.log(REL_MAX_DIST / max_exact) * (REL_BUCKETS - max_exact)).astype(jnp.int32)
    large = jnp.minimum(large, REL_BUCKETS - 1)
    return jnp.where(dist < max_exact, dist, large)


def _bias_tables(rel_bias, T, tq):
    nc = T // CMP_STRIDE
    i = jnp.arange(tq, dtype=jnp.int32)
    d = jnp.asarray(NEAR_OFFSETS, jnp.int32)[:, None, None] * tq + i[None, None, :] - i[None, :, None]
    near_b = jnp.where((d < 0) | (d >= WINDOW), MASKED_CODE, _t5_bucket(d))
    near_b = jnp.concatenate([near_b, jnp.full((1, tq, tq), MASKED_CODE, jnp.int32)], axis=0)
    cmp_end = jnp.arange(nc, dtype=jnp.int32) * CMP_STRIDE + CMP_LEN - 1
    cmp_b = _t5_bucket(jnp.arange(T, dtype=jnp.int32)[None, :] - cmp_end[:, None])
    return pl.pallas_call(
        _bias_kernel,
        grid=(NSA_HEADS,),
        in_specs=[pl.BlockSpec(memory_space=pltpu.SMEM),
                  pl.BlockSpec(near_b.shape, lambda h: (0, 0, 0)),
                  pl.BlockSpec(cmp_b.shape, lambda h: (0, 0))],
        out_specs=(pl.BlockSpec((1,) + near_b.shape, lambda h: (h, 0, 0, 0)),
                   pl.BlockSpec((1,) + cmp_b.shape, lambda h: (h, 0, 0))),
        out_shape=(jax.ShapeDtypeStruct((NSA_HEADS,) + near_b.shape, F32),
                   jax.ShapeDtypeStruct((NSA_HEADS,) + cmp_b.shape, F32)),
        name="rel_bias_tables",
    )(rel_bias, near_b, cmp_b)


def _nsa_kernel(q_ref, kc_ref, vct_ref, ks_ref, vst_ref, kw_ref, vwt_ref, gt_ref,
                near_ref, cb_ref, ovl_ref, o_ref,
                qa_scr, sa_scr, sb_scr, cma_scr, cmb_scr, ms_scr, mw_scr, accs_scr, accw_scr, *, tq, sel_k):
    HG = NSA_HG
    R = HG * tq
    NC = kc_ref.shape[2]
    NS = ovl_ref.shape[0]
    UW = tq
    ti = pl.program_id(2)
    s0 = ti * tq

    def head(x, h):
        return x[:, h * tq:(h + 1) * tq]

    def keys(ref, u, n=1):
        return ref[0, 0, pl.ds(pl.multiple_of(u * UW, UW), n * UW), :]

    def values_t(ref, u, n=1):
        return ref[0, 0, :, pl.ds(pl.multiple_of(u * UW, UW), n * UW)]

    cmax = lambda s_ref: cma_scr if s_ref is sa_scr else cmb_scr

    def score(s_ref, k_ref, u, n, queries, entries=None):
        def produce(h):
            s = _dot_nt(keys(k_ref, u, n), queries(h))
            if entries is not None:
                s = jnp.concatenate([s[i * UW:(i + 1) * UW] + near_ref[h, e] for i, e in enumerate(entries)], axis=0)
            s_ref[0:n * UW, h * tq:(h + 1) * tq] = s
            groups = [s[r * SUBLANES:(r + 1) * SUBLANES] for r in range(n * UW // SUBLANES)]
            cmax(s_ref)[:, h * tq:(h + 1) * tq] = functools.reduce(jnp.maximum, groups)
        return produce

    def fold(s_ref, n, m_ref, vt_ref, u, acc_ref):
        def consume(h):
            c = slice(h * tq, (h + 1) * tq)
            m_old = m_ref[0:1, c]
            m_new = jnp.maximum(m_old, jnp.max(cmax(s_ref)[:, c], axis=0, keepdims=True))
            part = [_dot(values_t(vt_ref, u + i), jnp.exp2((s_ref[i * UW:(i + 1) * UW, c] - m_new).astype(BF16)))
                    for i in range(n)]
            acc_ref[:, c] = jnp.exp2(m_old - m_new) * acc_ref[:, c] + functools.reduce(jnp.add, part)
            m_ref[:, c] = jnp.broadcast_to(m_new, (SUBLANES, tq))
        return consume

    def stage(produce=None, consume=None):
        for h in range(HG):
            if produce is not None:
                produce(h)
            if consume is not None:
                consume(h)

    q_head = lambda h: q_ref[0, 0, h]
    qa_head = lambda h: qa_scr[h * tq:(h + 1) * tq, :]

    def normalized(acc_ref):
        acc = acc_ref[...]
        return acc[:NSA_HD] / acc[NSA_HD:NSA_HD + 1]

    for ref in (ms_scr, mw_scr):
        ref[...] = jnp.full(ref.shape, NEG_INF, F32)
    for ref in (accs_scr, accw_scr):
        ref[...] = jnp.zeros_like(ref)

    masked = len(NEAR_OFFSETS)
    w0 = jnp.maximum(ti - 2, 0)

    def window_entry(slot):
        back = jnp.minimum(ti, 2) - slot
        return jnp.where(back >= 0, back, masked)

    win_a = (sa_scr, 2, mw_scr, vwt_ref, w0, accw_scr)
    win_b = (sb_scr, 1, mw_scr, vwt_ref, w0 + 2, accw_scr)
    window_a = score(sa_scr, kw_ref, w0, 2, q_head, (window_entry(0), window_entry(1)))
    window_b = score(sb_scr, kw_ref, w0 + 2, 1, q_head, (window_entry(2),))

    stage(window_a)

    pos_c = s0 + lax.broadcasted_iota(jnp.int32, (NC, tq), 1)
    cend = lax.broadcasted_iota(jnp.int32, (NC, tq), 0) * CMP_STRIDE + (CMP_LEN - 1)
    valid_c = cend <= pos_c
    valid_f = valid_c.astype(F32)
    lc = [_dot_nt(kc_ref[0, 0], q_head(h)) + cb_ref[h] for h in range(HG)]
    pc = []
    for h in range(HG):
        z = jnp.where(valid_c, lc[h], NEG_INF)
        e = jnp.exp2(z - jnp.max(z, 0, keepdims=True)) * valid_f
        pc.append(e / jnp.maximum(jnp.sum(e, 0, keepdims=True), 1e-30))
    oc = [_dot(vct_ref[0, 0], pc[h].astype(BF16)) for h in range(HG)]
    psum = functools.reduce(jnp.add, pc)

    p_hi = psum.astype(BF16)
    r1 = psum - p_hi.astype(F32)
    p_mid = r1.astype(BF16)
    p_lo = (r1 - p_mid.astype(F32)).astype(BF16)
    ovl = ovl_ref[...]
    imp = _dot(ovl, p_hi) + _dot(ovl, p_mid) + _dot(ovl, p_lo)
    jrow = lax.broadcasted_iota(jnp.int32, (NS, tq), 0)
    posl = s0 + lax.broadcasted_iota(jnp.int32, (NS, tq), 1)
    cur = lax.shift_right_logical(posl, int(math.log2(SEL_BLOCK)))
    forced = (jrow == 0) | (jrow == cur) | (jrow == cur - 1)
    started = jrow * SEL_BLOCK <= posl
    imp = jnp.where(forced, FORCED_SCORE, imp)
    imp = jnp.where(started, imp, NEG_INF)
    sel = jnp.zeros((NS, tq), F32)
    consume_a = fold(*win_a)
    for it in range(max(sel_k, HG)):
        if it < sel_k:
            mx = jnp.max(imp, axis=0, keepdims=True)
            first = jnp.min(jnp.where(imp == mx, jrow, NS), axis=0, keepdims=True)
            hit = jrow == first
            sel = jnp.where(hit, 1.0, sel)
            imp = jnp.where(hit, PICKED, imp)
        if it < HG:
            window_b(it)
            consume_a(it)
    notsel = 1.0 - jnp.where(started, sel, 0.0)
    notsel = jnp.concatenate([notsel, jnp.zeros((LANES - NS, tq), F32)], axis=0)
    notsel = pltpu.roll(notsel.T, NSA_HD, 1).astype(BF16)
    for hg in range(HG):
        qa_scr[hg * tq:(hg + 1) * tq, :NSA_HD] = q_ref[0, 0, hg]
        qa_scr[hg * tq:(hg + 1) * tq, NSA_HD:] = notsel[:, NSA_HD:]

    prev_u = jnp.where(ti >= 1, ti - 1, 1)
    sel_job = lambda s_ref, u, n: (s_ref, n, ms_scr, vst_ref, u, accs_scr)
    stage(score(sa_scr, ks_ref, ti, 1, qa_head, (0,)), fold(*win_b))
    stage(score(sb_scr, ks_ref, prev_u, 1, qa_head, (jnp.where(ti >= 1, 1, masked),)), fold(*sel_job(sa_scr, ti, 1)))
    stage(None, fold(*sel_job(sb_scr, prev_u, 1)))

    n_plain = jnp.maximum(ti - 1, 0)
    n_quads = lax.shift_right_logical(n_plain, 2)

    def plain_quad(j, carry):
        stage(score(sa_scr, ks_ref, 4 * j, 2, qa_head))
        stage(score(sb_scr, ks_ref, 4 * j + 2, 2, qa_head), fold(*sel_job(sa_scr, 4 * j, 2)))
        stage(None, fold(*sel_job(sb_scr, 4 * j + 2, 2)))
        return carry

    lax.fori_loop(0, n_quads, plain_quad, 0)
    left = n_plain - 4 * n_quads

    @pl.when(left >= 2)
    def _():
        stage(score(sa_scr, ks_ref, 4 * n_quads, 2, qa_head))
        stage(None, fold(*sel_job(sa_scr, 4 * n_quads, 2)))

    @pl.when(lax.rem(left, 2) == 1)
    def _():
        stage(score(sa_scr, ks_ref, n_plain - 1, 1, qa_head))
        stage(None, fold(*sel_job(sa_scr, n_plain - 1, 1)))

    gt = gt_ref[0, 0]
    o_s = normalized(accs_scr)
    o_w = normalized(accw_scr)

    def gated(h):
        return (gt[3 * h:3 * h + 1] * oc[h] + gt[3 * h + 1:3 * h + 2] * head(o_s, h)
                + gt[3 * h + 2:3 * h + 3] * head(o_w, h))

    pairs = [jnp.concatenate([gated(2 * j), gated(2 * j + 1)], axis=0).T for j in range(HG // 2)]
    o_ref[0] = jnp.concatenate(pairs, axis=1).astype(BF16)


def _nsa_attention(qn, kc, vct, ks, vst, kw, vwt, gates_t, near, cmpb):
    B, G, HG, T, hd = qn.shape
    tq = TQ
    assert T % tq == 0 and T >= 3 * tq and WINDOW == 2 * tq and tq >= 2 * REL_MAX_DIST and tq % LANES == 0
    nq = T // tq
    NC = kc.shape[2]
    NS = T // SEL_BLOCK
    sel_k = min(SEL_TOPK, NS)
    n = np.arange(NC)
    j = np.arange(NS)
    ovl = ((n[None, :] * CMP_STRIDE < (j[:, None] + 1) * SEL_BLOCK)
           & (n[None, :] * CMP_STRIDE + CMP_LEN - 1 >= j[:, None] * SEL_BLOCK) & (n[None, :] < NC - 1))
    ovl = jnp.asarray(ovl, BF16)
    seq = lambda rows, width: pl.BlockSpec((1, 1, rows, width), lambda g, b, t: (b, g, 0, 0))
    R = HG * tq
    W = HG * hd
    return pl.pallas_call(
        functools.partial(_nsa_kernel, tq=tq, sel_k=sel_k),
        grid=(G, B, nq),
        in_specs=[pl.BlockSpec((1, 1, HG, tq, hd), lambda g, b, t: (b, g, 0, t, 0)),
                  seq(NC, hd), seq(hd, NC), seq(T, LANES), seq(LANES, T), seq(T, hd), seq(LANES, T),
                  pl.BlockSpec((1, 1, LANES, tq), lambda g, b, t: (b, g, 0, t)),
                  pl.BlockSpec((HG, len(NEAR_OFFSETS) + 1, tq, tq), lambda g, b, t: (g, 0, 0, 0),
                               pipeline_mode=pl.Buffered(1)),
                  pl.BlockSpec((HG, NC, tq), lambda g, b, t: (g, 0, t)),
                  pl.BlockSpec(ovl.shape, lambda g, b, t: (0, 0))],
        out_specs=pl.BlockSpec((1, tq, W), lambda g, b, t: (b, t, g)),
        out_shape=jax.ShapeDtypeStruct((B, T, G * W), BF16),
        scratch_shapes=[pltpu.VMEM((R, LANES), BF16),
                        pltpu.VMEM((2 * tq, R), F32), pltpu.VMEM((2 * tq, R), F32),
                        pltpu.VMEM((SUBLANES, R), F32), pltpu.VMEM((SUBLANES, R), F32),
                        pltpu.VMEM((SUBLANES, R), F32), pltpu.VMEM((SUBLANES, R), F32),
                        pltpu.VMEM((LANES, R), F32), pltpu.VMEM((LANES, R), F32)],
        compiler_params=pltpu.CompilerParams(vmem_limit_bytes=NSA_VMEM_LIMIT),
        name="nsa_attention",
    )(qn, kc, vct, ks, vst, kw, vwt, gates_t, near, cmpb, ovl)


def _merge_kernel(x_ref, oret_ref, on_ref, ma_ref, mb_ref, wr_ref, wn_ref, wo_ref, o_ref):
    y_ret = _dot(oret_ref[...], wr_ref[...])
    y_nsa = _dot(on_ref[...], wn_ref[...])
    mixed = ma_ref[...].astype(F32) * y_ret + mb_ref[...].astype(F32) * y_nsa
    o_ref[...] = x_ref[...] + _dot(mixed.astype(BF16), wo_ref[...])


def _merge(x2, oret, on, ma, mb, w_o_ret, w_o_nsa, w_out):
    M, D = x2.shape
    tm = min(TM_MERGE, M)
    row = lambda width: pl.BlockSpec((tm, width), lambda i: (i, 0))
    res = lambda a: pl.BlockSpec(a.shape, lambda i: (0, 0), pipeline_mode=pl.Buffered(1))
    w = [a.astype(BF16) for a in (w_o_ret, w_o_nsa, w_out)]
    return pl.pallas_call(
        _merge_kernel,
        grid=(M // tm,),
        in_specs=[row(D), row(oret.shape[1]), row(D), row(D), row(D)] + [res(a) for a in w],
        out_specs=row(D),
        out_shape=jax.ShapeDtypeStruct((M, D), F32),
        compiler_params=pltpu.CompilerParams(vmem_limit_bytes=48 * MIB),
        name="merge_out_projection",
    )(x2, oret, on, ma, mb, *w)


def _rms(x, g):
    return x * lax.rsqrt(jnp.mean(x * x, -1, keepdims=True) + EPS) * g


def _ffn_kernel(x_ref, g_ref, wi_ref, wo_ref, gf_ref, o_ref, h_scr, acc_scr, *, final_norm):
    x = x_ref[...]
    h_scr[...] = _rms(x, g_ref[...]).astype(BF16)
    acc_scr[...] = x
    cw = FFN_HIDDEN // 2
    for c in range(2):
        a = _dot(h_scr[...], wi_ref[:, c * cw:(c + 1) * cw])
        b = _dot(h_scr[...], wi_ref[:, FFN_HIDDEN + c * cw:FFN_HIDDEN + (c + 1) * cw])
        acc_scr[...] += _dot((_silu(a) * b).astype(BF16), wo_ref[c * cw:(c + 1) * cw, :])
    y = acc_scr[...]
    o_ref[...] = _rms(y, gf_ref[...]) if final_norm else y


def _ffn(x2, g, w_in, w_out, g_final, final_norm):
    M, D = x2.shape
    tm = min(TM_MERGE, M)
    row = pl.BlockSpec((tm, D), lambda i: (i, 0))
    vec = pl.BlockSpec((1, D), lambda i: (0, 0))
    res = lambda a: pl.BlockSpec(a.shape, lambda i: (0, 0), pipeline_mode=pl.Buffered(1))
    wi, wo = w_in.astype(BF16), w_out.astype(BF16)
    return pl.pallas_call(
        functools.partial(_ffn_kernel, final_norm=final_norm),
        grid=(M // tm,),
        in_specs=[row, vec, res(wi), res(wo), vec],
        out_specs=row,
        out_shape=jax.ShapeDtypeStruct((M, D), F32),
        scratch_shapes=[pltpu.VMEM((tm, D), BF16), pltpu.VMEM((tm, D), F32)],
        compiler_params=pltpu.CompilerParams(vmem_limit_bytes=52 * MIB),
        name="swiglu_ffn",
    )(x2, g.reshape(1, D), wi, wo, g_final.reshape(1, D))


def kernel(x, norm_mix_g, w_in, cmp_pos_k, cmp_pos_v, cmp_w1_k, cmp_w2_k, cmp_w1_v, cmp_w2_v, w_o_ret, w_o_nsa,
           w_out, norm_ffn_g, w_ffn_in, w_ffn_out, rel_bias, norm_final_g):
    B, T, D = x.shape
    depth = w_in.shape[0]
    G, HG = NSA_KV_GROUPS, NSA_HG
    half = RET_DK // 2
    freqs = ROPE_BASE ** (-jnp.arange(half, dtype=F32) / half)
    ang = jnp.arange(T, dtype=jnp.int32).astype(F32)[:, None] * freqs
    cos, sin = jnp.cos(ang), jnp.sin(ang)
    near, cmpb = _bias_tables(rel_bias, T, TQ)
    for i in range(depth):
        qr, kr, vr, gr, qn, kvc16, ks, vst, kw, vwt, gates_t, ma, mb = _in_projection(
            x, norm_mix_g[i], _pack_w_in(w_in[i]), cos, sin)
        o_ret = _retention(qr, kr, vr, gr)
        kc, vct = _compress(kvc16,cmp_pos_k[i], cmp_pos_v[i], cmp_w1_k[i], cmp_w2_k[i], cmp_w1_v[i], cmp_w2_v[i])
        o_n = _nsa_attention(qn.reshape(B, G, HG, T, NSA_HD), kc, vct, ks, vst, kw, vwt, gates_t, near, cmpb)
        o_n = o_n.reshape(B * T, NSA_HEADS * NSA_HD)
        x2 = _merge(x.reshape(B * T, D), o_ret.reshape(B * T, -1), o_n, ma.reshape(B * T, D), mb.reshape(B * T, D),
                    w_o_ret[i], w_o_nsa[i], w_out[i])
        x2 = _ffn(x2, norm_ffn_g[i], w_ffn_in[i], w_ffn_out[i], norm_final_g, final_norm=(i == depth - 1))
        x = x2.reshape(B, T, D)
    return x
```

```python
import functools
import math

import jax
import jax.numpy as jnp
import numpy as np
from jax import lax
from jax.experimental import pallas as pl
from jax.experimental.pallas import tpu as pltpu

F32 = jnp.float32
BF16 = jnp.bfloat16

D_MODEL = 1024
RET_HEADS = 4
RET_DK = 256
RET_DV = 512
RET_BLOCK = 256
ROPE_BASE = 10000.0
GN_EPS = 1e-5
NSA_HEADS = 16
NSA_KV_GROUPS = 2
NSA_HG = NSA_HEADS // NSA_KV_GROUPS
NSA_HD = 64
CMP_LEN = 32
CMP_STRIDE = 16
CMP_HIDDEN = 256
SEL_BLOCK = 64
SEL_TOPK = 8
WINDOW = 512
FORCED_SCORE = 1e4
REL_BUCKETS = 32
REL_MAX_DIST = 128
FFN_HIDDEN = -(-8 * D_MODEL // (3 * 256)) * 256
EPS = 1e-6
NEG_INF = -1e30
PICKED = -3e38
MASKED_CODE = REL_BUCKETS
NEAR_OFFSETS = (0, 1, 2)
LOG2E = math.log2(math.e)

LANES = 128
SUBLANES = 8
MIB = 1024 * 1024
NSA_VMEM_LIMIT = 56 * MIB

_SEG = np.cumsum([0, RET_HEADS * RET_DK, RET_HEADS * RET_DK, RET_HEADS * RET_DV, RET_HEADS * RET_DV,
                  NSA_HEADS * NSA_HD, 6 * NSA_KV_GROUPS * NSA_HD, 3 * NSA_HEADS, D_MODEL, D_MODEL])
N_QR, N_KR, N_VR, N_GR = RET_HEADS * RET_DK, RET_HEADS * RET_DK, RET_HEADS * RET_DV, RET_HEADS * RET_DV
N_QN, N_KV = NSA_HEADS * NSA_HD, 6 * NSA_KV_GROUPS * NSA_HD
N_GATE = NSA_KV_GROUPS * LANES
_P = np.cumsum([0, N_QR, N_KR, N_VR, N_GR, N_QN, N_KV, N_GATE, D_MODEL, D_MODEL])
P_QR, P_KR, P_VR, P_GR, P_QN, P_KV, P_GATE, P_MA, P_MB, P_END = [int(v) for v in _P]

TM_PROJ = 512
TM_MERGE = 512
TQ = 256
NT = (((1,), (1,)), ((), ()))


def _dot(a, b):
    return jnp.dot(a, b, preferred_element_type=F32)


def _dot_nt(a, b):
    return lax.dot_general(a, b, NT, preferred_element_type=F32)


def _sigmoid(x):
    return 1.0 / (1.0 + jnp.exp(-x))


def _silu(x):
    return x * _sigmoid(x)


def _inproj_kernel(x_ref, g_ref, cos_ref, sin_ref, feat_ref, w_ref,
                   qr_o, kr_o, vr_o, gr_o, qn_o, kvc_o, ks_o, vst_o, kw_o, vwt_o, gate_o, ma_o, mb_o, h_scr, kvc_scr):
    x = x_ref[0]
    h = x * lax.rsqrt(jnp.mean(x * x, -1, keepdims=True) + EPS) * g_ref[...]
    h_scr[...] = h.astype(BF16)
    cos = cos_ref[...]
    sin = sin_ref[...]
    half = RET_DK // 2

    def proj(lo, width):
        return _dot(h_scr[...], w_ref[:, lo:lo + width])

    y32 = proj(P_KV, N_KV)
    y = y32.astype(BF16)
    y_t = y32.T.astype(BF16)
    tm = y.shape[0]
    ones_row = (lax.broadcasted_iota(jnp.int32, (LANES - NSA_HD, tm), 0) == 0).astype(BF16)
    for g in range(NSA_KV_GROUPS):
        lo = lambda j: (j * NSA_KV_GROUPS + g) * NSA_HD
        ks_o[0, g, :, :NSA_HD] = y[:, lo(2):lo(2) + NSA_HD]
        ks_o[0, g, :, NSA_HD:] = feat_ref[:, NSA_HD:]
        kw_o[0, g] = y[:, lo(4):lo(4) + NSA_HD]
        for out, j in ((vst_o, 3), (vwt_o, 5)):
            out[0, g, :NSA_HD, :] = y_t[lo(j):lo(j) + NSA_HD, :]
            out[0, g, NSA_HD:, :] = ones_row
    for s in range(2):
        kvc_scr[s] = y32[:, s * LANES:(s + 1) * LANES]
        for l in range(CMP_STRIDE):
            rows = kvc_scr[s, pl.ds(l, tm // CMP_STRIDE, stride=CMP_STRIDE), :].astype(BF16)
            for g in range(NSA_KV_GROUPS):
                kvc_o[s, 0, g, :, l * NSA_HD:(l + 1) * NSA_HD] = rows[:, g * NSA_HD:(g + 1) * NSA_HD]
    gates_t = _sigmoid(proj(P_GATE, N_GATE)).T
    for g in range(NSA_KV_GROUPS):
        gate_o[0, g] = gates_t[g * LANES:(g + 1) * LANES, :]
    cw = 512
    per = cw // NSA_HD
    for c in range(N_QN // cw):
        y = (proj(P_QN + c * cw, cw) * (NSA_HD ** -0.5 * LOG2E)).astype(BF16)
        for j in range(per):
            qn_o[0, c * per + j] = y[:, j * NSA_HD:(j + 1) * NSA_HD]
    for hd in range(RET_HEADS):
        for out, base, scale in ((qr_o, P_QR, 1.0), (kr_o, P_KR, RET_DK ** -0.5)):
            y = proj(base + hd * RET_DK, RET_DK)
            x1, x2 = y[:, :half], y[:, half:]
            out[0, :, hd * RET_DK:hd * RET_DK + half] = ((x1 * cos - x2 * sin) * scale).astype(BF16)
            out[0, :, hd * RET_DK + half:(hd + 1) * RET_DK] = ((x1 * sin + x2 * cos) * scale).astype(BF16)
    for c in range(N_VR // cw):
        vr_o[0, :, c * cw:(c + 1) * cw] = proj(P_VR + c * cw, cw).astype(BF16)
        gr_o[0, :, c * cw:(c + 1) * cw] = _silu(proj(P_GR + c * cw, cw)).astype(BF16)
    for c in range(D_MODEL // cw):
        ma_o[0, :, c * cw:(c + 1) * cw] = _sigmoid(proj(P_MA + c * cw, cw)).astype(BF16)
        mb_o[0, :, c * cw:(c + 1) * cw] = _sigmoid(proj(P_MB + c * cw, cw)).astype(BF16)


def _pack_w_in(w):
    s = [int(v) for v in _SEG]
    gate = w[:, s[6]:s[7]].reshape(D_MODEL, NSA_KV_GROUPS, 3 * NSA_HG)
    gate = jnp.pad(gate, ((0, 0), (0, 0), (0, LANES - 3 * NSA_HG))).reshape(D_MODEL, N_GATE)
    return jnp.concatenate([w[:, s[0]:s[6]], gate, w[:, s[7]:s[9]]], axis=1).astype(BF16)


def _in_projection(x, g, w_packed, cos, sin):
    B, T, D = x.shape
    G = NSA_KV_GROUPS
    tm = min(TM_PROJ, T)
    nt = T // tm
    assert T // SEL_BLOCK <= LANES - NSA_HD
    own_block = (np.arange(T)[:, None] // SEL_BLOCK) == (np.arange(LANES)[None, :] - NSA_HD)
    feat = jnp.asarray(np.where(own_block, NEG_INF, 0.0), BF16)
    row = lambda width: pl.BlockSpec((1, tm, width), lambda b, t: (b, t, 0))
    grp = lambda width: pl.BlockSpec((1, G, tm, width), lambda b, t: (b, 0, t, 0))
    grp_t = pl.BlockSpec((1, G, LANES, tm), lambda b, t: (b, 0, 0, t))
    grp_shape = lambda width: jax.ShapeDtypeStruct((B, G, T, width), BF16)
    grp_t_shape = lambda dt: jax.ShapeDtypeStruct((B, G, LANES, T), dt)
    out_shape = (
        jax.ShapeDtypeStruct((B, T, N_QR), BF16), jax.ShapeDtypeStruct((B, T, N_KR), BF16),
        jax.ShapeDtypeStruct((B, T, N_VR), BF16), jax.ShapeDtypeStruct((B, T, N_GR), BF16),
        jax.ShapeDtypeStruct((B, NSA_HEADS, T, NSA_HD), BF16),
        jax.ShapeDtypeStruct((2, B, G, T // CMP_STRIDE, CMP_STRIDE * NSA_HD), BF16),
        grp_shape(LANES), grp_t_shape(BF16), grp_shape(NSA_HD), grp_t_shape(BF16),
        grp_t_shape(F32),
        jax.ShapeDtypeStruct((B, T, D_MODEL), BF16), jax.ShapeDtypeStruct((B, T, D_MODEL), BF16))
    out_specs = (
        row(N_QR), row(N_KR), row(N_VR), row(N_GR),
        pl.BlockSpec((1, NSA_HEADS, tm, NSA_HD), lambda b, t: (b, 0, t, 0)),
        pl.BlockSpec((2, 1, G, tm // CMP_STRIDE, CMP_STRIDE * NSA_HD), lambda b, t: (0, b, 0, t, 0)),
        grp(LANES), grp_t, grp(NSA_HD), grp_t,
        grp_t,
        row(D_MODEL), row(D_MODEL))
    out_bytes = tm * (2 * (N_QR + N_KR + N_VR + N_GR + N_QN + 2 * D_MODEL) + 2 * G * (3 * NSA_HD + 3 * LANES)
                      + 4 * N_GATE)
    vmem = 2 * out_bytes + 2 * tm * D * 4 + 2 * D * P_END + tm * D * 2 + 10 * MIB
    return pl.pallas_call(
        _inproj_kernel,
        grid=(B, nt),
        in_specs=[pl.BlockSpec((1, tm, D), lambda b, t: (b, t, 0)),
                  pl.BlockSpec((1, D), lambda b, t: (0, 0)),
                  pl.BlockSpec((tm, RET_DK // 2), lambda b, t: (t, 0)),
                  pl.BlockSpec((tm, RET_DK // 2), lambda b, t: (t, 0)),
                  pl.BlockSpec((tm, LANES), lambda b, t: (t, 0)),
                  pl.BlockSpec((D, P_END), lambda b, t: (0, 0), pipeline_mode=pl.Buffered(1))],
        out_specs=out_specs,
        out_shape=out_shape,
        scratch_shapes=[pltpu.VMEM((tm, D), BF16), pltpu.VMEM((2, tm, G * NSA_HD), F32)],
        compiler_params=pltpu.CompilerParams(vmem_limit_bytes=int(vmem)),
        name="in_projection",
    )(x, g.reshape(1, D), cos, sin, feat, w_packed)


def _retention_kernel(q_ref, k_ref, v_ref, g_ref, dmask_ref, xi_ref, zeta_ref, gch_ref, o_ref):
    C = RET_BLOCK
    nc = q_ref.shape[1] // C
    dmask = dmask_ref[0]
    xi = xi_ref[0]
    zeta = zeta_ref[0]
    gch = gch_ref[0, :, 0:1]
    r = None
    for c in range(nc):
        rows = slice(c * C, (c + 1) * C)
        qc = q_ref[0, rows, :]
        kc = k_ref[0, rows, :]
        vc = v_ref[0, rows, :]
        if c < nc - 1:
            kz = (kc.astype(F32) * zeta).astype(BF16)
            kv = lax.dot_general(kz, vc, (((0,), (0,)), ((), ())), preferred_element_type=F32)
            r_next = kv if c == 0 else gch * r + kv
        s = _dot_nt(qc, kc) * dmask
        o = _dot(s.astype(BF16), vc)
        if c > 0:
            o = o + _dot(qc, r.astype(BF16)) * xi
        if c < nc - 1:
            r = r_next
        mu = jnp.mean(o, -1, keepdims=True)
        d = o - mu
        var = jnp.mean(d * d, -1, keepdims=True)
        on = d * lax.rsqrt(var + GN_EPS)
        o_ref[0, rows, :] = (g_ref[0, rows, :].astype(F32) * on).astype(BF16)


def _retention(qr, kr, vr, gr):
    B, T, _ = qr.shape
    C = RET_BLOCK
    lg = jnp.log(1.0 - 2.0 ** (-5.0 - jnp.arange(RET_HEADS, dtype=F32)))
    n = jnp.arange(C, dtype=F32)
    diff = n[:, None] - n[None, :]
    dmask = jnp.where(diff >= 0, jnp.exp(jnp.maximum(diff, 0.0)[None] * lg[:, None, None]), 0.0)
    xi = jnp.exp((n + 1.0)[None] * lg[:, None])[:, :, None]
    zeta = jnp.exp((C - 1.0 - n)[None] * lg[:, None])[:, :, None]
    gch = jnp.broadcast_to(jnp.exp(C * lg)[:, None, None], (RET_HEADS, 1, LANES))
    qk_spec = pl.BlockSpec((1, T, RET_DK), lambda b, h: (b, 0, h))
    v_spec = pl.BlockSpec((1, T, RET_DV), lambda b, h: (b, 0, h))
    return pl.pallas_call(
        _retention_kernel,
        grid=(B, RET_HEADS),
        in_specs=[qk_spec, qk_spec, v_spec, v_spec,
                  pl.BlockSpec((1, C, C), lambda b, h: (h, 0, 0)),
                  pl.BlockSpec((1, C, 1), lambda b, h: (h, 0, 0)),
                  pl.BlockSpec((1, C, 1), lambda b, h: (h, 0, 0)),
                  pl.BlockSpec((1, 1, LANES), lambda b, h: (h, 0, 0))],
        out_specs=v_spec,
        out_shape=jax.ShapeDtypeStruct((B, T, RET_HEADS * RET_DV), BF16),
        compiler_params=pltpu.CompilerParams(vmem_limit_bytes=32 * MIB),
        name="retention",
    )(qr, kr, vr, gr, dmask, xi, zeta, gch)


def _gelu_tanh(x):
    return 0.5 * x * (1.0 + jnp.tanh(math.sqrt(2.0 / math.pi) * (x + 0.044715 * (x * x * x))))


def _compress_kernel(x_ref, pk_ref, pv_ref, w1k_ref, w2k_ref, w1v_ref, w2vt_ref, ko_ref, vto_ref):
    half = CMP_STRIDE * NSA_HD

    def hidden(s, g, p_ref, w1_ref):
        x = x_ref[s, 0, g].astype(F32)
        a = _dot((x + p_ref[0:1, :]).astype(BF16), w1_ref[0:half, :])
        b = _dot((x + p_ref[1:2, :]).astype(BF16), w1_ref[half:2 * half, :])
        pre = a + pltpu.roll(b, b.shape[0] - 1, 0)
        return _gelu_tanh(pre).astype(BF16)

    for g in range(NSA_KV_GROUPS):
        ko_ref[0, g] = _dot(hidden(0, g, pk_ref, w1k_ref), w2k_ref[...]).astype(BF16)
        vto_ref[0, g] = _dot_nt(w2vt_ref[...], hidden(1, g, pv_ref, w1v_ref)).astype(BF16)


def _compress(kvc16, pos_k, pos_v, w1k, w2k, w1v, w2v):
    _, B, G, NC, F = kvc16.shape
    full = lambda a: pl.BlockSpec(a.shape, lambda b: (0,) * a.ndim)
    pk = pos_k.reshape(2, F)
    pv = pos_v.reshape(2, F)
    w = [a.astype(BF16) for a in (w1k, w2k, w1v, w2v.T)]
    return pl.pallas_call(
        _compress_kernel,
        grid=(B,),
        in_specs=[pl.BlockSpec((2, 1, G, NC, F), lambda b: (0, b, 0, 0, 0)), full(pk), full(pv)]
        + [full(a) for a in w],
        out_specs=(pl.BlockSpec((1, G, NC, NSA_HD), lambda b: (b, 0, 0, 0)),
                   pl.BlockSpec((1, G, NSA_HD, NC), lambda b: (b, 0, 0, 0))),
        out_shape=(jax.ShapeDtypeStruct((B, G, NC, NSA_HD), BF16), jax.ShapeDtypeStruct((B, G, NSA_HD, NC), BF16)),
        compiler_params=pltpu.CompilerParams(vmem_limit_bytes=32 * MIB),
        name="nsa_compress",
    )(kvc16, pk, pv, *w)


def _bias_kernel(tab_ref, nb_ref, cb_ref, near_o, cmp_o):
    h = pl.program_id(0)
    far = tab_ref[REL_BUCKETS - 1, h]

    def build(bk):
        acc = jnp.zeros(bk.shape, F32)
        for k in range(REL_BUCKETS - 1):
            acc = jnp.where(bk == k, (tab_ref[k, h] - far) * LOG2E, acc)
        return jnp.where(bk == MASKED_CODE, NEG_INF, acc)

    near_o[0] = build(nb_ref[...])
    cmp_o[0] = build(cb_ref[...])


def _t5_bucket(dist):
    dist = jnp.maximum(dist, 0)
    max_exact = REL_BUCKETS // 2
    large = max_exact + (jnp.log(jnp.maximum(dist, 1).astype(F32) / max_exact)
                         / math.log(REL_MAX_DIST / max_exact) * (REL_BUCKETS - max_exact)).astype(jnp.int32)
    large = jnp.minimum(large, REL_BUCKETS - 1)
    return jnp.where(dist < max_exact, dist, large)


def _bias_tables(rel_bias, T, tq):
    nc = T // CMP_STRIDE
    i = jnp.arange(tq, dtype=jnp.int32)
    d = jnp.asarray(NEAR_OFFSETS, jnp.int32)[:, None, None] * tq + i[None, None, :] - i[None, :, None]
    near_b = jnp.where((d < 0) | (d >= WINDOW), MASKED_CODE, _t5_bucket(d))
    near_b = jnp.concatenate([near_b, jnp.full((1, tq, tq), MASKED_CODE, jnp.int32)], axis=0)
    cmp_end = jnp.arange(nc, dtype=jnp.int32) * CMP_STRIDE + CMP_LEN - 1
    cmp_b = _t5_bucket(jnp.arange(T, dtype=jnp.int32)[None, :] - cmp_end[:, None])
    return pl.pallas_call(
        _bias_kernel,
        grid=(NSA_HEADS,),
        in_specs=[pl.BlockSpec(memory_space=pltpu.SMEM),
                  pl.BlockSpec(near_b.shape, lambda h: (0, 0, 0)),
                  pl.BlockSpec(cmp_b.shape, lambda h: (0, 0))],
        out_specs=(pl.BlockSpec((1,) + near_b.shape, lambda h: (h, 0, 0, 0)),
                   pl.BlockSpec((1,) + cmp_b.shape, lambda h: (h, 0, 0))),
        out_shape=(jax.ShapeDtypeStruct((NSA_HEADS,) + near_b.shape, F32),
                   jax.ShapeDtypeStruct((NSA_HEADS,) + cmp_b.shape, F32)),
        name="rel_bias_tables",
    )(rel_bias, near_b, cmp_b)


def _nsa_kernel(q_ref, kc_ref, vct_ref, ks_ref, vst_ref, kw_ref, vwt_ref, gt_ref,
                near_ref, cb_ref, ovl_ref, o_ref,
                qa_scr, sa_scr, sb_scr, cma_scr, cmb_scr, ms_scr, mw_scr, accs_scr, accw_scr, *, tq, sel_k):
    HG = NSA_HG
    R = HG * tq
    NC = kc_ref.shape[2]
    NS = ovl_ref.shape[0]
    UW = tq
    ti = pl.program_id(2)
    s0 = ti * tq

    def head(x, h):
        return x[:, h * tq:(h + 1) * tq]

    def keys(ref, u, n=1):
        return ref[0, 0, pl.ds(pl.multiple_of(u * UW, UW), n * UW), :]

    def values_t(ref, u, n=1):
        return ref[0, 0, :, pl.ds(pl.multiple_of(u * UW, UW), n * UW)]

    cmax = lambda s_ref: cma_scr if s_ref is sa_scr else cmb_scr

    def score(s_ref, k_ref, u, n, queries, entries=None):
        def produce(h):
            s = _dot_nt(keys(k_ref, u, n), queries(h))
            if entries is not None:
                s = jnp.concatenate([s[i * UW:(i + 1) * UW] + near_ref[h, e] for i, e in enumerate(entries)], axis=0)
            s_ref[0:n * UW, h * tq:(h + 1) * tq] = s
            groups = [s[r * SUBLANES:(r + 1) * SUBLANES] for r in range(n * UW // SUBLANES)]
            cmax(s_ref)[:, h * tq:(h + 1) * tq] = functools.reduce(jnp.maximum, groups)
        return produce

    def fold(s_ref, n, m_ref, vt_ref, u, acc_ref):
        def consume(h):
            c = slice(h * tq, (h + 1) * tq)
            m_old = m_ref[0:1, c]
            m_new = jnp.maximum(m_old, jnp.max(cmax(s_ref)[:, c], axis=0, keepdims=True))
            part = [_dot(values_t(vt_ref, u + i), jnp.exp2((s_ref[i * UW:(i + 1) * UW, c] - m_new).astype(BF16)))
                    for i in range(n)]
            acc_ref[:, c] = jnp.exp2(m_old - m_new) * acc_ref[:, c] + functools.reduce(jnp.add, part)
            m_ref[:, c] = jnp.broadcast_to(m_new, (SUBLANES, tq))
        return consume

    def stage(produce=None, consume=None):
        for h in range(HG):
            if produce is not None:
                produce(h)
            if consume is not None:
                consume(h)

    q_head = lambda h: q_ref[0, 0, h]
    qa_head = lambda h: qa_scr[h * tq:(h + 1) * tq, :]

    def normalized(acc_ref):
        acc = acc_ref[...]
        return acc[:NSA_HD] / acc[NSA_HD:NSA_HD + 1]

    for ref in (ms_scr, mw_scr):
        ref[...] = jnp.full(ref.shape, NEG_INF, F32)
    for ref in (accs_scr, accw_scr):
        ref[...] = jnp.zeros_like(ref)

    masked = len(NEAR_OFFSETS)
    w0 = jnp.maximum(ti - 2, 0)

    def window_entry(slot):
        back = jnp.minimum(ti, 2) - slot
        return jnp.where(back >= 0, back, masked)

    win_a = (sa_scr, 2, mw_scr, vwt_ref, w0, accw_scr)
    win_b = (sb_scr, 1, mw_scr, vwt_ref, w0 + 2, accw_scr)
    window_a = score(sa_scr, kw_ref, w0, 2, q_head, (window_entry(0), window_entry(1)))
    window_b = score(sb_scr, kw_ref, w0 + 2, 1, q_head, (window_entry(2),))

    stage(window_a)

    pos_c = s0 + lax.broadcasted_iota(jnp.int32, (NC, tq), 1)
    cend = lax.broadcasted_iota(jnp.int32, (NC, tq), 0) * CMP_STRIDE + (CMP_LEN - 1)
    valid_c = cend <= pos_c
    valid_f = valid_c.astype(F32)
    lc = [_dot_nt(kc_ref[0, 0], q_head(h)) + cb_ref[h] for h in range(HG)]
    pc = []
    for h in range(HG):
        z = jnp.where(valid_c, lc[h], NEG_INF)
        e = jnp.exp2(z - jnp.max(z, 0, keepdims=True)) * valid_f
        pc.append(e / jnp.maximum(jnp.sum(e, 0, keepdims=True), 1e-30))
    oc = [_dot(vct_ref[0, 0], pc[h].astype(BF16)) for h in range(HG)]
    psum = functools.reduce(jnp.add, pc)

    p_hi = psum.astype(BF16)
    r1 = psum - p_hi.astype(F32)
    p_mid = r1.astype(BF16)
    p_lo = (r1 - p_mid.astype(F32)).astype(BF16)
    ovl = ovl_ref[...]
    imp = _dot(ovl, p_hi) + _dot(ovl, p_mid) + _dot(ovl, p_lo)
    jrow = lax.broadcasted_iota(jnp.int32, (NS, tq), 0)
    posl = s0 + lax.broadcasted_iota(jnp.int32, (NS, tq), 1)
    cur = lax.shift_right_logical(posl, int(math.log2(SEL_BLOCK)))
    forced = (jrow == 0) | (jrow == cur) | (jrow == cur - 1)
    started = jrow * SEL_BLOCK <= posl
    imp = jnp.where(forced, FORCED_SCORE, imp)
    imp = jnp.where(started, imp, NEG_INF)
    sel = jnp.zeros((NS, tq), F32)
    consume_a = fold(*win_a)
    for it in range(max(sel_k, HG)):
        if it < sel_k:
            mx = jnp.max(imp, axis=0, keepdims=True)
            first = jnp.min(jnp.where(imp == mx, jrow, NS), axis=0, keepdims=True)
            hit = jrow == first
            sel = jnp.where(hit, 1.0, sel)
            imp = jnp.where(hit, PICKED, imp)
        if it < HG:
            window_b(it)
            consume_a(it)
    notsel = 1.0 - jnp.where(started, sel, 0.0)
    notsel = jnp.concatenate([notsel, jnp.zeros((LANES - NS, tq), F32)], axis=0)
    notsel = pltpu.roll(notsel.T, NSA_HD, 1).astype(BF16)
    for hg in range(HG):
        qa_scr[hg * tq:(hg + 1) * tq, :NSA_HD] = q_ref[0, 0, hg]
        qa_scr[hg * tq:(hg + 1) * tq, NSA_HD:] = notsel[:, NSA_HD:]

    prev_u = jnp.where(ti >= 1, ti - 1, 1)
    sel_job = lambda s_ref, u, n: (s_ref, n, ms_scr, vst_ref, u, accs_scr)
    stage(score(sa_scr, ks_ref, ti, 1, qa_head, (0,)), fold(*win_b))
    stage(score(sb_scr, ks_ref, prev_u, 1, qa_head, (jnp.where(ti >= 1, 1, masked),)), fold(*sel_job(sa_scr, ti, 1)))
    stage(None, fold(*sel_job(sb_scr, prev_u, 1)))

    n_plain = jnp.maximum(ti - 1, 0)
    n_quads = lax.shift_right_logical(n_plain, 2)

    def plain_run(first, count):
        bufs = (sa_scr, sb_scr)
        for i in range(count + 1):
            produce = score(bufs[i % 2], ks_ref, first + i, 1, qa_head) if i < count else None
            consume = fold(*sel_job(bufs[(i - 1) % 2], first + i - 1, 1)) if i > 0 else None
            stage(produce, consume)

    def plain_quad(j, carry):
        plain_run(4 * j, 4)
        return carry

    lax.fori_loop(0, n_quads, plain_quad, 0)
    left = n_plain - 4 * n_quads

    @pl.when(left >= 2)
    def _():
        plain_run(4 * n_quads, 2)

    @pl.when(lax.rem(left, 2) == 1)
    def _():
        stage(score(sa_scr, ks_ref, n_plain - 1, 1, qa_head))
        stage(None, fold(*sel_job(sa_scr, n_plain - 1, 1)))

    gt = gt_ref[0, 0]
    o_s = normalized(accs_scr)
    o_w = normalized(accw_scr)

    def gated(h):
        return (gt[3 * h:3 * h + 1] * oc[h] + gt[3 * h + 1:3 * h + 2] * head(o_s, h)
                + gt[3 * h + 2:3 * h + 3] * head(o_w, h))

    pairs = [jnp.concatenate([gated(2 * j), gated(2 * j + 1)], axis=0).T for j in range(HG // 2)]
    o_ref[0] = jnp.concatenate(pairs, axis=1).astype(BF16)


def _nsa_attention(qn, kc, vct, ks, vst, kw, vwt, gates_t, near, cmpb):
    B, G, HG, T, hd = qn.shape
    tq = TQ
    assert T % tq == 0 and T >= 3 * tq and WINDOW == 2 * tq and tq >= 2 * REL_MAX_DIST and tq % LANES == 0
    nq = T // tq
    NC = kc.shape[2]
    NS = T // SEL_BLOCK
    sel_k = min(SEL_TOPK, NS)
    n = np.arange(NC)
    j = np.arange(NS)
    ovl = ((n[None, :] * CMP_STRIDE < (j[:, None] + 1) * SEL_BLOCK)
           & (n[None, :] * CMP_STRIDE + CMP_LEN - 1 >= j[:, None] * SEL_BLOCK) & (n[None, :] < NC - 1))
    ovl = jnp.asarray(ovl, BF16)
    seq = lambda rows, width: pl.BlockSpec((1, 1, rows, width), lambda g, b, t: (b, g, 0, 0))
    R = HG * tq
    W = HG * hd
    return pl.pallas_call(
        functools.partial(_nsa_kernel, tq=tq, sel_k=sel_k),
        grid=(G, B, nq),
        in_specs=[pl.BlockSpec((1, 1, HG, tq, hd), lambda g, b, t: (b, g, 0, t, 0)),
                  seq(NC, hd), seq(hd, NC), seq(T, LANES), seq(LANES, T), seq(T, hd), seq(LANES, T),
                  pl.BlockSpec((1, 1, LANES, tq), lambda g, b, t: (b, g, 0, t)),
                  pl.BlockSpec((HG, len(NEAR_OFFSETS) + 1, tq, tq), lambda g, b, t: (g, 0, 0, 0),
                               pipeline_mode=pl.Buffered(1)),
                  pl.BlockSpec((HG, NC, tq), lambda g, b, t: (g, 0, t)),
                  pl.BlockSpec(ovl.shape, lambda g, b, t: (0, 0))],
        out_specs=pl.BlockSpec((1, tq, W), lambda g, b, t: (b, t, g)),
        out_shape=jax.ShapeDtypeStruct((B, T, G * W), BF16),
        scratch_shapes=[pltpu.VMEM((R, LANES), BF16),
                        pltpu.VMEM((2 * tq, R), F32), pltpu.VMEM((2 * tq, R), F32),
                        pltpu.VMEM((SUBLANES, R), F32), pltpu.VMEM((SUBLANES, R), F32),
                        pltpu.VMEM((SUBLANES, R), F32), pltpu.VMEM((SUBLANES, R), F32),
                        pltpu.VMEM((LANES, R), F32), pltpu.VMEM((LANES, R), F32)],
        compiler_params=pltpu.CompilerParams(vmem_limit_bytes=NSA_VMEM_LIMIT),
        name="nsa_attention",
    )(qn, kc, vct, ks, vst, kw, vwt, gates_t, near, cmpb, ovl)


def _merge_kernel(x_ref, oret_ref, on_ref, ma_ref, mb_ref, wr_ref, wn_ref, wo_ref, o_ref):
    y_ret = _dot(oret_ref[...], wr_ref[...])
    y_nsa = _dot(on_ref[...], wn_ref[...])
    mixed = ma_ref[...].astype(F32) * y_ret + mb_ref[...].astype(F32) * y_nsa
    o_ref[...] = x_ref[...] + _dot(mixed.astype(BF16), wo_ref[...])


def _merge(x2, oret, on, ma, mb, w_o_ret, w_o_nsa, w_out):
    M, D = x2.shape
    tm = min(TM_MERGE, M)
    row = lambda width: pl.BlockSpec((tm, width), lambda i: (i, 0))
    res = lambda a: pl.BlockSpec(a.shape, lambda i: (0, 0), pipeline_mode=pl.Buffered(1))
    w = [a.astype(BF16) for a in (w_o_ret, w_o_nsa, w_out)]
    return pl.pallas_call(
        _merge_kernel,
        grid=(M // tm,),
        in_specs=[row(D), row(oret.shape[1]), row(D), row(D), row(D)] + [res(a) for a in w],
        out_specs=row(D),
        out_shape=jax.ShapeDtypeStruct((M, D), F32),
        compiler_params=pltpu.CompilerParams(vmem_limit_bytes=48 * MIB),
        name="merge_out_projection",
    )(x2, oret, on, ma, mb, *w)


def _rms(x, g):
    return x * lax.rsqrt(jnp.mean(x * x, -1, keepdims=True) + EPS) * g


def _ffn_kernel(x_ref, g_ref, wi_ref, wo_ref, gf_ref, o_ref, h_scr, acc_scr, *, final_norm):
    x = x_ref[...]
    h_scr[...] = _rms(x, g_ref[...]).astype(BF16)
    acc_scr[...] = x
    cw = FFN_HIDDEN // 2
    for c in range(2):
        a = _dot(h_scr[...], wi_ref[:, c * cw:(c + 1) * cw])
        b = _dot(h_scr[...], wi_ref[:, FFN_HIDDEN + c * cw:FFN_HIDDEN + (c + 1) * cw])
        acc_scr[...] += _dot((_silu(a) * b).astype(BF16), wo_ref[c * cw:(c + 1) * cw, :])
    y = acc_scr[...]
    o_ref[...] = _rms(y, gf_ref[...]) if final_norm else y


def _ffn(x2, g, w_in, w_out, g_final, final_norm):
    M, D = x2.shape
    tm = min(TM_MERGE, M)
    row = pl.BlockSpec((tm, D), lambda i: (i, 0))
    vec = pl.BlockSpec((1, D), lambda i: (0, 0))
    res = lambda a: pl.BlockSpec(a.shape, lambda i: (0, 0), pipeline_mode=pl.Buffered(1))
    wi, wo = w_in.astype(BF16), w_out.astype(BF16)
    return pl.pallas_call(
        functools.partial(_ffn_kernel, final_norm=final_norm),
        grid=(M // tm,),
        in_specs=[row, vec, res(wi), res(wo), vec],
        out_specs=row,
        out_shape=jax.ShapeDtypeStruct((M, D), F32),
        scratch_shapes=[pltpu.VMEM((tm, D), BF16), pltpu.VMEM((tm, D), F32)],
        compiler_params=pltpu.CompilerParams(vmem_limit_bytes=52 * MIB),
        name="swiglu_ffn",
    )(x2, g.reshape(1, D), wi, wo, g_final.reshape(1, D))


def kernel(x, norm_mix_g, w_in, cmp_pos_k, cmp_pos_v, cmp_w1_k, cmp_w2_k, cmp_w1_v, cmp_w2_v, w_o_ret, w_o_nsa,
           w_out, norm_ffn_g, w_ffn_in, w_ffn_out, rel_bias, norm_final_g):
    B, T, D = x.shape
    depth = w_in.shape[0]
    G, HG = NSA_KV_GROUPS, NSA_HG
    half = RET_DK // 2
    freqs = ROPE_BASE ** (-jnp.arange(half, dtype=F32) / half)
    ang = jnp.arange(T, dtype=jnp.int32).astype(F32)[:, None] * freqs
    cos, sin = jnp.cos(ang), jnp.sin(ang)
    near, cmpb = _bias_tables(rel_bias, T, TQ)
    for i in range(depth):
        qr, kr, vr, gr, qn, kvc16, ks, vst, kw, vwt, gates_t, ma, mb = _in_projection(
            x, norm_mix_g[i], _pack_w_in(w_in[i]), cos, sin)
        o_ret = _retention(qr, kr, vr, gr)
        kc, vct = _compress(kvc16,cmp_pos_k[i], cmp_pos_v[i], cmp_w1_k[i], cmp_w2_k[i], cmp_w1_v[i], cmp_w2_v[i])
        o_n = _nsa_attention(qn.reshape(B, G, HG, T, NSA_HD), kc, vct, ks, vst, kw, vwt, gates_t, near, cmpb)
        o_n = o_n.reshape(B * T, NSA_HEADS * NSA_HD)
        x2 = _merge(x.reshape(B * T, D), o_ret.reshape(B * T, -1), o_n, ma.reshape(B * T, D), mb.reshape(B * T, D),
                    w_o_ret[i], w_o_nsa[i], w_out[i])
        x2 = _ffn(x2, norm_ffn_g[i], w_ffn_in[i], w_ffn_out[i], norm_final_g, final_norm=(i == depth - 1))
        x = x2.reshape(B, T, D)
    return x
```

```python
import functools
import math

import jax
import jax.numpy as jnp
import numpy as np
from jax import lax
from jax.experimental import pallas as pl
from jax.experimental.pallas import tpu as pltpu

F32 = jnp.float32
BF16 = jnp.bfloat16

D_MODEL = 1024
RET_HEADS = 4
RET_DK = 256
RET_DV = 512
RET_BLOCK = 256
ROPE_BASE = 10000.0
GN_EPS = 1e-5
NSA_HEADS = 16
NSA_KV_GROUPS = 2
NSA_HG = NSA_HEADS // NSA_KV_GROUPS
NSA_HD = 64
CMP_LEN = 32
CMP_STRIDE = 16
CMP_HIDDEN = 256
SEL_BLOCK = 64
SEL_TOPK = 8
WINDOW = 512
FORCED_SCORE = 1e4
REL_BUCKETS = 32
REL_MAX_DIST = 128
FFN_HIDDEN = -(-8 * D_MODEL // (3 * 256)) * 256
EPS = 1e-6
NEG_INF = -1e30
PICKED = -3e38
MASKED_CODE = REL_BUCKETS
NEAR_OFFSETS = (0, 1, 2)
LOG2E = math.log2(math.e)

LANES = 128
SUBLANES = 8
MIB = 1024 * 1024
NSA_VMEM_LIMIT = 56 * MIB
NSA_VT_ROWS = NSA_HD + 16

_SEG = np.cumsum([0, RET_HEADS * RET_DK, RET_HEADS * RET_DK, RET_HEADS * RET_DV, RET_HEADS * RET_DV,
                  NSA_HEADS * NSA_HD, 6 * NSA_KV_GROUPS * NSA_HD, 3 * NSA_HEADS, D_MODEL, D_MODEL])
N_QR, N_KR, N_VR, N_GR = RET_HEADS * RET_DK, RET_HEADS * RET_DK, RET_HEADS * RET_DV, RET_HEADS * RET_DV
N_QN, N_KV = NSA_HEADS * NSA_HD, 6 * NSA_KV_GROUPS * NSA_HD
N_GATE = NSA_KV_GROUPS * LANES
_P = np.cumsum([0, N_QR, N_KR, N_VR, N_GR, N_QN, N_KV, N_GATE, D_MODEL, D_MODEL])
P_QR, P_KR, P_VR, P_GR, P_QN, P_KV, P_GATE, P_MA, P_MB, P_END = [int(v) for v in _P]

TM_PROJ = 512
TM_MERGE = 512
TQ = 256
NT = (((1,), (1,)), ((), ()))


def _dot(a, b):
    return jnp.dot(a, b, preferred_element_type=F32)


def _dot_nt(a, b):
    return lax.dot_general(a, b, NT, preferred_element_type=F32)


def _sigmoid(x):
    return 1.0 / (1.0 + jnp.exp(-x))


def _silu(x):
    return x * _sigmoid(x)


def _inproj_kernel(x_ref, g_ref, cos_ref, sin_ref, feat_ref, w_ref,
                   qr_o, kr_o, vr_o, gr_o, qn_o, kvc_o, ks_o, vst_o, kw_o, vwt_o, gate_o, ma_o, mb_o, h_scr, kvc_scr):
    x = x_ref[0]
    h = x * lax.rsqrt(jnp.mean(x * x, -1, keepdims=True) + EPS) * g_ref[...]
    h_scr[...] = h.astype(BF16)
    cos = cos_ref[...]
    sin = sin_ref[...]
    half = RET_DK // 2

    def proj(lo, width):
        return _dot(h_scr[...], w_ref[:, lo:lo + width])

    y32 = proj(P_KV, N_KV)
    y = y32.astype(BF16)
    y_t = y32.T.astype(BF16)
    tm = y.shape[0]
    ones_row = (lax.broadcasted_iota(jnp.int32, (NSA_VT_ROWS - NSA_HD, tm), 0) == 0).astype(BF16)
    for g in range(NSA_KV_GROUPS):
        lo = lambda j: (j * NSA_KV_GROUPS + g) * NSA_HD
        ks_o[0, g, :, :NSA_HD] = y[:, lo(2):lo(2) + NSA_HD]
        ks_o[0, g, :, NSA_HD:] = feat_ref[:, NSA_HD:]
        kw_o[0, g] = y[:, lo(4):lo(4) + NSA_HD]
        for out, j in ((vst_o, 3), (vwt_o, 5)):
            out[0, g, :NSA_HD, :] = y_t[lo(j):lo(j) + NSA_HD, :]
            out[0, g, NSA_HD:, :] = ones_row
    for s in range(2):
        kvc_scr[s] = y32[:, s * LANES:(s + 1) * LANES]
        for l in range(CMP_STRIDE):
            rows = kvc_scr[s, pl.ds(l, tm // CMP_STRIDE, stride=CMP_STRIDE), :].astype(BF16)
            for g in range(NSA_KV_GROUPS):
                kvc_o[s, 0, g, :, l * NSA_HD:(l + 1) * NSA_HD] = rows[:, g * NSA_HD:(g + 1) * NSA_HD]
    gates_t = _sigmoid(proj(P_GATE, N_GATE)).T
    for g in range(NSA_KV_GROUPS):
        gate_o[0, g] = gates_t[g * LANES:(g + 1) * LANES, :]
    cw = 512
    per = cw // NSA_HD
    for c in range(N_QN // cw):
        y = (proj(P_QN + c * cw, cw) * (NSA_HD ** -0.5 * LOG2E)).astype(BF16)
        for j in range(per):
            qn_o[0, c * per + j] = y[:, j * NSA_HD:(j + 1) * NSA_HD]
    for hd in range(RET_HEADS):
        for out, base, scale in ((qr_o, P_QR, 1.0), (kr_o, P_KR, RET_DK ** -0.5)):
            y = proj(base + hd * RET_DK, RET_DK)
            x1, x2 = y[:, :half], y[:, half:]
            out[0, :, hd * RET_DK:hd * RET_DK + half] = ((x1 * cos - x2 * sin) * scale).astype(BF16)
            out[0, :, hd * RET_DK + half:(hd + 1) * RET_DK] = ((x1 * sin + x2 * cos) * scale).astype(BF16)
    for c in range(N_VR // cw):
        vr_o[0, :, c * cw:(c + 1) * cw] = proj(P_VR + c * cw, cw).astype(BF16)
        gr_o[0, :, c * cw:(c + 1) * cw] = _silu(proj(P_GR + c * cw, cw)).astype(BF16)
    for c in range(D_MODEL // cw):
        ma_o[0, :, c * cw:(c + 1) * cw] = _sigmoid(proj(P_MA + c * cw, cw)).astype(BF16)
        mb_o[0, :, c * cw:(c + 1) * cw] = _sigmoid(proj(P_MB + c * cw, cw)).astype(BF16)


def _pack_w_in(w):
    s = [int(v) for v in _SEG]
    gate = w[:, s[6]:s[7]].reshape(D_MODEL, NSA_KV_GROUPS, 3 * NSA_HG)
    gate = jnp.pad(gate, ((0, 0), (0, 0), (0, LANES - 3 * NSA_HG))).reshape(D_MODEL, N_GATE)
    return jnp.concatenate([w[:, s[0]:s[6]], gate, w[:, s[7]:s[9]]], axis=1).astype(BF16)


def _in_projection(x, g, w_packed, cos, sin):
    B, T, D = x.shape
    G = NSA_KV_GROUPS
    tm = min(TM_PROJ, T)
    nt = T // tm
    assert T // SEL_BLOCK <= LANES - NSA_HD
    own_block = (np.arange(T)[:, None] // SEL_BLOCK) == (np.arange(LANES)[None, :] - NSA_HD)
    feat = jnp.asarray(np.where(own_block, NEG_INF, 0.0), BF16)
    row = lambda width: pl.BlockSpec((1, tm, width), lambda b, t: (b, t, 0))
    grp = lambda width: pl.BlockSpec((1, G, tm, width), lambda b, t: (b, 0, t, 0))
    grp_t = lambda rows: pl.BlockSpec((1, G, rows, tm), lambda b, t: (b, 0, 0, t))
    grp_shape = lambda width: jax.ShapeDtypeStruct((B, G, T, width), BF16)
    grp_t_shape = lambda rows, dt: jax.ShapeDtypeStruct((B, G, rows, T), dt)
    out_shape = (
        jax.ShapeDtypeStruct((B, T, N_QR), BF16), jax.ShapeDtypeStruct((B, T, N_KR), BF16),
        jax.ShapeDtypeStruct((B, T, N_VR), BF16), jax.ShapeDtypeStruct((B, T, N_GR), BF16),
        jax.ShapeDtypeStruct((B, NSA_HEADS, T, NSA_HD), BF16),
        jax.ShapeDtypeStruct((2, B, G, T // CMP_STRIDE, CMP_STRIDE * NSA_HD), BF16),
        grp_shape(LANES), grp_t_shape(NSA_VT_ROWS, BF16), grp_shape(NSA_HD), grp_t_shape(NSA_VT_ROWS, BF16),
        grp_t_shape(LANES, F32),
        jax.ShapeDtypeStruct((B, T, D_MODEL), BF16), jax.ShapeDtypeStruct((B, T, D_MODEL), BF16))
    out_specs = (
        row(N_QR), row(N_KR), row(N_VR), row(N_GR),
        pl.BlockSpec((1, NSA_HEADS, tm, NSA_HD), lambda b, t: (b, 0, t, 0)),
        pl.BlockSpec((2, 1, G, tm // CMP_STRIDE, CMP_STRIDE * NSA_HD), lambda b, t: (0, b, 0, t, 0)),
        grp(LANES), grp_t(NSA_VT_ROWS), grp(NSA_HD), grp_t(NSA_VT_ROWS),
        grp_t(LANES),
        row(D_MODEL), row(D_MODEL))
    out_bytes = tm * (2 * (N_QR + N_KR + N_VR + N_GR + N_QN + 2 * D_MODEL) + 2 * G * (3 * NSA_HD + 3 * LANES)
                      + 4 * N_GATE)
    vmem = 2 * out_bytes + 2 * tm * D * 4 + 2 * D * P_END + tm * D * 2 + 10 * MIB
    return pl.pallas_call(
        _inproj_kernel,
        grid=(B, nt),
        in_specs=[pl.BlockSpec((1, tm, D), lambda b, t: (b, t, 0)),
                  pl.BlockSpec((1, D), lambda b, t: (0, 0)),
                  pl.BlockSpec((tm, RET_DK // 2), lambda b, t: (t, 0)),
                  pl.BlockSpec((tm, RET_DK // 2), lambda b, t: (t, 0)),
                  pl.BlockSpec((tm, LANES), lambda b, t: (t, 0)),
                  pl.BlockSpec((D, P_END), lambda b, t: (0, 0), pipeline_mode=pl.Buffered(1))],
        out_specs=out_specs,
        out_shape=out_shape,
        scratch_shapes=[pltpu.VMEM((tm, D), BF16), pltpu.VMEM((2, tm, G * NSA_HD), F32)],
        compiler_params=pltpu.CompilerParams(vmem_limit_bytes=int(vmem)),
        name="in_projection",
    )(x, g.reshape(1, D), cos, sin, feat, w_packed)


def _retention_kernel(q_ref, k_ref, v_ref, g_ref, dmask_ref, xi_ref, zeta_ref, gch_ref, o_ref):
    C = RET_BLOCK
    nc = q_ref.shape[1] // C
    dmask = dmask_ref[0]
    xi = xi_ref[0]
    zeta = zeta_ref[0]
    gch = gch_ref[0, :, 0:1]
    r = None
    for c in range(nc):
        rows = slice(c * C, (c + 1) * C)
        qc = q_ref[0, rows, :]
        kc = k_ref[0, rows, :]
        vc = v_ref[0, rows, :]
        if c < nc - 1:
            kz = (kc.astype(F32) * zeta).astype(BF16)
            kv = lax.dot_general(kz, vc, (((0,), (0,)), ((), ())), preferred_element_type=F32)
            r_next = kv if c == 0 else gch * r + kv
        s = _dot_nt(qc, kc) * dmask
        o = _dot(s.astype(BF16), vc)
        if c > 0:
            o = o + _dot(qc, r.astype(BF16)) * xi
        if c < nc - 1:
            r = r_next
        mu = jnp.mean(o, -1, keepdims=True)
        d = o - mu
        var = jnp.mean(d * d, -1, keepdims=True)
        on = d * lax.rsqrt(var + GN_EPS)
        o_ref[0, rows, :] = (g_ref[0, rows, :].astype(F32) * on).astype(BF16)


def _retention(qr, kr, vr, gr):
    B, T, _ = qr.shape
    C = RET_BLOCK
    lg = jnp.log(1.0 - 2.0 ** (-5.0 - jnp.arange(RET_HEADS, dtype=F32)))
    n = jnp.arange(C, dtype=F32)
    diff = n[:, None] - n[None, :]
    dmask = jnp.where(diff >= 0, jnp.exp(jnp.maximum(diff, 0.0)[None] * lg[:, None, None]), 0.0)
    xi = jnp.exp((n + 1.0)[None] * lg[:, None])[:, :, None]
    zeta = jnp.exp((C - 1.0 - n)[None] * lg[:, None])[:, :, None]
    gch = jnp.broadcast_to(jnp.exp(C * lg)[:, None, None], (RET_HEADS, 1, LANES))
    qk_spec = pl.BlockSpec((1, T, RET_DK), lambda b, h: (b, 0, h))
    v_spec = pl.BlockSpec((1, T, RET_DV), lambda b, h: (b, 0, h))
    return pl.pallas_call(
        _retention_kernel,
        grid=(B, RET_HEADS),
        in_specs=[qk_spec, qk_spec, v_spec, v_spec,
                  pl.BlockSpec((1, C, C), lambda b, h: (h, 0, 0)),
                  pl.BlockSpec((1, C, 1), lambda b, h: (h, 0, 0)),
                  pl.BlockSpec((1, C, 1), lambda b, h: (h, 0, 0)),
                  pl.BlockSpec((1, 1, LANES), lambda b, h: (h, 0, 0))],
        out_specs=v_spec,
        out_shape=jax.ShapeDtypeStruct((B, T, RET_HEADS * RET_DV), BF16),
        compiler_params=pltpu.CompilerParams(vmem_limit_bytes=32 * MIB),
        name="retention",
    )(qr, kr, vr, gr, dmask, xi, zeta, gch)


def _gelu_tanh(x):
    return 0.5 * x * (1.0 + jnp.tanh(math.sqrt(2.0 / math.pi) * (x + 0.044715 * (x * x * x))))


def _compress_kernel(x_ref, pk_ref, pv_ref, w1k_ref, w2k_ref, w1v_ref, w2vt_ref, ko_ref, vto_ref):
    half = CMP_STRIDE * NSA_HD

    def hidden(s, g, p_ref, w1_ref):
        x = x_ref[s, 0, g].astype(F32)
        a = _dot((x + p_ref[0:1, :]).astype(BF16), w1_ref[0:half, :])
        b = _dot((x + p_ref[1:2, :]).astype(BF16), w1_ref[half:2 * half, :])
        pre = a + pltpu.roll(b, b.shape[0] - 1, 0)
        return _gelu_tanh(pre).astype(BF16)

    for g in range(NSA_KV_GROUPS):
        ko_ref[0, g] = _dot(hidden(0, g, pk_ref, w1k_ref), w2k_ref[...]).astype(BF16)
        vto_ref[0, g] = _dot_nt(w2vt_ref[...], hidden(1, g, pv_ref, w1v_ref)).astype(BF16)


def _compress(kvc16, pos_k, pos_v, w1k, w2k, w1v, w2v):
    _, B, G, NC, F = kvc16.shape
    full = lambda a: pl.BlockSpec(a.shape, lambda b: (0,) * a.ndim)
    pk = pos_k.reshape(2, F)
    pv = pos_v.reshape(2, F)
    w = [a.astype(BF16) for a in (w1k, w2k, w1v, w2v.T)]
    return pl.pallas_call(
        _compress_kernel,
        grid=(B,),
        in_specs=[pl.BlockSpec((2, 1, G, NC, F), lambda b: (0, b, 0, 0, 0)), full(pk), full(pv)]
        + [full(a) for a in w],
        out_specs=(pl.BlockSpec((1, G, NC, NSA_HD), lambda b: (b, 0, 0, 0)),
                   pl.BlockSpec((1, G, NSA_HD, NC), lambda b: (b, 0, 0, 0))),
        out_shape=(jax.ShapeDtypeStruct((B, G, NC, NSA_HD), BF16), jax.ShapeDtypeStruct((B, G, NSA_HD, NC), BF16)),
        compiler_params=pltpu.CompilerParams(vmem_limit_bytes=32 * MIB),
        name="nsa_compress",
    )(kvc16, pk, pv, *w)


def _bias_kernel(tab_ref, nb_ref, cb_ref, near_o, cmp_o):
    h = pl.program_id(0)
    far = tab_ref[REL_BUCKETS - 1, h]

    def build(bk):
        acc = jnp.zeros(bk.shape, F32)
        for k in range(REL_BUCKETS - 1):
            acc = jnp.where(bk == k, (tab_ref[k, h] - far) * LOG2E, acc)
        return jnp.where(bk == MASKED_CODE, NEG_INF, acc)

    near_o[0] = build(nb_ref[...])
    cmp_o[0] = build(cb_ref[...])


def _t5_bucket(dist):
    dist = jnp.maximum(dist, 0)
    max_exact = REL_BUCKETS // 2
    large = max_exact + (jnp.log(jnp.maximum(dist, 1).astype(F32) / max_exact)
                         / math.log(REL_MAX_DIST / max_exact) * (REL_BUCKETS - max_exact)).astype(jnp.int32)
    large = jnp.minimum(large, REL_BUCKETS - 1)
    return jnp.where(dist < max_exact, dist, large)


def _bias_tables(rel_bias, T, tq):
    nc = T // CMP_STRIDE
    i = jnp.arange(tq, dtype=jnp.int32)
    d = jnp.asarray(NEAR_OFFSETS, jnp.int32)[:, None, None] * tq + i[None, None, :] - i[None, :, None]
    near_b = jnp.where((d < 0) | (d >= WINDOW), MASKED_CODE, _t5_bucket(d))
    near_b = jnp.concatenate([near_b, jnp.full((1, tq, tq), MASKED_CODE, jnp.int32)], axis=0)
    cmp_end = jnp.arange(nc, dtype=jnp.int32) * CMP_STRIDE + CMP_LEN - 1
    cmp_b = _t5_bucket(jnp.arange(T, dtype=jnp.int32)[None, :] - cmp_end[:, None])
    return pl.pallas_call(
        _bias_kernel,
        grid=(NSA_HEADS,),
        in_specs=[pl.BlockSpec(memory_space=pltpu.SMEM),
                  pl.BlockSpec(near_b.shape, lambda h: (0, 0, 0)),
                  pl.BlockSpec(cmp_b.shape, lambda h: (0, 0))],
        out_specs=(pl.BlockSpec((1,) + near_b.shape, lambda h: (h, 0, 0, 0)),
                   pl.BlockSpec((1,) + cmp_b.shape, lambda h: (h, 0, 0))),
        out_shape=(jax.ShapeDtypeStruct((NSA_HEADS,) + near_b.shape, F32),
                   jax.ShapeDtypeStruct((NSA_HEADS,) + cmp_b.shape, F32)),
        name="rel_bias_tables",
    )(rel_bias, near_b, cmp_b)


def _nsa_kernel(q_ref, kc_ref, vct_ref, ks_ref, vst_ref, kw_ref, vwt_ref, gt_ref,
                near_ref, cb_ref, ovl_ref, o_ref,
                qa_scr, sa_scr, sb_scr, cma_scr, cmb_scr, ms_scr, mw_scr, accs_scr, accw_scr, *, tq, sel_k):
    HG = NSA_HG
    R = HG * tq
    NC = kc_ref.shape[2]
    NS = ovl_ref.shape[0]
    UW = tq
    ti = pl.program_id(2)
    s0 = ti * tq

    def head(x, h):
        return x[:, h * tq:(h + 1) * tq]

    def keys(ref, u, n=1):
        return ref[0, 0, pl.ds(pl.multiple_of(u * UW, UW), n * UW), :]

    def values_t(ref, u, n=1):
        return ref[0, 0, :, pl.ds(pl.multiple_of(u * UW, UW), n * UW)]

    cmax = lambda s_ref: cma_scr if s_ref is sa_scr else cmb_scr

    def score(s_ref, k_ref, u, n, queries, entries=None):
        def produce(h):
            s = _dot_nt(keys(k_ref, u, n), queries(h))
            if entries is not None:
                s = jnp.concatenate([s[i * UW:(i + 1) * UW] + near_ref[h, e] for i, e in enumerate(entries)], axis=0)
            s_ref[0:n * UW, h * tq:(h + 1) * tq] = s
            groups = [s[r * SUBLANES:(r + 1) * SUBLANES] for r in range(n * UW // SUBLANES)]
            cmax(s_ref)[:, h * tq:(h + 1) * tq] = functools.reduce(jnp.maximum, groups)
        return produce

    def fold(s_ref, n, m_ref, vt_ref, u, acc_ref):
        def consume(h):
            c = slice(h * tq, (h + 1) * tq)
            m_old = m_ref[0:1, c]
            m_new = jnp.maximum(m_old, jnp.max(cmax(s_ref)[:, c], axis=0, keepdims=True))
            part = [_dot(values_t(vt_ref, u + i), jnp.exp2((s_ref[i * UW:(i + 1) * UW, c] - m_new).astype(BF16)))
                    for i in range(n)]
            acc_ref[:, c] = jnp.exp2(m_old - m_new) * acc_ref[:, c] + functools.reduce(jnp.add, part)
            m_ref[:, c] = jnp.broadcast_to(m_new, (SUBLANES, tq))
        return consume

    def stage(produce=None, consume=None):
        for h in range(HG):
            if produce is not None:
                produce(h)
            if consume is not None:
                consume(h)

    q_head = lambda h: q_ref[0, 0, h]
    qa_head = lambda h: qa_scr[h * tq:(h + 1) * tq, :]

    def normalized(acc_ref):
        acc = acc_ref[...]
        return acc[:NSA_HD] / acc[NSA_HD:NSA_HD + 1]

    for ref in (ms_scr, mw_scr):
        ref[...] = jnp.full(ref.shape, NEG_INF, F32)
    for ref in (accs_scr, accw_scr):
        ref[...] = jnp.zeros_like(ref)

    masked = len(NEAR_OFFSETS)
    w0 = jnp.maximum(ti - 2, 0)

    def window_entry(slot):
        back = jnp.minimum(ti, 2) - slot
        return jnp.where(back >= 0, back, masked)

    win_a = (sa_scr, 2, mw_scr, vwt_ref, w0, accw_scr)
    win_b = (sb_scr, 1, mw_scr, vwt_ref, w0 + 2, accw_scr)
    window_a = score(sa_scr, kw_ref, w0, 2, q_head, (window_entry(0), window_entry(1)))
    window_b = score(sb_scr, kw_ref, w0 + 2, 1, q_head, (window_entry(2),))

    stage(window_a)

    pos_c = s0 + lax.broadcasted_iota(jnp.int32, (NC, tq), 1)
    cend = lax.broadcasted_iota(jnp.int32, (NC, tq), 0) * CMP_STRIDE + (CMP_LEN - 1)
    valid_c = cend <= pos_c
    valid_f = valid_c.astype(F32)
    lc = [_dot_nt(kc_ref[0, 0], q_head(h)) + cb_ref[h] for h in range(HG)]
    pc = []
    for h in range(HG):
        z = jnp.where(valid_c, lc[h], NEG_INF)
        e = jnp.exp2(z - jnp.max(z, 0, keepdims=True)) * valid_f
        pc.append(e / jnp.maximum(jnp.sum(e, 0, keepdims=True), 1e-30))
    oc = [_dot(vct_ref[0, 0], pc[h].astype(BF16)) for h in range(HG)]
    psum = functools.reduce(jnp.add, pc)

    p_hi = psum.astype(BF16)
    r1 = psum - p_hi.astype(F32)
    p_mid = r1.astype(BF16)
    p_lo = (r1 - p_mid.astype(F32)).astype(BF16)
    ovl = ovl_ref[...]
    imp = _dot(ovl, p_hi) + _dot(ovl, p_mid) + _dot(ovl, p_lo)
    jrow = lax.broadcasted_iota(jnp.int32, (NS, tq), 0)
    posl = s0 + lax.broadcasted_iota(jnp.int32, (NS, tq), 1)
    cur = lax.shift_right_logical(posl, int(math.log2(SEL_BLOCK)))
    forced = (jrow == 0) | (jrow == cur) | (jrow == cur - 1)
    started = jrow * SEL_BLOCK <= posl
    imp = jnp.where(forced, FORCED_SCORE, imp)
    imp = jnp.where(started, imp, NEG_INF)
    sel = jnp.zeros((NS, tq), F32)
    consume_a = fold(*win_a)
    for it in range(max(sel_k, HG)):
        if it < sel_k:
            mx = jnp.max(imp, axis=0, keepdims=True)
            first = jnp.min(jnp.where(imp == mx, jrow, NS), axis=0, keepdims=True)
            hit = jrow == first
            sel = jnp.where(hit, 1.0, sel)
            imp = jnp.where(hit, PICKED, imp)
        if it < HG:
            window_b(it)
            consume_a(it)
    notsel = 1.0 - jnp.where(started, sel, 0.0)
    notsel = jnp.concatenate([notsel, jnp.zeros((LANES - NS, tq), F32)], axis=0)
    notsel = pltpu.roll(notsel.T, NSA_HD, 1).astype(BF16)
    for hg in range(HG):
        qa_scr[hg * tq:(hg + 1) * tq, :NSA_HD] = q_ref[0, 0, hg]
        qa_scr[hg * tq:(hg + 1) * tq, NSA_HD:] = notsel[:, NSA_HD:]

    prev_u = jnp.where(ti >= 1, ti - 1, 1)
    sel_job = lambda s_ref, u, n: (s_ref, n, ms_scr, vst_ref, u, accs_scr)
    stage(score(sa_scr, ks_ref, ti, 1, qa_head, (0,)), fold(*win_b))
    stage(score(sb_scr, ks_ref, prev_u, 1, qa_head, (jnp.where(ti >= 1, 1, masked),)), fold(*sel_job(sa_scr, ti, 1)))
    stage(None, fold(*sel_job(sb_scr, prev_u, 1)))

    n_plain = jnp.maximum(ti - 1, 0)
    n_quads = lax.shift_right_logical(n_plain, 2)

    def plain_run(first, count):
        bufs = (sa_scr, sb_scr)
        for i in range(count + 1):
            produce = score(bufs[i % 2], ks_ref, first + i, 1, qa_head) if i < count else None
            consume = fold(*sel_job(bufs[(i - 1) % 2], first + i - 1, 1)) if i > 0 else None
            stage(produce, consume)

    def plain_quad(j, carry):
        plain_run(4 * j, 4)
        return carry

    lax.fori_loop(0, n_quads, plain_quad, 0)
    left = n_plain - 4 * n_quads

    @pl.when(left >= 2)
    def _():
        plain_run(4 * n_quads, 2)

    @pl.when(lax.rem(left, 2) == 1)
    def _():
        stage(score(sa_scr, ks_ref, n_plain - 1, 1, qa_head))
        stage(None, fold(*sel_job(sa_scr, n_plain - 1, 1)))

    gt = gt_ref[0, 0]
    o_s = normalized(accs_scr)
    o_w = normalized(accw_scr)

    def gated(h):
        return (gt[3 * h:3 * h + 1] * oc[h] + gt[3 * h + 1:3 * h + 2] * head(o_s, h)
                + gt[3 * h + 2:3 * h + 3] * head(o_w, h))

    pairs = [jnp.concatenate([gated(2 * j), gated(2 * j + 1)], axis=0).T for j in range(HG // 2)]
    o_ref[0] = jnp.concatenate(pairs, axis=1).astype(BF16)


def _nsa_attention(qn, kc, vct, ks, vst, kw, vwt, gates_t, near, cmpb):
    B, G, HG, T, hd = qn.shape
    tq = TQ
    assert T % tq == 0 and T >= 3 * tq and WINDOW == 2 * tq and tq >= 2 * REL_MAX_DIST and tq % LANES == 0
    nq = T // tq
    NC = kc.shape[2]
    NS = T // SEL_BLOCK
    sel_k = min(SEL_TOPK, NS)
    n = np.arange(NC)
    j = np.arange(NS)
    ovl = ((n[None, :] * CMP_STRIDE < (j[:, None] + 1) * SEL_BLOCK)
           & (n[None, :] * CMP_STRIDE + CMP_LEN - 1 >= j[:, None] * SEL_BLOCK) & (n[None, :] < NC - 1))
    ovl = jnp.asarray(ovl, BF16)
    seq = lambda rows, width: pl.BlockSpec((1, 1, rows, width), lambda g, b, t: (b, g, 0, 0))
    R = HG * tq
    W = HG * hd
    return pl.pallas_call(
        functools.partial(_nsa_kernel, tq=tq, sel_k=sel_k),
        grid=(G, B, nq),
        in_specs=[pl.BlockSpec((1, 1, HG, tq, hd), lambda g, b, t: (b, g, 0, t, 0)),
                  seq(NC, hd), seq(hd, NC), seq(T, LANES), seq(NSA_VT_ROWS, T), seq(T, hd), seq(NSA_VT_ROWS, T),
                  pl.BlockSpec((1, 1, LANES, tq), lambda g, b, t: (b, g, 0, t)),
                  pl.BlockSpec((HG, len(NEAR_OFFSETS) + 1, tq, tq), lambda g, b, t: (g, 0, 0, 0),
                               pipeline_mode=pl.Buffered(1)),
                  pl.BlockSpec((HG, NC, tq), lambda g, b, t: (g, 0, t)),
                  pl.BlockSpec(ovl.shape, lambda g, b, t: (0, 0))],
        out_specs=pl.BlockSpec((1, tq, W), lambda g, b, t: (b, t, g)),
        out_shape=jax.ShapeDtypeStruct((B, T, G * W), BF16),
        scratch_shapes=[pltpu.VMEM((R, LANES), BF16),
                        pltpu.VMEM((2 * tq, R), F32), pltpu.VMEM((2 * tq, R), F32),
                        pltpu.VMEM((SUBLANES, R), F32), pltpu.VMEM((SUBLANES, R), F32),
                        pltpu.VMEM((SUBLANES, R), F32), pltpu.VMEM((SUBLANES, R), F32),
                        pltpu.VMEM((NSA_VT_ROWS, R), F32), pltpu.VMEM((NSA_VT_ROWS, R), F32)],
        compiler_params=pltpu.CompilerParams(vmem_limit_bytes=NSA_VMEM_LIMIT),
        name="nsa_attention",
    )(qn, kc, vct, ks, vst, kw, vwt, gates_t, near, cmpb, ovl)


def _merge_kernel(x_ref, oret_ref, on_ref, ma_ref, mb_ref, wr_ref, wn_ref, wo_ref, o_ref):
    y_ret = _dot(oret_ref[...], wr_ref[...])
    y_nsa = _dot(on_ref[...], wn_ref[...])
    mixed = ma_ref[...].astype(F32) * y_ret + mb_ref[...].astype(F32) * y_nsa
    o_ref[...] = x_ref[...] + _dot(mixed.astype(BF16), wo_ref[...])


def _merge(x2, oret, on, ma, mb, w_o_ret, w_o_nsa, w_out):
    M, D = x2.shape
    tm = min(TM_MERGE, M)
    row = lambda width: pl.BlockSpec((tm, width), lambda i: (i, 0))
    res = lambda a: pl.BlockSpec(a.shape, lambda i: (0, 0), pipeline_mode=pl.Buffered(1))
    w = [a.astype(BF16) for a in (w_o_ret, w_o_nsa, w_out)]
    return pl.pallas_call(
        _merge_kernel,
        grid=(M // tm,),
        in_specs=[row(D), row(oret.shape[1]), row(D), row(D), row(D)] + [res(a) for a in w],
        out_specs=row(D),
        out_shape=jax.ShapeDtypeStruct((M, D), F32),
        compiler_params=pltpu.CompilerParams(vmem_limit_bytes=48 * MIB),
        name="merge_out_projection",
    )(x2, oret, on, ma, mb, *w)


def _rms(x, g):
    return x * lax.rsqrt(jnp.mean(x * x, -1, keepdims=True) + EPS) * g


def _ffn_kernel(x_ref, g_ref, wi_ref, wo_ref, gf_ref, o_ref, h_scr, acc_scr, *, final_norm):
    x = x_ref[...]
    h_scr[...] = _rms(x, g_ref[...]).astype(BF16)
    acc_scr[...] = x
    cw = FFN_HIDDEN // 2
    for c in range(2):
        a = _dot(h_scr[...], wi_ref[:, c * cw:(c + 1) * cw])
        b = _dot(h_scr[...], wi_ref[:, FFN_HIDDEN + c * cw:FFN_HIDDEN + (c + 1) * cw])
        acc_scr[...] += _dot((_silu(a) * b).astype(BF16), wo_ref[c * cw:(c + 1) * cw, :])
    y = acc_scr[...]
    o_ref[...] = _rms(y, gf_ref[...]) if final_norm else y


def _ffn(x2, g, w_in, w_out, g_final, final_norm):
    M, D = x2.shape
    tm = min(TM_MERGE, M)
    row = pl.BlockSpec((tm, D), lambda i: (i, 0))
    vec = pl.BlockSpec((1, D), lambda i: (0, 0))
    res = lambda a: pl.BlockSpec(a.shape, lambda i: (0, 0), pipeline_mode=pl.Buffered(1))
    wi, wo = w_in.astype(BF16), w_out.astype(BF16)
    return pl.pallas_call(
        functools.partial(_ffn_kernel, final_norm=final_norm),
        grid=(M // tm,),
        in_specs=[row, vec, res(wi), res(wo), vec],
        out_specs=row,
        out_shape=jax.ShapeDtypeStruct((M, D), F32),
        scratch_shapes=[pltpu.VMEM((tm, D), BF16), pltpu.VMEM((tm, D), F32)],
        compiler_params=pltpu.CompilerParams(vmem_limit_bytes=52 * MIB),
        name="swiglu_ffn",
    )(x2, g.reshape(1, D), wi, wo, g_final.reshape(1, D))


def kernel(x, norm_mix_g, w_in, cmp_pos_k, cmp_pos_v, cmp_w1_k, cmp_w2_k, cmp_w1_v, cmp_w2_v, w_o_ret, w_o_nsa,
           w_out, norm_ffn_g, w_ffn_in, w_ffn_out, rel_bias, norm_final_g):
    B, T, D = x.shape
    depth = w_in.shape[0]
    G, HG = NSA_KV_GROUPS, NSA_HG
    half = RET_DK // 2
    freqs = ROPE_BASE ** (-jnp.arange(half, dtype=F32) / half)
    ang = jnp.arange(T, dtype=jnp.int32).astype(F32)[:, None] * freqs
    cos, sin = jnp.cos(ang), jnp.sin(ang)
    near, cmpb = _bias_tables(rel_bias, T, TQ)
    for i in range(depth):
        qr, kr, vr, gr, qn, kvc16, ks, vst, kw, vwt, gates_t, ma, mb = _in_projection(
            x, norm_mix_g[i], _pack_w_in(w_in[i]), cos, sin)
        o_ret = _retention(qr, kr, vr, gr)
        kc, vct = _compress(kvc16,cmp_pos_k[i], cmp_pos_v[i], cmp_w1_k[i], cmp_w2_k[i], cmp_w1_v[i], cmp_w2_v[i])
        o_n = _nsa_attention(qn.reshape(B, G, HG, T, NSA_HD), kc, vct, ks, vst, kw, vwt, gates_t, near, cmpb)
        o_n = o_n.reshape(B * T, NSA_HEADS * NSA_HD)
        x2 = _merge(x.reshape(B * T, D), o_ret.reshape(B * T, -1), o_n, ma.reshape(B * T, D), mb.reshape(B * T, D),
                    w_o_ret[i], w_o_nsa[i], w_out[i])
        x2 = _ffn(x2, norm_ffn_g[i], w_ffn_in[i], w_ffn_out[i], norm_final_g, final_norm=(i == depth - 1))
        x = x2.reshape(B, T, D)
    return x
```

```python
import functools
import math

import jax
import jax.numpy as jnp
import numpy as np
from jax import lax
from jax.experimental import pallas as pl
from jax.experimental.pallas import tpu as pltpu

F32 = jnp.float32
BF16 = jnp.bfloat16

D_MODEL = 1024
RET_HEADS = 4
RET_DK = 256
RET_DV = 512
RET_BLOCK = 256
ROPE_BASE = 10000.0
GN_EPS = 1e-5
NSA_HEADS = 16
NSA_KV_GROUPS = 2
NSA_HG = NSA_HEADS // NSA_KV_GROUPS
NSA_HD = 64
CMP_LEN = 32
CMP_STRIDE = 16
CMP_HIDDEN = 256
SEL_BLOCK = 64
SEL_TOPK = 8
WINDOW = 512
FORCED_SCORE = 1e4
REL_BUCKETS = 32
REL_MAX_DIST = 128
FFN_HIDDEN = -(-8 * D_MODEL // (3 * 256)) * 256
EPS = 1e-6
NEG_INF = -1e30
PICKED = -3e38
MASKED_CODE = REL_BUCKETS
NEAR_OFFSETS = (0, 1, 2)
LOG2E = math.log2(math.e)

LANES = 128
SUBLANES = 8
MIB = 1024 * 1024
NSA_VMEM_LIMIT = 56 * MIB
NSA_VT_ROWS = NSA_HD + 16

_SEG = np.cumsum([0, RET_HEADS * RET_DK, RET_HEADS * RET_DK, RET_HEADS * RET_DV, RET_HEADS * RET_DV,
                  NSA_HEADS * NSA_HD, 6 * NSA_KV_GROUPS * NSA_HD, 3 * NSA_HEADS, D_MODEL, D_MODEL])
N_QR, N_KR, N_VR, N_GR = RET_HEADS * RET_DK, RET_HEADS * RET_DK, RET_HEADS * RET_DV, RET_HEADS * RET_DV
N_QN, N_KV = NSA_HEADS * NSA_HD, 6 * NSA_KV_GROUPS * NSA_HD
N_GATE = NSA_KV_GROUPS * LANES
_P = np.cumsum([0, N_QR, N_KR, N_VR, N_GR, N_QN, N_KV, N_GATE, D_MODEL, D_MODEL])
P_QR, P_KR, P_VR, P_GR, P_QN, P_KV, P_GATE, P_MA, P_MB, P_END = [int(v) for v in _P]

TM_PROJ = 512
TM_MERGE = 512
TQ = 256
NT = (((1,), (1,)), ((), ()))


def _dot(a, b):
    return jnp.dot(a, b, preferred_element_type=F32)


def _dot_nt(a, b):
    return lax.dot_general(a, b, NT, preferred_element_type=F32)


def _sigmoid(x):
    return 1.0 / (1.0 + jnp.exp(-x))


def _silu(x):
    return x * _sigmoid(x)


def _inproj_kernel(x_ref, g_ref, cos_ref, sin_ref, feat_ref, w_ref,
                   qr_o, kr_o, vr_o, gr_o, qn_o, kvc_o, ks_o, vst_o, kw_o, vwt_o, gate_o, ma_o, mb_o, h_scr, kvc_scr):
    x = x_ref[0]
    h = x * lax.rsqrt(jnp.mean(x * x, -1, keepdims=True) + EPS) * g_ref[...]
    h_scr[...] = h.astype(BF16)
    cos = cos_ref[...]
    sin = sin_ref[...]
    half = RET_DK // 2

    def proj(lo, width):
        return _dot(h_scr[...], w_ref[:, lo:lo + width])

    y32 = proj(P_KV, N_KV)
    y = y32.astype(BF16)
    y_t = y32.T.astype(BF16)
    tm = y.shape[0]
    ones_row = (lax.broadcasted_iota(jnp.int32, (NSA_VT_ROWS - NSA_HD, tm), 0) == 0).astype(BF16)
    for g in range(NSA_KV_GROUPS):
        lo = lambda j: (j * NSA_KV_GROUPS + g) * NSA_HD
        ks_o[0, g, :, :NSA_HD] = y[:, lo(2):lo(2) + NSA_HD]
        ks_o[0, g, :, NSA_HD:] = feat_ref[:, NSA_HD:]
        kw_o[0, g] = y[:, lo(4):lo(4) + NSA_HD]
        for out, j in ((vst_o, 3), (vwt_o, 5)):
            out[0, g, :NSA_HD, :] = y_t[lo(j):lo(j) + NSA_HD, :]
            out[0, g, NSA_HD:, :] = ones_row
    for s in range(2):
        kvc_scr[s] = y32[:, s * LANES:(s + 1) * LANES]
        for l in range(CMP_STRIDE):
            rows = kvc_scr[s, pl.ds(l, tm // CMP_STRIDE, stride=CMP_STRIDE), :].astype(BF16)
            for g in range(NSA_KV_GROUPS):
                kvc_o[s, 0, g, :, l * NSA_HD:(l + 1) * NSA_HD] = rows[:, g * NSA_HD:(g + 1) * NSA_HD]
    gates_t = _sigmoid(proj(P_GATE, N_GATE)).T
    for g in range(NSA_KV_GROUPS):
        gate_o[0, g] = gates_t[g * LANES:(g + 1) * LANES, :]
    cw = 512
    per = cw // NSA_HD
    for c in range(N_QN // cw):
        y = (proj(P_QN + c * cw, cw) * (NSA_HD ** -0.5 * LOG2E)).astype(BF16)
        for j in range(per):
            qn_o[0, c * per + j] = y[:, j * NSA_HD:(j + 1) * NSA_HD]
    for hd in range(RET_HEADS):
        for out, base, scale in ((qr_o, P_QR, 1.0), (kr_o, P_KR, RET_DK ** -0.5)):
            y = proj(base + hd * RET_DK, RET_DK)
            x1, x2 = y[:, :half], y[:, half:]
            out[0, :, hd * RET_DK:hd * RET_DK + half] = ((x1 * cos - x2 * sin) * scale).astype(BF16)
            out[0, :, hd * RET_DK + half:(hd + 1) * RET_DK] = ((x1 * sin + x2 * cos) * scale).astype(BF16)
    for c in range(N_VR // cw):
        vr_o[0, :, c * cw:(c + 1) * cw] = proj(P_VR + c * cw, cw).astype(BF16)
        gr_o[0, :, c * cw:(c + 1) * cw] = _silu(proj(P_GR + c * cw, cw)).astype(BF16)
    for c in range(D_MODEL // cw):
        ma_o[0, :, c * cw:(c + 1) * cw] = _sigmoid(proj(P_MA + c * cw, cw)).astype(BF16)
        mb_o[0, :, c * cw:(c + 1) * cw] = _sigmoid(proj(P_MB + c * cw, cw)).astype(BF16)


def _pack_w_in(w):
    s = [int(v) for v in _SEG]
    gate = w[:, s[6]:s[7]].reshape(D_MODEL, NSA_KV_GROUPS, 3 * NSA_HG)
    gate = jnp.pad(gate, ((0, 0), (0, 0), (0, LANES - 3 * NSA_HG))).reshape(D_MODEL, N_GATE)
    return jnp.concatenate([w[:, s[0]:s[6]], gate, w[:, s[7]:s[9]]], axis=1).astype(BF16)


def _in_projection(x, g, w_packed, cos, sin):
    B, T, D = x.shape
    G = NSA_KV_GROUPS
    tm = min(TM_PROJ, T)
    nt = T // tm
    assert T // SEL_BLOCK <= LANES - NSA_HD
    own_block = (np.arange(T)[:, None] // SEL_BLOCK) == (np.arange(LANES)[None, :] - NSA_HD)
    feat = jnp.asarray(np.where(own_block, NEG_INF, 0.0), BF16)
    row = lambda width: pl.BlockSpec((1, tm, width), lambda b, t: (b, t, 0))
    grp = lambda width: pl.BlockSpec((1, G, tm, width), lambda b, t: (b, 0, t, 0))
    grp_t = lambda rows: pl.BlockSpec((1, G, rows, tm), lambda b, t: (b, 0, 0, t))
    grp_shape = lambda width: jax.ShapeDtypeStruct((B, G, T, width), BF16)
    grp_t_shape = lambda rows, dt: jax.ShapeDtypeStruct((B, G, rows, T), dt)
    out_shape = (
        jax.ShapeDtypeStruct((B, T, N_QR), BF16), jax.ShapeDtypeStruct((B, T, N_KR), BF16),
        jax.ShapeDtypeStruct((B, T, N_VR), BF16), jax.ShapeDtypeStruct((B, T, N_GR), BF16),
        jax.ShapeDtypeStruct((B, NSA_HEADS, T, NSA_HD), BF16),
        jax.ShapeDtypeStruct((2, B, G, T // CMP_STRIDE, CMP_STRIDE * NSA_HD), BF16),
        grp_shape(LANES), grp_t_shape(NSA_VT_ROWS, BF16), grp_shape(NSA_HD), grp_t_shape(NSA_VT_ROWS, BF16),
        grp_t_shape(LANES, F32),
        jax.ShapeDtypeStruct((B, T, D_MODEL), BF16), jax.ShapeDtypeStruct((B, T, D_MODEL), BF16))
    out_specs = (
        row(N_QR), row(N_KR), row(N_VR), row(N_GR),
        pl.BlockSpec((1, NSA_HEADS, tm, NSA_HD), lambda b, t: (b, 0, t, 0)),
        pl.BlockSpec((2, 1, G, tm // CMP_STRIDE, CMP_STRIDE * NSA_HD), lambda b, t: (0, b, 0, t, 0)),
        grp(LANES), grp_t(NSA_VT_ROWS), grp(NSA_HD), grp_t(NSA_VT_ROWS),
        grp_t(LANES),
        row(D_MODEL), row(D_MODEL))
    out_bytes = tm * (2 * (N_QR + N_KR + N_VR + N_GR + N_QN + 2 * D_MODEL) + 2 * G * (3 * NSA_HD + 3 * LANES)
                      + 4 * N_GATE)
    vmem = 2 * out_bytes + 2 * tm * D * 4 + 2 * D * P_END + tm * D * 2 + 10 * MIB
    return pl.pallas_call(
        _inproj_kernel,
        grid=(B, nt),
        in_specs=[pl.BlockSpec((1, tm, D), lambda b, t: (b, t, 0)),
                  pl.BlockSpec((1, D), lambda b, t: (0, 0)),
                  pl.BlockSpec((tm, RET_DK // 2), lambda b, t: (t, 0)),
                  pl.BlockSpec((tm, RET_DK // 2), lambda b, t: (t, 0)),
                  pl.BlockSpec((tm, LANES), lambda b, t: (t, 0)),
                  pl.BlockSpec((D, P_END), lambda b, t: (0, 0), pipeline_mode=pl.Buffered(1))],
        out_specs=out_specs,
        out_shape=out_shape,
        scratch_shapes=[pltpu.VMEM((tm, D), BF16), pltpu.VMEM((2, tm, G * NSA_HD), F32)],
        compiler_params=pltpu.CompilerParams(vmem_limit_bytes=int(vmem)),
        name="in_projection",
    )(x, g.reshape(1, D), cos, sin, feat, w_packed)


def _retention_kernel(q_ref, k_ref, v_ref, g_ref, dmask_ref, xi_ref, zeta_ref, gch_ref, o_ref):
    C = RET_BLOCK
    nc = q_ref.shape[1] // C
    dmask = dmask_ref[0]
    xi = xi_ref[0]
    zeta = zeta_ref[0]
    gch = gch_ref[0, :, 0:1]
    r = None
    for c in range(nc):
        rows = slice(c * C, (c + 1) * C)
        qc = q_ref[0, rows, :]
        kc = k_ref[0, rows, :]
        vc = v_ref[0, rows, :]
        if c < nc - 1:
            kz = (kc.astype(F32) * zeta).astype(BF16)
            kv = lax.dot_general(kz, vc, (((0,), (0,)), ((), ())), preferred_element_type=F32)
            r_next = kv if c == 0 else gch * r + kv
        s = _dot_nt(qc, kc) * dmask
        o = _dot(s.astype(BF16), vc)
        if c > 0:
            o = o + _dot(qc, r.astype(BF16)) * xi
        if c < nc - 1:
            r = r_next
        mu = jnp.mean(o, -1, keepdims=True)
        d = o - mu
        var = jnp.mean(d * d, -1, keepdims=True)
        on = d * lax.rsqrt(var + GN_EPS)
        o_ref[0, rows, :] = (g_ref[0, rows, :].astype(F32) * on).astype(BF16)


def _retention(qr, kr, vr, gr):
    B, T, _ = qr.shape
    C = RET_BLOCK
    lg = jnp.log(1.0 - 2.0 ** (-5.0 - jnp.arange(RET_HEADS, dtype=F32)))
    n = jnp.arange(C, dtype=F32)
    diff = n[:, None] - n[None, :]
    dmask = jnp.where(diff >= 0, jnp.exp(jnp.maximum(diff, 0.0)[None] * lg[:, None, None]), 0.0)
    xi = jnp.exp((n + 1.0)[None] * lg[:, None])[:, :, None]
    zeta = jnp.exp((C - 1.0 - n)[None] * lg[:, None])[:, :, None]
    gch = jnp.broadcast_to(jnp.exp(C * lg)[:, None, None], (RET_HEADS, 1, LANES))
    qk_spec = pl.BlockSpec((1, T, RET_DK), lambda b, h: (b, 0, h))
    v_spec = pl.BlockSpec((1, T, RET_DV), lambda b, h: (b, 0, h))
    return pl.pallas_call(
        _retention_kernel,
        grid=(B, RET_HEADS),
        in_specs=[qk_spec, qk_spec, v_spec, v_spec,
                  pl.BlockSpec((1, C, C), lambda b, h: (h, 0, 0)),
                  pl.BlockSpec((1, C, 1), lambda b, h: (h, 0, 0)),
                  pl.BlockSpec((1, C, 1), lambda b, h: (h, 0, 0)),
                  pl.BlockSpec((1, 1, LANES), lambda b, h: (h, 0, 0))],
        out_specs=v_spec,
        out_shape=jax.ShapeDtypeStruct((B, T, RET_HEADS * RET_DV), BF16),
        compiler_params=pltpu.CompilerParams(vmem_limit_bytes=32 * MIB),
        name="retention",
    )(qr, kr, vr, gr, dmask, xi, zeta, gch)


def _gelu_tanh(x):
    return 0.5 * x * (1.0 + jnp.tanh(math.sqrt(2.0 / math.pi) * (x + 0.044715 * (x * x * x))))


def _compress_kernel(x_ref, pk_ref, pv_ref, w1k_ref, w2k_ref, w1v_ref, w2vt_ref, ko_ref, vto_ref):
    half = CMP_STRIDE * NSA_HD

    def hidden(s, g, p_ref, w1_ref):
        x = x_ref[s, 0, g].astype(F32)
        a = _dot((x + p_ref[0:1, :]).astype(BF16), w1_ref[0:half, :])
        b = _dot((x + p_ref[1:2, :]).astype(BF16), w1_ref[half:2 * half, :])
        pre = a + pltpu.roll(b, b.shape[0] - 1, 0)
        return _gelu_tanh(pre).astype(BF16)

    for g in range(NSA_KV_GROUPS):
        ko_ref[0, g] = _dot(hidden(0, g, pk_ref, w1k_ref), w2k_ref[...]).astype(BF16)
        vto_ref[0, g] = _dot_nt(w2vt_ref[...], hidden(1, g, pv_ref, w1v_ref)).astype(BF16)


def _compress(kvc16, pos_k, pos_v, w1k, w2k, w1v, w2v):
    _, B, G, NC, F = kvc16.shape
    full = lambda a: pl.BlockSpec(a.shape, lambda b: (0,) * a.ndim)
    pk = pos_k.reshape(2, F)
    pv = pos_v.reshape(2, F)
    w = [a.astype(BF16) for a in (w1k, w2k, w1v, w2v.T)]
    return pl.pallas_call(
        _compress_kernel,
        grid=(B,),
        in_specs=[pl.BlockSpec((2, 1, G, NC, F), lambda b: (0, b, 0, 0, 0)), full(pk), full(pv)]
        + [full(a) for a in w],
        out_specs=(pl.BlockSpec((1, G, NC, NSA_HD), lambda b: (b, 0, 0, 0)),
                   pl.BlockSpec((1, G, NSA_HD, NC), lambda b: (b, 0, 0, 0))),
        out_shape=(jax.ShapeDtypeStruct((B, G, NC, NSA_HD), BF16), jax.ShapeDtypeStruct((B, G, NSA_HD, NC), BF16)),
        compiler_params=pltpu.CompilerParams(vmem_limit_bytes=32 * MIB),
        name="nsa_compress",
    )(kvc16, pk, pv, *w)


def _bias_kernel(tab_ref, nb_ref, cb_ref, near_o, cmp_o):
    h = pl.program_id(0)
    far = tab_ref[REL_BUCKETS - 1, h]

    def build(bk):
        acc = jnp.zeros(bk.shape, F32)
        for k in range(REL_BUCKETS - 1):
            acc = jnp.where(bk == k, (tab_ref[k, h] - far) * LOG2E, acc)
        return jnp.where(bk == MASKED_CODE, NEG_INF, acc)

    near_o[0] = build(nb_ref[...])
    cmp_o[0] = build(cb_ref[...])


def _t5_bucket(dist):
    dist = jnp.maximum(dist, 0)
    max_exact = REL_BUCKETS // 2
    large = max_exact + (jnp.log(jnp.maximum(dist, 1).astype(F32) / max_exact)
                         / math.log(REL_MAX_DIST / max_exact) * (REL_BUCKETS - max_exact)).astype(jnp.int32)
    large = jnp.minimum(large, REL_BUCKETS - 1)
    return jnp.where(dist < max_exact, dist, large)


def _bias_tables(rel_bias, T, tq):
    nc = T // CMP_STRIDE
    i = jnp.arange(tq, dtype=jnp.int32)
    d = jnp.asarray(NEAR_OFFSETS, jnp.int32)[:, None, None] * tq + i[None, None, :] - i[None, :, None]
    near_b = jnp.where((d < 0) | (d >= WINDOW), MASKED_CODE, _t5_bucket(d))
    near_b = jnp.concatenate([near_b, jnp.full((1, tq, tq), MASKED_CODE, jnp.int32)], axis=0)
    cmp_end = jnp.arange(nc, dtype=jnp.int32) * CMP_STRIDE + CMP_LEN - 1
    cmp_b = _t5_bucket(jnp.arange(T, dtype=jnp.int32)[None, :] - cmp_end[:, None])
    return pl.pallas_call(
        _bias_kernel,
        grid=(NSA_HEADS,),
        in_specs=[pl.BlockSpec(memory_space=pltpu.SMEM),
                  pl.BlockSpec(near_b.shape, lambda h: (0, 0, 0)),
                  pl.BlockSpec(cmp_b.shape, lambda h: (0, 0))],
        out_specs=(pl.BlockSpec((1,) + near_b.shape, lambda h: (h, 0, 0, 0)),
                   pl.BlockSpec((1,) + cmp_b.shape, lambda h: (h, 0, 0))),
        out_shape=(jax.ShapeDtypeStruct((NSA_HEADS,) + near_b.shape, F32),
                   jax.ShapeDtypeStruct((NSA_HEADS,) + cmp_b.shape, F32)),
        name="rel_bias_tables",
    )(rel_bias, near_b, cmp_b)


def _nsa_kernel(q_ref, kc_ref, vct_ref, ks_ref, vst_ref, kw_ref, vwt_ref, gt_ref,
                near_ref, cb_ref, ovl_ref, o_ref,
                qa_scr, sa_scr, sb_scr, cma_scr, cmb_scr, ms_scr, mw_scr, accs_scr, accw_scr, *, tq, sel_k):
    HG = NSA_HG
    R = HG * tq
    NC = kc_ref.shape[2]
    NS = ovl_ref.shape[0]
    UW = tq
    ti = pl.program_id(2)
    s0 = ti * tq

    def head(x, h):
        return x[:, h * tq:(h + 1) * tq]

    def keys(ref, u, n=1):
        return ref[0, 0, pl.ds(pl.multiple_of(u * UW, UW), n * UW), :]

    def values_t(ref, u, n=1):
        return ref[0, 0, :, pl.ds(pl.multiple_of(u * UW, UW), n * UW)]

    cmax = lambda s_ref: cma_scr if s_ref is sa_scr else cmb_scr

    def score(s_ref, k_ref, u, n, queries, entries=None):
        def produce(h):
            s = _dot_nt(keys(k_ref, u, n), queries(h))
            if entries is not None:
                s = jnp.concatenate([s[i * UW:(i + 1) * UW] + near_ref[h, e] for i, e in enumerate(entries)], axis=0)
            s_ref[0:n * UW, h * tq:(h + 1) * tq] = s
            groups = [s[r * SUBLANES:(r + 1) * SUBLANES] for r in range(n * UW // SUBLANES)]
            cmax(s_ref)[:, h * tq:(h + 1) * tq] = functools.reduce(jnp.maximum, groups)
        return produce

    def fold(s_ref, n, m_ref, vt_ref, u, acc_ref):
        def consume(h):
            c = slice(h * tq, (h + 1) * tq)
            m_old = m_ref[0:1, c]
            m_new = jnp.maximum(m_old, jnp.max(cmax(s_ref)[:, c], axis=0, keepdims=True))
            part = [_dot(values_t(vt_ref, u + i), jnp.exp2((s_ref[i * UW:(i + 1) * UW, c] - m_new).astype(BF16)))
                    for i in range(n)]
            acc_ref[:, c] = jnp.exp2(m_old - m_new) * acc_ref[:, c] + functools.reduce(jnp.add, part)
            m_ref[:, c] = jnp.broadcast_to(m_new, (SUBLANES, tq))
        return consume

    def stage(produce=None, consume=None):
        for h in range(HG):
            if produce is not None:
                produce(h)
            if consume is not None:
                consume(h)

    q_head = lambda h: q_ref[0, 0, h]
    qa_head = lambda h: qa_scr[h * tq:(h + 1) * tq, :]

    def normalized(acc_ref):
        acc = acc_ref[...]
        return acc[:NSA_HD] / acc[NSA_HD:NSA_HD + 1]

    for ref in (ms_scr, mw_scr):
        ref[...] = jnp.full(ref.shape, NEG_INF, F32)
    for ref in (accs_scr, accw_scr):
        ref[...] = jnp.zeros_like(ref)

    masked = len(NEAR_OFFSETS)
    w0 = jnp.maximum(ti - 2, 0)

    def window_entry(slot):
        back = jnp.minimum(ti, 2) - slot
        return jnp.where(back >= 0, back, masked)

    win_bufs = (sa_scr, sb_scr, sa_scr)
    win_slots = (1, 0, 2)
    win_score = [score(buf, kw_ref, w0 + slot, 1, q_head, (window_entry(slot),))
                 for buf, slot in zip(win_bufs, win_slots)]
    win_fold = [fold(buf, 1, mw_scr, vwt_ref, w0 + slot, accw_scr) for buf, slot in zip(win_bufs, win_slots)]

    stage(win_score[0])

    pos_c = s0 + lax.broadcasted_iota(jnp.int32, (NC, tq), 1)
    cend = lax.broadcasted_iota(jnp.int32, (NC, tq), 0) * CMP_STRIDE + (CMP_LEN - 1)
    valid_c = cend <= pos_c
    valid_f = valid_c.astype(F32)
    lc = [_dot_nt(kc_ref[0, 0], q_head(h)) + cb_ref[h] for h in range(HG)]
    pc = []
    for h in range(HG):
        z = jnp.where(valid_c, lc[h], NEG_INF)
        e = jnp.exp2(z - jnp.max(z, 0, keepdims=True)) * valid_f
        pc.append(e / jnp.maximum(jnp.sum(e, 0, keepdims=True), 1e-30))
    oc = [_dot(vct_ref[0, 0], pc[h].astype(BF16)) for h in range(HG)]
    psum = functools.reduce(jnp.add, pc)

    p_hi = psum.astype(BF16)
    r1 = psum - p_hi.astype(F32)
    p_mid = r1.astype(BF16)
    p_lo = (r1 - p_mid.astype(F32)).astype(BF16)
    ovl = ovl_ref[...]
    imp = _dot(ovl, p_hi) + _dot(ovl, p_mid) + _dot(ovl, p_lo)
    jrow = lax.broadcasted_iota(jnp.int32, (NS, tq), 0)
    posl = s0 + lax.broadcasted_iota(jnp.int32, (NS, tq), 1)
    cur = lax.shift_right_logical(posl, int(math.log2(SEL_BLOCK)))
    forced = (jrow == 0) | (jrow == cur) | (jrow == cur - 1)
    started = jrow * SEL_BLOCK <= posl
    imp = jnp.where(forced, FORCED_SCORE, imp)
    imp = jnp.where(started, imp, NEG_INF)
    sel = jnp.zeros((NS, tq), F32)
    for it in range(max(sel_k, HG)):
        if it < sel_k:
            mx = jnp.max(imp, axis=0, keepdims=True)
            first = jnp.min(jnp.where(imp == mx, jrow, NS), axis=0, keepdims=True)
            hit = jrow == first
            sel = jnp.where(hit, 1.0, sel)
            imp = jnp.where(hit, PICKED, imp)
        if it < HG:
            win_score[1](it)
            win_fold[0](it)
    stage(win_score[2], win_fold[1])
    notsel = 1.0 - jnp.where(started, sel, 0.0)
    notsel = jnp.concatenate([notsel, jnp.zeros((LANES - NS, tq), F32)], axis=0)
    notsel = pltpu.roll(notsel.T, NSA_HD, 1).astype(BF16)
    for hg in range(HG):
        qa_scr[hg * tq:(hg + 1) * tq, :NSA_HD] = q_ref[0, 0, hg]
        qa_scr[hg * tq:(hg + 1) * tq, NSA_HD:] = notsel[:, NSA_HD:]

    prev_u = jnp.where(ti >= 1, ti - 1, 1)
    sel_job = lambda s_ref, u, n: (s_ref, n, ms_scr, vst_ref, u, accs_scr)
    stage(score(sb_scr, ks_ref, ti, 1, qa_head, (0,)), win_fold[2])
    stage(score(sa_scr, ks_ref, prev_u, 1, qa_head, (jnp.where(ti >= 1, 1, masked),)), fold(*sel_job(sb_scr, ti, 1)))
    stage(None, fold(*sel_job(sa_scr, prev_u, 1)))

    n_plain = jnp.maximum(ti - 1, 0)
    n_quads = lax.shift_right_logical(n_plain, 2)

    def plain_run(first, count):
        bufs = (sa_scr, sb_scr)
        for i in range(count + 1):
            produce = score(bufs[i % 2], ks_ref, first + i, 1, qa_head) if i < count else None
            consume = fold(*sel_job(bufs[(i - 1) % 2], first + i - 1, 1)) if i > 0 else None
            stage(produce, consume)

    def plain_quad(j, carry):
        plain_run(4 * j, 4)
        return carry

    lax.fori_loop(0, n_quads, plain_quad, 0)
    left = n_plain - 4 * n_quads

    @pl.when(left >= 2)
    def _():
        plain_run(4 * n_quads, 2)

    @pl.when(lax.rem(left, 2) == 1)
    def _():
        stage(score(sa_scr, ks_ref, n_plain - 1, 1, qa_head))
        stage(None, fold(*sel_job(sa_scr, n_plain - 1, 1)))

    gt = gt_ref[0, 0]
    o_s = normalized(accs_scr)
    o_w = normalized(accw_scr)

    def gated(h):
        return (gt[3 * h:3 * h + 1] * oc[h] + gt[3 * h + 1:3 * h + 2] * head(o_s, h)
                + gt[3 * h + 2:3 * h + 3] * head(o_w, h))

    pairs = [jnp.concatenate([gated(2 * j), gated(2 * j + 1)], axis=0).T for j in range(HG // 2)]
    o_ref[0] = jnp.concatenate(pairs, axis=1).astype(BF16)


def _nsa_attention(qn, kc, vct, ks, vst, kw, vwt, gates_t, near, cmpb):
    B, G, HG, T, hd = qn.shape
    tq = TQ
    assert T % tq == 0 and T >= 3 * tq and WINDOW == 2 * tq and tq >= 2 * REL_MAX_DIST and tq % LANES == 0
    nq = T // tq
    NC = kc.shape[2]
    NS = T // SEL_BLOCK
    sel_k = min(SEL_TOPK, NS)
    n = np.arange(NC)
    j = np.arange(NS)
    ovl = ((n[None, :] * CMP_STRIDE < (j[:, None] + 1) * SEL_BLOCK)
           & (n[None, :] * CMP_STRIDE + CMP_LEN - 1 >= j[:, None] * SEL_BLOCK) & (n[None, :] < NC - 1))
    ovl = jnp.asarray(ovl, BF16)
    seq = lambda rows, width: pl.BlockSpec((1, 1, rows, width), lambda g, b, t: (b, g, 0, 0))
    R = HG * tq
    W = HG * hd
    return pl.pallas_call(
        functools.partial(_nsa_kernel, tq=tq, sel_k=sel_k),
        grid=(G, B, nq),
        in_specs=[pl.BlockSpec((1, 1, HG, tq, hd), lambda g, b, t: (b, g, 0, t, 0)),
                  seq(NC, hd), seq(hd, NC), seq(T, LANES), seq(NSA_VT_ROWS, T), seq(T, hd), seq(NSA_VT_ROWS, T),
                  pl.BlockSpec((1, 1, LANES, tq), lambda g, b, t: (b, g, 0, t)),
                  pl.BlockSpec((HG, len(NEAR_OFFSETS) + 1, tq, tq), lambda g, b, t: (g, 0, 0, 0),
                               pipeline_mode=pl.Buffered(1)),
                  pl.BlockSpec((HG, NC, tq), lambda g, b, t: (g, 0, t)),
                  pl.BlockSpec(ovl.shape, lambda g, b, t: (0, 0))],
        out_specs=pl.BlockSpec((1, tq, W), lambda g, b, t: (b, t, g)),
        out_shape=jax.ShapeDtypeStruct((B, T, G * W), BF16),
        scratch_shapes=[pltpu.VMEM((R, LANES), BF16),
                        pltpu.VMEM((2 * tq, R), F32), pltpu.VMEM((2 * tq, R), F32),
                        pltpu.VMEM((SUBLANES, R), F32), pltpu.VMEM((SUBLANES, R), F32),
                        pltpu.VMEM((SUBLANES, R), F32), pltpu.VMEM((SUBLANES, R), F32),
                        pltpu.VMEM((NSA_VT_ROWS, R), F32), pltpu.VMEM((NSA_VT_ROWS, R), F32)],
        compiler_params=pltpu.CompilerParams(vmem_limit_bytes=NSA_VMEM_LIMIT),
        name="nsa_attention",
    )(qn, kc, vct, ks, vst, kw, vwt, gates_t, near, cmpb, ovl)


def _merge_kernel(x_ref, oret_ref, on_ref, ma_ref, mb_ref, wr_ref, wn_ref, wo_ref, o_ref):
    y_ret = _dot(oret_ref[...], wr_ref[...])
    y_nsa = _dot(on_ref[...], wn_ref[...])
    mixed = ma_ref[...].astype(F32) * y_ret + mb_ref[...].astype(F32) * y_nsa
    o_ref[...] = x_ref[...] + _dot(mixed.astype(BF16), wo_ref[...])


def _merge(x2, oret, on, ma, mb, w_o_ret, w_o_nsa, w_out):
    M, D = x2.shape
    tm = min(TM_MERGE, M)
    row = lambda width: pl.BlockSpec((tm, width), lambda i: (i, 0))
    res = lambda a: pl.BlockSpec(a.shape, lambda i: (0, 0), pipeline_mode=pl.Buffered(1))
    w = [a.astype(BF16) for a in (w_o_ret, w_o_nsa, w_out)]
    return pl.pallas_call(
        _merge_kernel,
        grid=(M // tm,),
        in_specs=[row(D), row(oret.shape[1]), row(D), row(D), row(D)] + [res(a) for a in w],
        out_specs=row(D),
        out_shape=jax.ShapeDtypeStruct((M, D), F32),
        compiler_params=pltpu.CompilerParams(vmem_limit_bytes=48 * MIB),
        name="merge_out_projection",
    )(x2, oret, on, ma, mb, *w)


def _rms(x, g):
    return x * lax.rsqrt(jnp.mean(x * x, -1, keepdims=True) + EPS) * g


def _ffn_kernel(x_ref, g_ref, wi_ref, wo_ref, gf_ref, o_ref, h_scr, acc_scr, *, final_norm):
    x = x_ref[...]
    h_scr[...] = _rms(x, g_ref[...]).astype(BF16)
    acc_scr[...] = x
    cw = FFN_HIDDEN // 2
    for c in range(2):
        a = _dot(h_scr[...], wi_ref[:, c * cw:(c + 1) * cw])
        b = _dot(h_scr[...], wi_ref[:, FFN_HIDDEN + c * cw:FFN_HIDDEN + (c + 1) * cw])
        acc_scr[...] += _dot((_silu(a) * b).astype(BF16), wo_ref[c * cw:(c + 1) * cw, :])
    y = acc_scr[...]
    o_ref[...] = _rms(y, gf_ref[...]) if final_norm else y


def _ffn(x2, g, w_in, w_out, g_final, final_norm):
    M, D = x2.shape
    tm = min(TM_MERGE, M)
    row = pl.BlockSpec((tm, D), lambda i: (i, 0))
    vec = pl.BlockSpec((1, D), lambda i: (0, 0))
    res = lambda a: pl.BlockSpec(a.shape, lambda i: (0, 0), pipeline_mode=pl.Buffered(1))
    wi, wo = w_in.astype(BF16), w_out.astype(BF16)
    return pl.pallas_call(
        functools.partial(_ffn_kernel, final_norm=final_norm),
        grid=(M // tm,),
        in_specs=[row, vec, res(wi), res(wo), vec],
        out_specs=row,
        out_shape=jax.ShapeDtypeStruct((M, D), F32),
        scratch_shapes=[pltpu.VMEM((tm, D), BF16), pltpu.VMEM((tm, D), F32)],
        compiler_params=pltpu.CompilerParams(vmem_limit_bytes=52 * MIB),
        name="swiglu_ffn",
    )(x2, g.reshape(1, D), wi, wo, g_final.reshape(1, D))


def kernel(x, norm_mix_g, w_in, cmp_pos_k, cmp_pos_v, cmp_w1_k, cmp_w2_k, cmp_w1_v, cmp_w2_v, w_o_ret, w_o_nsa,
           w_out, norm_ffn_g, w_ffn_in, w_ffn_out, rel_bias, norm_final_g):
    B, T, D = x.shape
    depth = w_in.shape[0]
    G, HG = NSA_KV_GROUPS, NSA_HG
    half = RET_DK // 2
    freqs = ROPE_BASE ** (-jnp.arange(half, dtype=F32) / half)
    ang = jnp.arange(T, dtype=jnp.int32).astype(F32)[:, None] * freqs
    cos, sin = jnp.cos(ang), jnp.sin(ang)
    near, cmpb = _bias_tables(rel_bias, T, TQ)
    for i in range(depth):
        qr, kr, vr, gr, qn, kvc16, ks, vst, kw, vwt, gates_t, ma, mb = _in_projection(
            x, norm_mix_g[i], _pack_w_in(w_in[i]), cos, sin)
        o_ret = _retention(qr, kr, vr, gr)
        kc, vct = _compress(kvc16,cmp_pos_k[i], cmp_pos_v[i], cmp_w1_k[i], cmp_w2_k[i], cmp_w1_v[i], cmp_w2_v[i])
        o_n = _nsa_attention(qn.reshape(B, G, HG, T, NSA_HD), kc, vct, ks, vst, kw, vwt, gates_t, near, cmpb)
        o_n = o_n.reshape(B * T, NSA_HEADS * NSA_HD)
        x2 = _merge(x.reshape(B * T, D), o_ret.reshape(B * T, -1), o_n, ma.reshape(B * T, D), mb.reshape(B * T, D),
                    w_o_ret[i], w_o_nsa[i], w_out[i])
        x2 = _ffn(x2, norm_ffn_g[i], w_ffn_in[i], w_ffn_out[i], norm_final_g, final_norm=(i == depth - 1))
        x = x2.reshape(B, T, D)
    return x
```

```python
import functools
import math

import jax
import jax.numpy as jnp
import numpy as np
from jax import lax
from jax.experimental import pallas as pl
from jax.experimental.pallas import tpu as pltpu

F32 = jnp.float32
BF16 = jnp.bfloat16

D_MODEL = 1024
RET_HEADS = 4
RET_DK = 256
RET_DV = 512
RET_BLOCK = 256
ROPE_BASE = 10000.0
GN_EPS = 1e-5
NSA_HEADS = 16
NSA_KV_GROUPS = 2
NSA_HG = NSA_HEADS // NSA_KV_GROUPS
NSA_HD = 64
CMP_LEN = 32
CMP_STRIDE = 16
CMP_HIDDEN = 256
SEL_BLOCK = 64
SEL_TOPK = 8
WINDOW = 512
FORCED_SCORE = 1e4
REL_BUCKETS = 32
REL_MAX_DIST = 128
FFN_HIDDEN = -(-8 * D_MODEL // (3 * 256)) * 256
EPS = 1e-6
NEG_INF = -1e30
PICKED = -3e38
MASKED_CODE = REL_BUCKETS
NEAR_OFFSETS = (0, 1, 2)
LOG2E = math.log2(math.e)

LANES = 128
SUBLANES = 8
MIB = 1024 * 1024
NSA_VMEM_LIMIT = 56 * MIB
NSA_VT_ROWS = NSA_HD + 16

_SEG = np.cumsum([0, RET_HEADS * RET_DK, RET_HEADS * RET_DK, RET_HEADS * RET_DV, RET_HEADS * RET_DV,
                  NSA_HEADS * NSA_HD, 6 * NSA_KV_GROUPS * NSA_HD, 3 * NSA_HEADS, D_MODEL, D_MODEL])
N_QR, N_KR, N_VR, N_GR = RET_HEADS * RET_DK, RET_HEADS * RET_DK, RET_HEADS * RET_DV, RET_HEADS * RET_DV
N_QN, N_KV = NSA_HEADS * NSA_HD, 6 * NSA_KV_GROUPS * NSA_HD
N_GATE = NSA_KV_GROUPS * LANES
_P = np.cumsum([0, N_QR, N_KR, N_VR, N_GR, N_QN, N_KV, N_GATE, D_MODEL, D_MODEL])
P_QR, P_KR, P_VR, P_GR, P_QN, P_KV, P_GATE, P_MA, P_MB, P_END = [int(v) for v in _P]

TM_PROJ = 512
TM_MERGE = 512
TQ = 256
NT = (((1,), (1,)), ((), ()))


def _dot(a, b):
    return jnp.dot(a, b, preferred_element_type=F32)


def _dot_nt(a, b):
    return lax.dot_general(a, b, NT, preferred_element_type=F32)


def _sigmoid(x):
    return 1.0 / (1.0 + jnp.exp(-x))


def _silu(x):
    return x * _sigmoid(x)


def _inproj_kernel(x_ref, g_ref, cos_ref, sin_ref, feat_ref, w_ref,
                   qr_o, kr_o, vr_o, gr_o, qn_o, kvc_o, ks_o, vst_o, kw_o, vwt_o, gate_o, ma_o, mb_o, h_scr, kvc_scr):
    x = x_ref[0]
    h = x * lax.rsqrt(jnp.mean(x * x, -1, keepdims=True) + EPS) * g_ref[...]
    h_scr[...] = h.astype(BF16)
    cos = cos_ref[...]
    sin = sin_ref[...]
    half = RET_DK // 2

    def proj(lo, width):
        return _dot(h_scr[...], w_ref[:, lo:lo + width])

    y32 = proj(P_KV, N_KV)
    y = y32.astype(BF16)
    y_t = y32.T.astype(BF16)
    tm = y.shape[0]
    ones_row = (lax.broadcasted_iota(jnp.int32, (NSA_VT_ROWS - NSA_HD, tm), 0) == 0).astype(BF16)
    for g in range(NSA_KV_GROUPS):
        lo = lambda j: (j * NSA_KV_GROUPS + g) * NSA_HD
        ks_o[0, g, :, :NSA_HD] = y[:, lo(2):lo(2) + NSA_HD]
        ks_o[0, g, :, NSA_HD:] = feat_ref[:, NSA_HD:]
        kw_o[0, g] = y[:, lo(4):lo(4) + NSA_HD]
        for out, j in ((vst_o, 3), (vwt_o, 5)):
            out[0, g, :NSA_HD, :] = y_t[lo(j):lo(j) + NSA_HD, :]
            out[0, g, NSA_HD:, :] = ones_row
    for s in range(2):
        kvc_scr[s] = y32[:, s * LANES:(s + 1) * LANES]
        for l in range(CMP_STRIDE):
            rows = kvc_scr[s, pl.ds(l, tm // CMP_STRIDE, stride=CMP_STRIDE), :].astype(BF16)
            for g in range(NSA_KV_GROUPS):
                kvc_o[s, 0, g, :, l * NSA_HD:(l + 1) * NSA_HD] = rows[:, g * NSA_HD:(g + 1) * NSA_HD]
    gates_t = _sigmoid(proj(P_GATE, N_GATE)).T
    for g in range(NSA_KV_GROUPS):
        gate_o[0, g] = gates_t[g * LANES:(g + 1) * LANES, :]
    cw = 512
    per = cw // NSA_HD
    for c in range(N_QN // cw):
        y = (proj(P_QN + c * cw, cw) * (NSA_HD ** -0.5 * LOG2E)).astype(BF16)
        for j in range(per):
            qn_o[0, c * per + j] = y[:, j * NSA_HD:(j + 1) * NSA_HD]
    for hd in range(RET_HEADS):
        for out, base, scale in ((qr_o, P_QR, 1.0), (kr_o, P_KR, RET_DK ** -0.5)):
            y = proj(base + hd * RET_DK, RET_DK)
            x1, x2 = y[:, :half], y[:, half:]
            out[0, :, hd * RET_DK:hd * RET_DK + half] = ((x1 * cos - x2 * sin) * scale).astype(BF16)
            out[0, :, hd * RET_DK + half:(hd + 1) * RET_DK] = ((x1 * sin + x2 * cos) * scale).astype(BF16)
    for c in range(N_VR // cw):
        vr_o[0, :, c * cw:(c + 1) * cw] = proj(P_VR + c * cw, cw).astype(BF16)
        gr_o[0, :, c * cw:(c + 1) * cw] = _silu(proj(P_GR + c * cw, cw)).astype(BF16)
    for c in range(D_MODEL // cw):
        ma_o[0, :, c * cw:(c + 1) * cw] = _sigmoid(proj(P_MA + c * cw, cw)).astype(BF16)
        mb_o[0, :, c * cw:(c + 1) * cw] = _sigmoid(proj(P_MB + c * cw, cw)).astype(BF16)


def _pack_w_in(w):
    s = [int(v) for v in _SEG]
    gate = w[:, s[6]:s[7]].reshape(D_MODEL, NSA_KV_GROUPS, 3 * NSA_HG)
    gate = jnp.pad(gate, ((0, 0), (0, 0), (0, LANES - 3 * NSA_HG))).reshape(D_MODEL, N_GATE)
    return jnp.concatenate([w[:, s[0]:s[6]], gate, w[:, s[7]:s[9]]], axis=1).astype(BF16)


def _in_projection(x, g, w_packed, cos, sin):
    B, T, D = x.shape
    G = NSA_KV_GROUPS
    tm = min(TM_PROJ, T)
    nt = T // tm
    assert T // SEL_BLOCK <= LANES - NSA_HD
    own_block = (np.arange(T)[:, None] // SEL_BLOCK) == (np.arange(LANES)[None, :] - NSA_HD)
    feat = jnp.asarray(np.where(own_block, NEG_INF, 0.0), BF16)
    row = lambda width: pl.BlockSpec((1, tm, width), lambda b, t: (b, t, 0))
    grp = lambda width: pl.BlockSpec((1, G, tm, width), lambda b, t: (b, 0, t, 0))
    grp_t = lambda rows: pl.BlockSpec((1, G, rows, tm), lambda b, t: (b, 0, 0, t))
    grp_shape = lambda width: jax.ShapeDtypeStruct((B, G, T, width), BF16)
    grp_t_shape = lambda rows, dt: jax.ShapeDtypeStruct((B, G, rows, T), dt)
    out_shape = (
        jax.ShapeDtypeStruct((B, T, N_QR), BF16), jax.ShapeDtypeStruct((B, T, N_KR), BF16),
        jax.ShapeDtypeStruct((B, T, N_VR), BF16), jax.ShapeDtypeStruct((B, T, N_GR), BF16),
        jax.ShapeDtypeStruct((B, NSA_HEADS, T, NSA_HD), BF16),
        jax.ShapeDtypeStruct((2, B, G, T // CMP_STRIDE, CMP_STRIDE * NSA_HD), BF16),
        grp_shape(LANES), grp_t_shape(NSA_VT_ROWS, BF16), grp_shape(NSA_HD), grp_t_shape(NSA_VT_ROWS, BF16),
        grp_t_shape(LANES, F32),
        jax.ShapeDtypeStruct((B, T, D_MODEL), BF16), jax.ShapeDtypeStruct((B, T, D_MODEL), BF16))
    out_specs = (
        row(N_QR), row(N_KR), row(N_VR), row(N_GR),
        pl.BlockSpec((1, NSA_HEADS, tm, NSA_HD), lambda b, t: (b, 0, t, 0)),
        pl.BlockSpec((2, 1, G, tm // CMP_STRIDE, CMP_STRIDE * NSA_HD), lambda b, t: (0, b, 0, t, 0)),
        grp(LANES), grp_t(NSA_VT_ROWS), grp(NSA_HD), grp_t(NSA_VT_ROWS),
        grp_t(LANES),
        row(D_MODEL), row(D_MODEL))
    out_bytes = tm * (2 * (N_QR + N_KR + N_VR + N_GR + N_QN + 2 * D_MODEL) + 2 * G * (3 * NSA_HD + 3 * LANES)
                      + 4 * N_GATE)
    vmem = 2 * out_bytes + 2 * tm * D * 4 + 2 * D * P_END + tm * D * 2 + 10 * MIB
    return pl.pallas_call(
        _inproj_kernel,
        grid=(B, nt),
        in_specs=[pl.BlockSpec((1, tm, D), lambda b, t: (b, t, 0)),
                  pl.BlockSpec((1, D), lambda b, t: (0, 0)),
                  pl.BlockSpec((tm, RET_DK // 2), lambda b, t: (t, 0)),
                  pl.BlockSpec((tm, RET_DK // 2), lambda b, t: (t, 0)),
                  pl.BlockSpec((tm, LANES), lambda b, t: (t, 0)),
                  pl.BlockSpec((D, P_END), lambda b, t: (0, 0), pipeline_mode=pl.Buffered(1))],
        out_specs=out_specs,
        out_shape=out_shape,
        scratch_shapes=[pltpu.VMEM((tm, D), BF16), pltpu.VMEM((2, tm, G * NSA_HD), F32)],
        compiler_params=pltpu.CompilerParams(vmem_limit_bytes=int(vmem)),
        name="in_projection",
    )(x, g.reshape(1, D), cos, sin, feat, w_packed)


def _retention_kernel(q_ref, k_ref, v_ref, g_ref, dmask_ref, xi_ref, zeta_ref, gch_ref, o_ref):
    C = RET_BLOCK
    nc = q_ref.shape[1] // C
    dmask = dmask_ref[0]
    xi = xi_ref[0]
    zeta = zeta_ref[0]
    gch = gch_ref[0, :, 0:1]
    r = None
    for c in range(nc):
        rows = slice(c * C, (c + 1) * C)
        qc = q_ref[0, rows, :]
        kc = k_ref[0, rows, :]
        vc = v_ref[0, rows, :]
        if c < nc - 1:
            kz = (kc.astype(F32) * zeta).astype(BF16)
            kv = lax.dot_general(kz, vc, (((0,), (0,)), ((), ())), preferred_element_type=F32)
            r_next = kv if c == 0 else gch * r + kv
        s = _dot_nt(qc, kc) * dmask
        o = _dot(s.astype(BF16), vc)
        if c > 0:
            o = o + _dot(qc, r.astype(BF16)) * xi
        if c < nc - 1:
            r = r_next
        mu = jnp.mean(o, -1, keepdims=True)
        d = o - mu
        var = jnp.mean(d * d, -1, keepdims=True)
        on = d * lax.rsqrt(var + GN_EPS)
        o_ref[0, rows, :] = (g_ref[0, rows, :].astype(F32) * on).astype(BF16)


def _retention(qr, kr, vr, gr):
    B, T, _ = qr.shape
    C = RET_BLOCK
    lg = jnp.log(1.0 - 2.0 ** (-5.0 - jnp.arange(RET_HEADS, dtype=F32)))
    n = jnp.arange(C, dtype=F32)
    diff = n[:, None] - n[None, :]
    dmask = jnp.where(diff >= 0, jnp.exp(jnp.maximum(diff, 0.0)[None] * lg[:, None, None]), 0.0)
    xi = jnp.exp((n + 1.0)[None] * lg[:, None])[:, :, None]
    zeta = jnp.exp((C - 1.0 - n)[None] * lg[:, None])[:, :, None]
    gch = jnp.broadcast_to(jnp.exp(C * lg)[:, None, None], (RET_HEADS, 1, LANES))
    qk_spec = pl.BlockSpec((1, T, RET_DK), lambda b, h: (b, 0, h))
    v_spec = pl.BlockSpec((1, T, RET_DV), lambda b, h: (b, 0, h))
    return pl.pallas_call(
        _retention_kernel,
        grid=(B, RET_HEADS),
        in_specs=[qk_spec, qk_spec, v_spec, v_spec,
                  pl.BlockSpec((1, C, C), lambda b, h: (h, 0, 0)),
                  pl.BlockSpec((1, C, 1), lambda b, h: (h, 0, 0)),
                  pl.BlockSpec((1, C, 1), lambda b, h: (h, 0, 0)),
                  pl.BlockSpec((1, 1, LANES), lambda b, h: (h, 0, 0))],
        out_specs=v_spec,
        out_shape=jax.ShapeDtypeStruct((B, T, RET_HEADS * RET_DV), BF16),
        compiler_params=pltpu.CompilerParams(vmem_limit_bytes=32 * MIB),
        name="retention",
    )(qr, kr, vr, gr, dmask, xi, zeta, gch)


def _gelu_tanh(x):
    return 0.5 * x * (1.0 + jnp.tanh(math.sqrt(2.0 / math.pi) * (x + 0.044715 * (x * x * x))))


def _compress_kernel(x_ref, pk_ref, pv_ref, w1k_ref, w2k_ref, w1v_ref, w2vt_ref, ko_ref, vto_ref):
    half = CMP_STRIDE * NSA_HD

    def hidden(s, g, p_ref, w1_ref):
        x = x_ref[s, 0, g].astype(F32)
        a = _dot((x + p_ref[0:1, :]).astype(BF16), w1_ref[0:half, :])
        b = _dot((x + p_ref[1:2, :]).astype(BF16), w1_ref[half:2 * half, :])
        pre = a + pltpu.roll(b, b.shape[0] - 1, 0)
        return _gelu_tanh(pre).astype(BF16)

    for g in range(NSA_KV_GROUPS):
        ko_ref[0, g] = _dot(hidden(0, g, pk_ref, w1k_ref), w2k_ref[...]).astype(BF16)
        vto_ref[0, g] = _dot_nt(w2vt_ref[...], hidden(1, g, pv_ref, w1v_ref)).astype(BF16)


def _compress(kvc16, pos_k, pos_v, w1k, w2k, w1v, w2v):
    _, B, G, NC, F = kvc16.shape
    full = lambda a: pl.BlockSpec(a.shape, lambda b: (0,) * a.ndim)
    pk = pos_k.reshape(2, F)
    pv = pos_v.reshape(2, F)
    w = [a.astype(BF16) for a in (w1k, w2k, w1v, w2v.T)]
    return pl.pallas_call(
        _compress_kernel,
        grid=(B,),
        in_specs=[pl.BlockSpec((2, 1, G, NC, F), lambda b: (0, b, 0, 0, 0)), full(pk), full(pv)]
        + [full(a) for a in w],
        out_specs=(pl.BlockSpec((1, G, NC, NSA_HD), lambda b: (b, 0, 0, 0)),
                   pl.BlockSpec((1, G, NSA_HD, NC), lambda b: (b, 0, 0, 0))),
        out_shape=(jax.ShapeDtypeStruct((B, G, NC, NSA_HD), BF16), jax.ShapeDtypeStruct((B, G, NSA_HD, NC), BF16)),
        compiler_params=pltpu.CompilerParams(vmem_limit_bytes=32 * MIB),
        name="nsa_compress",
    )(kvc16, pk, pv, *w)


def _bias_kernel(tab_ref, nb_ref, cb_ref, near_o, cmp_o):
    h = pl.program_id(0)
    far = tab_ref[REL_BUCKETS - 1, h]

    def build(bk):
        acc = jnp.zeros(bk.shape, F32)
        for k in range(REL_BUCKETS - 1):
            acc = jnp.where(bk == k, (tab_ref[k, h] - far) * LOG2E, acc)
        return jnp.where(bk == MASKED_CODE, NEG_INF, acc)

    near_o[0] = build(nb_ref[...])
    cmp_o[0] = build(cb_ref[...])


def _t5_bucket(dist):
    dist = jnp.maximum(dist, 0)
    max_exact = REL_BUCKETS // 2
    large = max_exact + (jnp.log(jnp.maximum(dist, 1).astype(F32) / max_exact)
                         / math.log(REL_MAX_DIST / max_exact) * (REL_BUCKETS - max_exact)).astype(jnp.int32)
    large = jnp.minimum(large, REL_BUCKETS - 1)
    return jnp.where(dist < max_exact, dist, large)


def _bias_tables(rel_bias, T, tq):
    nc = T // CMP_STRIDE
    i = jnp.arange(tq, dtype=jnp.int32)
    d = jnp.asarray(NEAR_OFFSETS, jnp.int32)[:, None, None] * tq + i[None, None, :] - i[None, :, None]
    near_b = jnp.where((d < 0) | (d >= WINDOW), MASKED_CODE, _t5_bucket(d))
    near_b = jnp.concatenate([near_b, jnp.full((1, tq, tq), MASKED_CODE, jnp.int32)], axis=0)
    cmp_end = jnp.arange(nc, dtype=jnp.int32) * CMP_STRIDE + CMP_LEN - 1
    cmp_b = _t5_bucket(jnp.arange(T, dtype=jnp.int32)[None, :] - cmp_end[:, None])
    return pl.pallas_call(
        _bias_kernel,
        grid=(NSA_HEADS,),
        in_specs=[pl.BlockSpec(memory_space=pltpu.SMEM),
                  pl.BlockSpec(near_b.shape, lambda h: (0, 0, 0)),
                  pl.BlockSpec(cmp_b.shape, lambda h: (0, 0))],
        out_specs=(pl.BlockSpec((1,) + near_b.shape, lambda h: (h, 0, 0, 0)),
                   pl.BlockSpec((1,) + cmp_b.shape, lambda h: (h, 0, 0))),
        out_shape=(jax.ShapeDtypeStruct((NSA_HEADS,) + near_b.shape, F32),
                   jax.ShapeDtypeStruct((NSA_HEADS,) + cmp_b.shape, F32)),
        name="rel_bias_tables",
    )(rel_bias, near_b, cmp_b)


def _nsa_kernel(q_ref, kc_ref, vct_ref, ks_ref, vst_ref, kw_ref, vwt_ref, gt_ref,
                near_ref, cb_ref, ovl_ref, o_ref,
                qa_scr, sa_scr, sb_scr, cma_scr, cmb_scr, ms_scr, mw_scr, accs_scr, accw_scr, *, tq, sel_k):
    HG = NSA_HG
    R = HG * tq
    NC = kc_ref.shape[2]
    NS = ovl_ref.shape[0]
    UW = tq
    ti = pl.program_id(2)
    s0 = ti * tq

    def head(x, h):
        return x[:, h * tq:(h + 1) * tq]

    def keys(ref, u, n=1):
        return ref[0, 0, pl.ds(pl.multiple_of(u * UW, UW), n * UW), :]

    def values_t(ref, u, n=1):
        return ref[0, 0, :, pl.ds(pl.multiple_of(u * UW, UW), n * UW)]

    cmax = lambda s_ref: cma_scr if s_ref is sa_scr else cmb_scr

    def score(s_ref, k_ref, u, n, queries, entries=None):
        def produce(h):
            s = _dot_nt(keys(k_ref, u, n), queries(h))
            if entries is not None:
                s = jnp.concatenate([s[i * UW:(i + 1) * UW] + near_ref[h, e] for i, e in enumerate(entries)], axis=0)
            s_ref[0:n * UW, h * tq:(h + 1) * tq] = s
            groups = [s[r * SUBLANES:(r + 1) * SUBLANES] for r in range(n * UW // SUBLANES)]
            cmax(s_ref)[:, h * tq:(h + 1) * tq] = functools.reduce(jnp.maximum, groups)
        return produce

    def fold(s_ref, n, m_ref, vt_ref, u, acc_ref):
        def consume(h):
            c = slice(h * tq, (h + 1) * tq)
            m_old = m_ref[0:1, c]
            m_new = jnp.maximum(m_old, jnp.max(cmax(s_ref)[:, c], axis=0, keepdims=True))
            part = [_dot(values_t(vt_ref, u + i), jnp.exp2((s_ref[i * UW:(i + 1) * UW, c] - m_new).astype(BF16)))
                    for i in range(n)]
            acc_ref[:, c] = jnp.exp2(m_old - m_new) * acc_ref[:, c] + functools.reduce(jnp.add, part)
            m_ref[:, c] = jnp.broadcast_to(m_new, (SUBLANES, tq))
        return consume

    def stage(produce=None, consume=None):
        for h in range(HG):
            if produce is not None:
                produce(h)
            if consume is not None:
                consume(h)

    q_head = lambda h: q_ref[0, 0, h]
    qa_head = lambda h: qa_scr[h * tq:(h + 1) * tq, :]

    def normalized(acc_ref):
        acc = acc_ref[...]
        return acc[:NSA_HD] / acc[NSA_HD:NSA_HD + 1]

    for ref in (ms_scr, mw_scr):
        ref[...] = jnp.full(ref.shape, NEG_INF, F32)
    for ref in (accs_scr, accw_scr):
        ref[...] = jnp.zeros_like(ref)

    masked = len(NEAR_OFFSETS)
    w0 = jnp.maximum(ti - 2, 0)

    def window_entry(slot):
        back = jnp.minimum(ti, 2) - slot
        return jnp.where(back >= 0, back, masked)

    win_bufs = (sa_scr, sb_scr, sa_scr)
    win_slots = (1, 0, 2)
    win_score = [score(buf, kw_ref, w0 + slot, 1, q_head, (window_entry(slot),))
                 for buf, slot in zip(win_bufs, win_slots)]
    win_fold = [fold(buf, 1, mw_scr, vwt_ref, w0 + slot, accw_scr) for buf, slot in zip(win_bufs, win_slots)]

    stage(win_score[0])

    pos_c = s0 + lax.broadcasted_iota(jnp.int32, (NC, tq), 1)
    cend = lax.broadcasted_iota(jnp.int32, (NC, tq), 0) * CMP_STRIDE + (CMP_LEN - 1)
    valid_c = cend <= pos_c
    valid_f = valid_c.astype(F32)
    tile_cols = pl.ds(pl.multiple_of(s0, tq), tq)
    lc = [_dot_nt(kc_ref[0, 0], q_head(h)) + cb_ref[h, :, tile_cols] for h in range(HG)]
    pc = []
    for h in range(HG):
        z = jnp.where(valid_c, lc[h], NEG_INF)
        e = jnp.exp2(z - jnp.max(z, 0, keepdims=True)) * valid_f
        pc.append(e / jnp.maximum(jnp.sum(e, 0, keepdims=True), 1e-30))
    oc = [_dot(vct_ref[0, 0], pc[h].astype(BF16)) for h in range(HG)]
    psum = functools.reduce(jnp.add, pc)

    p_hi = psum.astype(BF16)
    r1 = psum - p_hi.astype(F32)
    p_mid = r1.astype(BF16)
    p_lo = (r1 - p_mid.astype(F32)).astype(BF16)
    ovl = ovl_ref[...]
    imp = _dot(ovl, p_hi) + _dot(ovl, p_mid) + _dot(ovl, p_lo)
    jrow = lax.broadcasted_iota(jnp.int32, (NS, tq), 0)
    posl = s0 + lax.broadcasted_iota(jnp.int32, (NS, tq), 1)
    cur = lax.shift_right_logical(posl, int(math.log2(SEL_BLOCK)))
    forced = (jrow == 0) | (jrow == cur) | (jrow == cur - 1)
    started = jrow * SEL_BLOCK <= posl
    imp = jnp.where(forced, FORCED_SCORE, imp)
    imp = jnp.where(started, imp, NEG_INF)
    sel = jnp.zeros((NS, tq), F32)
    for it in range(max(sel_k, HG)):
        if it < sel_k:
            mx = jnp.max(imp, axis=0, keepdims=True)
            first = jnp.min(jnp.where(imp == mx, jrow, NS), axis=0, keepdims=True)
            hit = jrow == first
            sel = jnp.where(hit, 1.0, sel)
            imp = jnp.where(hit, PICKED, imp)
        if it < HG:
            win_score[1](it)
            win_fold[0](it)
    stage(win_score[2], win_fold[1])
    notsel = 1.0 - jnp.where(started, sel, 0.0)
    notsel = jnp.concatenate([notsel, jnp.zeros((LANES - NS, tq), F32)], axis=0)
    notsel = pltpu.roll(notsel.T, NSA_HD, 1).astype(BF16)
    for hg in range(HG):
        qa_scr[hg * tq:(hg + 1) * tq, :NSA_HD] = q_ref[0, 0, hg]
        qa_scr[hg * tq:(hg + 1) * tq, NSA_HD:] = notsel[:, NSA_HD:]

    prev_u = jnp.where(ti >= 1, ti - 1, 1)
    sel_job = lambda s_ref, u, n: (s_ref, n, ms_scr, vst_ref, u, accs_scr)
    stage(score(sb_scr, ks_ref, ti, 1, qa_head, (0,)), win_fold[2])
    stage(score(sa_scr, ks_ref, prev_u, 1, qa_head, (jnp.where(ti >= 1, 1, masked),)), fold(*sel_job(sb_scr, ti, 1)))
    stage(None, fold(*sel_job(sa_scr, prev_u, 1)))

    n_plain = jnp.maximum(ti - 1, 0)
    n_quads = lax.shift_right_logical(n_plain, 2)

    def plain_run(first, count):
        bufs = (sa_scr, sb_scr)
        for i in range(count + 1):
            produce = score(bufs[i % 2], ks_ref, first + i, 1, qa_head) if i < count else None
            consume = fold(*sel_job(bufs[(i - 1) % 2], first + i - 1, 1)) if i > 0 else None
            stage(produce, consume)

    def plain_quad(j, carry):
        plain_run(4 * j, 4)
        return carry

    lax.fori_loop(0, n_quads, plain_quad, 0)
    left = n_plain - 4 * n_quads

    @pl.when(left >= 2)
    def _():
        plain_run(4 * n_quads, 2)

    @pl.when(lax.rem(left, 2) == 1)
    def _():
        stage(score(sa_scr, ks_ref, n_plain - 1, 1, qa_head))
        stage(None, fold(*sel_job(sa_scr, n_plain - 1, 1)))

    gt = gt_ref[0, 0]
    o_s = normalized(accs_scr)
    o_w = normalized(accw_scr)

    def gated(h):
        return (gt[3 * h:3 * h + 1] * oc[h] + gt[3 * h + 1:3 * h + 2] * head(o_s, h)
                + gt[3 * h + 2:3 * h + 3] * head(o_w, h))

    pairs = [jnp.concatenate([gated(2 * j), gated(2 * j + 1)], axis=0).T for j in range(HG // 2)]
    o_ref[0] = jnp.concatenate(pairs, axis=1).astype(BF16)


def _nsa_attention(qn, kc, vct, ks, vst, kw, vwt, gates_t, near, cmpb):
    B, G, HG, T, hd = qn.shape
    tq = TQ
    assert T % tq == 0 and T >= 3 * tq and WINDOW == 2 * tq and tq >= 2 * REL_MAX_DIST and tq % LANES == 0
    nq = T // tq
    NC = kc.shape[2]
    NS = T // SEL_BLOCK
    sel_k = min(SEL_TOPK, NS)
    n = np.arange(NC)
    j = np.arange(NS)
    ovl = ((n[None, :] * CMP_STRIDE < (j[:, None] + 1) * SEL_BLOCK)
           & (n[None, :] * CMP_STRIDE + CMP_LEN - 1 >= j[:, None] * SEL_BLOCK) & (n[None, :] < NC - 1))
    ovl = jnp.asarray(ovl, BF16)
    seq = lambda rows, width: pl.BlockSpec((1, 1, rows, width), lambda g, b, t: (b, g, 0, 0))
    R = HG * tq
    W = HG * hd
    return pl.pallas_call(
        functools.partial(_nsa_kernel, tq=tq, sel_k=sel_k),
        grid=(G, B, nq),
        in_specs=[pl.BlockSpec((1, 1, HG, tq, hd), lambda g, b, t: (b, g, 0, t, 0)),
                  seq(NC, hd), seq(hd, NC), seq(T, LANES), seq(NSA_VT_ROWS, T), seq(T, hd), seq(NSA_VT_ROWS, T),
                  pl.BlockSpec((1, 1, LANES, tq), lambda g, b, t: (b, g, 0, t)),
                  pl.BlockSpec((HG, len(NEAR_OFFSETS) + 1, tq, tq), lambda g, b, t: (g, 0, 0, 0),
                               pipeline_mode=pl.Buffered(1)),
                  pl.BlockSpec((HG, NC, T), lambda g, b, t: (g, 0, 0), pipeline_mode=pl.Buffered(1)),
                  pl.BlockSpec(ovl.shape, lambda g, b, t: (0, 0))],
        out_specs=pl.BlockSpec((1, tq, W), lambda g, b, t: (b, t, g)),
        out_shape=jax.ShapeDtypeStruct((B, T, G * W), BF16),
        scratch_shapes=[pltpu.VMEM((R, LANES), BF16),
                        pltpu.VMEM((2 * tq, R), F32), pltpu.VMEM((2 * tq, R), F32),
                        pltpu.VMEM((SUBLANES, R), F32), pltpu.VMEM((SUBLANES, R), F32),
                        pltpu.VMEM((SUBLANES, R), F32), pltpu.VMEM((SUBLANES, R), F32),
                        pltpu.VMEM((NSA_VT_ROWS, R), F32), pltpu.VMEM((NSA_VT_ROWS, R), F32)],
        compiler_params=pltpu.CompilerParams(vmem_limit_bytes=NSA_VMEM_LIMIT),
        name="nsa_attention",
    )(qn, kc, vct, ks, vst, kw, vwt, gates_t, near, cmpb, ovl)


def _merge_kernel(x_ref, oret_ref, on_ref, ma_ref, mb_ref, wr_ref, wn_ref, wo_ref, o_ref):
    y_ret = _dot(oret_ref[...], wr_ref[...])
    y_nsa = _dot(on_ref[...], wn_ref[...])
    mixed = ma_ref[...].astype(F32) * y_ret + mb_ref[...].astype(F32) * y_nsa
    o_ref[...] = x_ref[...] + _dot(mixed.astype(BF16), wo_ref[...])


def _merge(x2, oret, on, ma, mb, w_o_ret, w_o_nsa, w_out):
    M, D = x2.shape
    tm = min(TM_MERGE, M)
    row = lambda width: pl.BlockSpec((tm, width), lambda i: (i, 0))
    res = lambda a: pl.BlockSpec(a.shape, lambda i: (0, 0), pipeline_mode=pl.Buffered(1))
    w = [a.astype(BF16) for a in (w_o_ret, w_o_nsa, w_out)]
    return pl.pallas_call(
        _merge_kernel,
        grid=(M // tm,),
        in_specs=[row(D), row(oret.shape[1]), row(D), row(D), row(D)] + [res(a) for a in w],
        out_specs=row(D),
        out_shape=jax.ShapeDtypeStruct((M, D), F32),
        compiler_params=pltpu.CompilerParams(vmem_limit_bytes=48 * MIB),
        name="merge_out_projection",
    )(x2, oret, on, ma, mb, *w)


def _rms(x, g):
    return x * lax.rsqrt(jnp.mean(x * x, -1, keepdims=True) + EPS) * g


def _ffn_kernel(x_ref, g_ref, wi_ref, wo_ref, gf_ref, o_ref, h_scr, acc_scr, *, final_norm):
    x = x_ref[...]
    h_scr[...] = _rms(x, g_ref[...]).astype(BF16)
    acc_scr[...] = x
    cw = FFN_HIDDEN // 2
    for c in range(2):
        a = _dot(h_scr[...], wi_ref[:, c * cw:(c + 1) * cw])
        b = _dot(h_scr[...], wi_ref[:, FFN_HIDDEN + c * cw:FFN_HIDDEN + (c + 1) * cw])
        acc_scr[...] += _dot((_silu(a) * b).astype(BF16), wo_ref[c * cw:(c + 1) * cw, :])
    y = acc_scr[...]
    o_ref[...] = _rms(y, gf_ref[...]) if final_norm else y


def _ffn(x2, g, w_in, w_out, g_final, final_norm):
    M, D = x2.shape
    tm = min(TM_MERGE, M)
    row = pl.BlockSpec((tm, D), lambda i: (i, 0))
    vec = pl.BlockSpec((1, D), lambda i: (0, 0))
    res = lambda a: pl.BlockSpec(a.shape, lambda i: (0, 0), pipeline_mode=pl.Buffered(1))
    wi, wo = w_in.astype(BF16), w_out.astype(BF16)
    return pl.pallas_call(
        functools.partial(_ffn_kernel, final_norm=final_norm),
        grid=(M // tm,),
        in_specs=[row, vec, res(wi), res(wo), vec],
        out_specs=row,
        out_shape=jax.ShapeDtypeStruct((M, D), F32),
        scratch_shapes=[pltpu.VMEM((tm, D), BF16), pltpu.VMEM((tm, D), F32)],
        compiler_params=pltpu.CompilerParams(vmem_limit_bytes=52 * MIB),
        name="swiglu_ffn",
    )(x2, g.reshape(1, D), wi, wo, g_final.reshape(1, D))


def kernel(x, norm_mix_g, w_in, cmp_pos_k, cmp_pos_v, cmp_w1_k, cmp_w2_k, cmp_w1_v, cmp_w2_v, w_o_ret, w_o_nsa,
           w_out, norm_ffn_g, w_ffn_in, w_ffn_out, rel_bias, norm_final_g):
    B, T, D = x.shape
    depth = w_in.shape[0]
    G, HG = NSA_KV_GROUPS, NSA_HG
    half = RET_DK // 2
    freqs = ROPE_BASE ** (-jnp.arange(half, dtype=F32) / half)
    ang = jnp.arange(T, dtype=jnp.int32).astype(F32)[:, None] * freqs
    cos, sin = jnp.cos(ang), jnp.sin(ang)
    near, cmpb = _bias_tables(rel_bias, T, TQ)
    for i in range(depth):
        qr, kr, vr, gr, qn, kvc16, ks, vst, kw, vwt, gates_t, ma, mb = _in_projection(
            x, norm_mix_g[i], _pack_w_in(w_in[i]), cos, sin)
        o_ret = _retention(qr, kr, vr, gr)
        kc, vct = _compress(kvc16,cmp_pos_k[i], cmp_pos_v[i], cmp_w1_k[i], cmp_w2_k[i], cmp_w1_v[i], cmp_w2_v[i])
        o_n = _nsa_attention(qn.reshape(B, G, HG, T, NSA_HD), kc, vct, ks, vst, kw, vwt, gates_t, near, cmpb)
        o_n = o_n.reshape(B * T, NSA_HEADS * NSA_HD)
        x2 = _merge(x.reshape(B * T, D), o_ret.reshape(B * T, -1), o_n, ma.reshape(B * T, D), mb.reshape(B * T, D),
                    w_o_ret[i], w_o_nsa[i], w_out[i])
        x2 = _ffn(x2, norm_ffn_g[i], w_ffn_in[i], w_ffn_out[i], norm_final_g, final_norm=(i == depth - 1))
        x = x2.reshape(B, T, D)
    return x
```

```python
import functools
import math

import jax
import jax.numpy as jnp
import numpy as np
from jax import lax
from jax.experimental import pallas as pl
from jax.experimental.pallas import tpu as pltpu

F32 = jnp.float32
BF16 = jnp.bfloat16

D_MODEL = 1024
RET_HEADS = 4
RET_DK = 256
RET_DV = 512
RET_BLOCK = 256
ROPE_BASE = 10000.0
GN_EPS = 1e-5
NSA_HEADS = 16
NSA_KV_GROUPS = 2
NSA_HG = NSA_HEADS // NSA_KV_GROUPS
NSA_HD = 64
CMP_LEN = 32
CMP_STRIDE = 16
SEL_BLOCK = 64
SEL_TOPK = 8
WINDOW = 512
FORCED_SCORE = 1e4
REL_BUCKETS = 32
REL_MAX_DIST = 128
FFN_HIDDEN = -(-8 * D_MODEL // (3 * 256)) * 256
EPS = 1e-6
NEG_INF = -1e30
PICKED = -3e38
MASKED_CODE = REL_BUCKETS
NEAR_OFFSETS = (0, 1, 2)
LOG2E = math.log2(math.e)

LANES = 128
SUBLANES = 8
MIB = 1024 * 1024
NSA_VMEM_LIMIT = 56 * MIB
NSA_VT_ROWS = NSA_HD + 16

_SEG = np.cumsum([0, RET_HEADS * RET_DK, RET_HEADS * RET_DK, RET_HEADS * RET_DV, RET_HEADS * RET_DV,
                  NSA_HEADS * NSA_HD, 6 * NSA_KV_GROUPS * NSA_HD, 3 * NSA_HEADS, D_MODEL, D_MODEL])
N_QR, N_KR, N_VR, N_GR = RET_HEADS * RET_DK, RET_HEADS * RET_DK, RET_HEADS * RET_DV, RET_HEADS * RET_DV
N_QN, N_KV = NSA_HEADS * NSA_HD, 6 * NSA_KV_GROUPS * NSA_HD
N_GATE = NSA_KV_GROUPS * LANES
_P = np.cumsum([0, N_QR, N_KR, N_VR, N_GR, N_QN, N_KV, N_GATE, D_MODEL, D_MODEL])
P_QR, P_KR, P_VR, P_GR, P_QN, P_KV, P_GATE, P_MA, P_MB, P_END = [int(v) for v in _P]

TM_PROJ = 512
TM_MERGE = 512
TQ = 256
NT = (((1,), (1,)), ((), ()))


def _dot(a, b):
    return jnp.dot(a, b, preferred_element_type=F32)


def _dot_nt(a, b):
    return lax.dot_general(a, b, NT, preferred_element_type=F32)


def _sigmoid(x):
    return 1.0 / (1.0 + jnp.exp(-x))


def _silu(x):
    return x * _sigmoid(x)


def _inproj_kernel(x_ref, g_ref, cos_ref, sin_ref, feat_ref, w_ref,
                   qr_o, kr_o, vr_o, gr_o, qn_o, kvc_o, ks_o, vst_o, kw_o, vwt_o, gate_o, ma_o, mb_o, h_scr, kvc_scr):
    x = x_ref[0]
    h = x * lax.rsqrt(jnp.mean(x * x, -1, keepdims=True) + EPS) * g_ref[...]
    h_scr[...] = h.astype(BF16)
    cos = cos_ref[...]
    sin = sin_ref[...]
    half = RET_DK // 2

    def proj(lo, width):
        return _dot(h_scr[...], w_ref[:, lo:lo + width])

    y32 = proj(P_KV, N_KV)
    y = y32.astype(BF16)
    y_t = y32.T.astype(BF16)
    tm = y.shape[0]
    ones_row = (lax.broadcasted_iota(jnp.int32, (NSA_VT_ROWS - NSA_HD, tm), 0) == 0).astype(BF16)
    for g in range(NSA_KV_GROUPS):
        lo = lambda j: (j * NSA_KV_GROUPS + g) * NSA_HD
        ks_o[0, g, :, :NSA_HD] = y[:, lo(2):lo(2) + NSA_HD]
        ks_o[0, g, :, NSA_HD:] = feat_ref[:, NSA_HD:]
        kw_o[0, g] = y[:, lo(4):lo(4) + NSA_HD]
        for out, j in ((vst_o, 3), (vwt_o, 5)):
            out[0, g, :NSA_HD, :] = y_t[lo(j):lo(j) + NSA_HD, :]
            out[0, g, NSA_HD:, :] = ones_row
    for s in range(2):
        kvc_scr[s] = y32[:, s * LANES:(s + 1) * LANES]
        for l in range(CMP_STRIDE):
            rows = kvc_scr[s, pl.ds(l, tm // CMP_STRIDE, stride=CMP_STRIDE), :].astype(BF16)
            for g in range(NSA_KV_GROUPS):
                kvc_o[s, 0, g, :, l * NSA_HD:(l + 1) * NSA_HD] = rows[:, g * NSA_HD:(g + 1) * NSA_HD]
    gates_t = _sigmoid(proj(P_GATE, N_GATE)).T
    for g in range(NSA_KV_GROUPS):
        gate_o[0, g] = gates_t[g * LANES:(g + 1) * LANES, :]
    cw = 512
    per = cw // NSA_HD
    for c in range(N_QN // cw):
        y = (proj(P_QN + c * cw, cw) * (NSA_HD ** -0.5 * LOG2E)).astype(BF16)
        for j in range(per):
            qn_o[0, c * per + j] = y[:, j * NSA_HD:(j + 1) * NSA_HD]
    for hd in range(RET_HEADS):
        for out, base, scale in ((qr_o, P_QR, 1.0), (kr_o, P_KR, RET_DK ** -0.5)):
            y = proj(base + hd * RET_DK, RET_DK)
            x1, x2 = y[:, :half], y[:, half:]
            out[0, :, hd * RET_DK:hd * RET_DK + half] = ((x1 * cos - x2 * sin) * scale).astype(BF16)
            out[0, :, hd * RET_DK + half:(hd + 1) * RET_DK] = ((x1 * sin + x2 * cos) * scale).astype(BF16)
    for c in range(N_VR // cw):
        vr_o[0, :, c * cw:(c + 1) * cw] = proj(P_VR + c * cw, cw).astype(BF16)
        gr_o[0, :, c * cw:(c + 1) * cw] = _silu(proj(P_GR + c * cw, cw)).astype(BF16)
    for c in range(D_MODEL // cw):
        ma_o[0, :, c * cw:(c + 1) * cw] = _sigmoid(proj(P_MA + c * cw, cw)).astype(BF16)
        mb_o[0, :, c * cw:(c + 1) * cw] = _sigmoid(proj(P_MB + c * cw, cw)).astype(BF16)


def _pack_w_in(w):
    s = [int(v) for v in _SEG]
    gate = w[:, s[6]:s[7]].reshape(D_MODEL, NSA_KV_GROUPS, 3 * NSA_HG)
    gate = jnp.pad(gate, ((0, 0), (0, 0), (0, LANES - 3 * NSA_HG))).reshape(D_MODEL, N_GATE)
    return jnp.concatenate([w[:, s[0]:s[6]], gate, w[:, s[7]:s[9]]], axis=1).astype(BF16)


def _in_projection(x, g, w_packed, cos, sin):
    B, T, D = x.shape
    G = NSA_KV_GROUPS
    tm = min(TM_PROJ, T)
    nt = T // tm
    assert T // SEL_BLOCK <= LANES - NSA_HD
    own_block = (np.arange(T)[:, None] // SEL_BLOCK) == (np.arange(LANES)[None, :] - NSA_HD)
    feat = jnp.asarray(np.where(own_block, NEG_INF, 0.0), BF16)
    row = lambda width: pl.BlockSpec((1, tm, width), lambda b, t: (b, t, 0))
    grp = lambda width: pl.BlockSpec((1, G, tm, width), lambda b, t: (b, 0, t, 0))
    grp_t = lambda rows: pl.BlockSpec((1, G, rows, tm), lambda b, t: (b, 0, 0, t))
    grp_shape = lambda width: jax.ShapeDtypeStruct((B, G, T, width), BF16)
    grp_t_shape = lambda rows, dt: jax.ShapeDtypeStruct((B, G, rows, T), dt)
    out_shape = (
        jax.ShapeDtypeStruct((B, T, N_QR), BF16), jax.ShapeDtypeStruct((B, T, N_KR), BF16),
        jax.ShapeDtypeStruct((B, T, N_VR), BF16), jax.ShapeDtypeStruct((B, T, N_GR), BF16),
        jax.ShapeDtypeStruct((B, NSA_HEADS, T, NSA_HD), BF16),
        jax.ShapeDtypeStruct((2, B, G, T // CMP_STRIDE, CMP_STRIDE * NSA_HD), BF16),
        grp_shape(LANES), grp_t_shape(NSA_VT_ROWS, BF16), grp_shape(NSA_HD), grp_t_shape(NSA_VT_ROWS, BF16),
        grp_t_shape(LANES, F32),
        jax.ShapeDtypeStruct((B, T, D_MODEL), BF16), jax.ShapeDtypeStruct((B, T, D_MODEL), BF16))
    out_specs = (
        row(N_QR), row(N_KR), row(N_VR), row(N_GR),
        pl.BlockSpec((1, NSA_HEADS, tm, NSA_HD), lambda b, t: (b, 0, t, 0)),
        pl.BlockSpec((2, 1, G, tm // CMP_STRIDE, CMP_STRIDE * NSA_HD), lambda b, t: (0, b, 0, t, 0)),
        grp(LANES), grp_t(NSA_VT_ROWS), grp(NSA_HD), grp_t(NSA_VT_ROWS),
        grp_t(LANES),
        row(D_MODEL), row(D_MODEL))
    out_bytes = tm * (2 * (N_QR + N_KR + N_VR + N_GR + N_QN + 2 * D_MODEL) + 2 * G * (3 * NSA_HD + 3 * LANES)
                      + 4 * N_GATE)
    vmem = 2 * out_bytes + 2 * tm * D * 4 + 2 * D * P_END + tm * D * 2 + 10 * MIB
    return pl.pallas_call(
        _inproj_kernel,
        grid=(B, nt),
        in_specs=[pl.BlockSpec((1, tm, D), lambda b, t: (b, t, 0)),
                  pl.BlockSpec((1, D), lambda b, t: (0, 0)),
                  pl.BlockSpec((tm, RET_DK // 2), lambda b, t: (t, 0)),
                  pl.BlockSpec((tm, RET_DK // 2), lambda b, t: (t, 0)),
                  pl.BlockSpec((tm, LANES), lambda b, t: (t, 0)),
                  pl.BlockSpec((D, P_END), lambda b, t: (0, 0), pipeline_mode=pl.Buffered(1))],
        out_specs=out_specs,
        out_shape=out_shape,
        scratch_shapes=[pltpu.VMEM((tm, D), BF16), pltpu.VMEM((2, tm, G * NSA_HD), F32)],
        compiler_params=pltpu.CompilerParams(vmem_limit_bytes=int(vmem)),
        name="in_projection",
    )(x, g.reshape(1, D), cos, sin, feat, w_packed)


def _retention_kernel(q_ref, k_ref, v_ref, g_ref, dmask_ref, xi_ref, zeta_ref, gch_ref, o_ref):
    C = RET_BLOCK
    nc = q_ref.shape[1] // C
    dmask = dmask_ref[0]
    xi = xi_ref[0]
    zeta = zeta_ref[0]
    gch = gch_ref[0, :, 0:1]
    r = None
    for c in range(nc):
        rows = slice(c * C, (c + 1) * C)
        qc = q_ref[0, rows, :]
        kc = k_ref[0, rows, :]
        vc = v_ref[0, rows, :]
        if c < nc - 1:
            kz = (kc.astype(F32) * zeta).astype(BF16)
            kv = lax.dot_general(kz, vc, (((0,), (0,)), ((), ())), preferred_element_type=F32)
            r_next = kv if c == 0 else gch * r + kv
        s = _dot_nt(qc, kc) * dmask
        o = _dot(s.astype(BF16), vc)
        if c > 0:
            o = o + _dot(qc, r.astype(BF16)) * xi
        if c < nc - 1:
            r = r_next
        mu = jnp.mean(o, -1, keepdims=True)
        d = o - mu
        var = jnp.mean(d * d, -1, keepdims=True)
        on = d * lax.rsqrt(var + GN_EPS)
        o_ref[0, rows, :] = (g_ref[0, rows, :].astype(F32) * on).astype(BF16)


def _retention(qr, kr, vr, gr):
    B, T, _ = qr.shape
    C = RET_BLOCK
    lg = jnp.log(1.0 - 2.0 ** (-5.0 - jnp.arange(RET_HEADS, dtype=F32)))
    n = jnp.arange(C, dtype=F32)
    diff = n[:, None] - n[None, :]
    dmask = jnp.where(diff >= 0, jnp.exp(jnp.maximum(diff, 0.0)[None] * lg[:, None, None]), 0.0)
    xi = jnp.exp((n + 1.0)[None] * lg[:, None])[:, :, None]
    zeta = jnp.exp((C - 1.0 - n)[None] * lg[:, None])[:, :, None]
    gch = jnp.broadcast_to(jnp.exp(C * lg)[:, None, None], (RET_HEADS, 1, LANES))
    qk_spec = pl.BlockSpec((1, T, RET_DK), lambda b, h: (b, 0, h))
    v_spec = pl.BlockSpec((1, T, RET_DV), lambda b, h: (b, 0, h))
    return pl.pallas_call(
        _retention_kernel,
        grid=(B, RET_HEADS),
        in_specs=[qk_spec, qk_spec, v_spec, v_spec,
                  pl.BlockSpec((1, C, C), lambda b, h: (h, 0, 0)),
                  pl.BlockSpec((1, C, 1), lambda b, h: (h, 0, 0)),
                  pl.BlockSpec((1, C, 1), lambda b, h: (h, 0, 0)),
                  pl.BlockSpec((1, 1, LANES), lambda b, h: (h, 0, 0))],
        out_specs=v_spec,
        out_shape=jax.ShapeDtypeStruct((B, T, RET_HEADS * RET_DV), BF16),
        compiler_params=pltpu.CompilerParams(vmem_limit_bytes=32 * MIB),
        name="retention",
    )(qr, kr, vr, gr, dmask, xi, zeta, gch)


def _gelu_tanh(x):
    return 0.5 * x * (1.0 + jnp.tanh(math.sqrt(2.0 / math.pi) * (x + 0.044715 * (x * x * x))))


def _compress_kernel(x_ref, pk_ref, pv_ref, w1k_ref, w2k_ref, w1v_ref, w2vt_ref, ko_ref, vto_ref):
    half = CMP_STRIDE * NSA_HD

    def hidden(s, g, p_ref, w1_ref):
        x = x_ref[s, 0, g].astype(F32)
        a = _dot((x + p_ref[0:1, :]).astype(BF16), w1_ref[0:half, :])
        b = _dot((x + p_ref[1:2, :]).astype(BF16), w1_ref[half:2 * half, :])
        pre = a + pltpu.roll(b, b.shape[0] - 1, 0)
        return _gelu_tanh(pre).astype(BF16)

    for g in range(NSA_KV_GROUPS):
        ko_ref[0, g] = _dot(hidden(0, g, pk_ref, w1k_ref), w2k_ref[...]).astype(BF16)
        vto_ref[0, g] = _dot_nt(w2vt_ref[...], hidden(1, g, pv_ref, w1v_ref)).astype(BF16)


def _compress(kvc16, pos_k, pos_v, w1k, w2k, w1v, w2v):
    _, B, G, NC, F = kvc16.shape
    full = lambda a: pl.BlockSpec(a.shape, lambda b: (0,) * a.ndim)
    pk = pos_k.reshape(2, F)
    pv = pos_v.reshape(2, F)
    w = [a.astype(BF16) for a in (w1k, w2k, w1v, w2v.T)]
    return pl.pallas_call(
        _compress_kernel,
        grid=(B,),
        in_specs=[pl.BlockSpec((2, 1, G, NC, F), lambda b: (0, b, 0, 0, 0)), full(pk), full(pv)]
        + [full(a) for a in w],
        out_specs=(pl.BlockSpec((1, G, NC, NSA_HD), lambda b: (b, 0, 0, 0)),
                   pl.BlockSpec((1, G, NSA_HD, NC), lambda b: (b, 0, 0, 0))),
        out_shape=(jax.ShapeDtypeStruct((B, G, NC, NSA_HD), BF16), jax.ShapeDtypeStruct((B, G, NSA_HD, NC), BF16)),
        compiler_params=pltpu.CompilerParams(vmem_limit_bytes=32 * MIB),
        name="nsa_compress",
    )(kvc16, pk, pv, *w)


def _bias_kernel(tab_ref, nb_ref, cb_ref, near_o, cmp_o):
    h = pl.program_id(0)
    far = tab_ref[REL_BUCKETS - 1, h]

    def build(bk):
        acc = jnp.zeros(bk.shape, F32)
        for k in range(REL_BUCKETS - 1):
            acc = jnp.where(bk == k, (tab_ref[k, h] - far) * LOG2E, acc)
        return jnp.where(bk == MASKED_CODE, NEG_INF, acc)

    near_o[0] = build(nb_ref[...])
    cmp_o[0] = build(cb_ref[...])


def _t5_bucket(dist):
    dist = jnp.maximum(dist, 0)
    max_exact = REL_BUCKETS // 2
    large = max_exact + (jnp.log(jnp.maximum(dist, 1).astype(F32) / max_exact)
                         / math.log(REL_MAX_DIST / max_exact) * (REL_BUCKETS - max_exact)).astype(jnp.int32)
    large = jnp.minimum(large, REL_BUCKETS - 1)
    return jnp.where(dist < max_exact, dist, large)


def _bias_tables(rel_bias, T, tq):
    nc = T // CMP_STRIDE
    i = jnp.arange(tq, dtype=jnp.int32)
    d = jnp.asarray(NEAR_OFFSETS, jnp.int32)[:, None, None] * tq + i[None, None, :] - i[None, :, None]
    near_b = jnp.where((d < 0) | (d >= WINDOW), MASKED_CODE, _t5_bucket(d))
    near_b = jnp.concatenate([near_b, jnp.full((1, tq, tq), MASKED_CODE, jnp.int32)], axis=0)
    cmp_end = jnp.arange(nc, dtype=jnp.int32) * CMP_STRIDE + CMP_LEN - 1
    cmp_b = _t5_bucket(jnp.arange(T, dtype=jnp.int32)[None, :] - cmp_end[:, None])
    return pl.pallas_call(
        _bias_kernel,
        grid=(NSA_HEADS,),
        in_specs=[pl.BlockSpec(memory_space=pltpu.SMEM),
                  pl.BlockSpec(near_b.shape, lambda h: (0, 0, 0)),
                  pl.BlockSpec(cmp_b.shape, lambda h: (0, 0))],
        out_specs=(pl.BlockSpec((1,) + near_b.shape, lambda h: (h, 0, 0, 0)),
                   pl.BlockSpec((1,) + cmp_b.shape, lambda h: (h, 0, 0))),
        out_shape=(jax.ShapeDtypeStruct((NSA_HEADS,) + near_b.shape, F32),
                   jax.ShapeDtypeStruct((NSA_HEADS,) + cmp_b.shape, F32)),
        name="rel_bias_tables",
    )(rel_bias, near_b, cmp_b)


def _nsa_kernel(q_ref, kc_ref, vct_ref, ks_ref, vst_ref, kw_ref, vwt_ref, gt_ref,
                near_ref, cb_ref, ovl_ref, o_ref,
                qa_scr, sa_scr, sb_scr, cma_scr, cmb_scr, ms_scr, mw_scr, accs_scr, accw_scr, *, tq, sel_k):
    HG = NSA_HG
    NC = kc_ref.shape[2]
    NS = ovl_ref.shape[0]
    UW = tq
    ti = pl.program_id(2)
    s0 = ti * tq

    def head(x, h):
        return x[:, h * tq:(h + 1) * tq]

    def keys(ref, u, n=1):
        return ref[0, 0, pl.ds(pl.multiple_of(u * UW, UW), n * UW), :]

    def values_t(ref, u, n=1):
        return ref[0, 0, :, pl.ds(pl.multiple_of(u * UW, UW), n * UW)]

    cmax = lambda s_ref: cma_scr if s_ref is sa_scr else cmb_scr

    def score(s_ref, k_ref, u, n, queries, entries=None):
        def produce(h):
            s = _dot_nt(keys(k_ref, u, n), queries(h))
            if entries is not None:
                s = jnp.concatenate([s[i * UW:(i + 1) * UW] + near_ref[h, e] for i, e in enumerate(entries)], axis=0)
            s_ref[0:n * UW, h * tq:(h + 1) * tq] = s
            groups = [s[r * SUBLANES:(r + 1) * SUBLANES] for r in range(n * UW // SUBLANES)]
            cmax(s_ref)[:, h * tq:(h + 1) * tq] = functools.reduce(jnp.maximum, groups)
        return produce

    def fold(s_ref, n, m_ref, vt_ref, u, acc_ref, first=False):
        def consume(h):
            c = slice(h * tq, (h + 1) * tq)
            m_new = jnp.max(cmax(s_ref)[:, c], axis=0, keepdims=True)
            if not first:
                m_old = m_ref[0:1, c]
                m_new = jnp.maximum(m_old, m_new)
            part = [_dot(values_t(vt_ref, u + i), jnp.exp2((s_ref[i * UW:(i + 1) * UW, c] - m_new).astype(BF16)))
                    for i in range(n)]
            part = functools.reduce(jnp.add, part)
            acc_ref[:, c] = part if first else jnp.exp2(m_old - m_new) * acc_ref[:, c] + part
            m_ref[:, c] = jnp.broadcast_to(m_new, (SUBLANES, tq))
        return consume

    def stage(produce=None, consume=None):
        for h in range(HG):
            if produce is not None:
                produce(h)
            if consume is not None:
                consume(h)

    q_head = lambda h: q_ref[0, 0, h]
    qa_head = lambda h: qa_scr[h * tq:(h + 1) * tq, :]

    def normalized(acc_ref):
        acc = acc_ref[...]
        return acc[:NSA_HD] / acc[NSA_HD:NSA_HD + 1]

    masked = len(NEAR_OFFSETS)
    w0 = jnp.maximum(ti - 2, 0)

    def window_entry(slot):
        back = jnp.minimum(ti, 2) - slot
        return jnp.where(back >= 0, back, masked)

    win_bufs = (sa_scr, sb_scr, sa_scr)
    win_slots = (1, 0, 2)
    win_score = [score(buf, kw_ref, w0 + slot, 1, q_head, (window_entry(slot),))
                 for buf, slot in zip(win_bufs, win_slots)]
    win_fold = [fold(buf, 1, mw_scr, vwt_ref, w0 + slot, accw_scr, first=(j == 0))
                for j, (buf, slot) in enumerate(zip(win_bufs, win_slots))]

    stage(win_score[0])

    pos_c = s0 + lax.broadcasted_iota(jnp.int32, (NC, tq), 1)
    cend = lax.broadcasted_iota(jnp.int32, (NC, tq), 0) * CMP_STRIDE + (CMP_LEN - 1)
    valid_c = cend <= pos_c
    valid_f = valid_c.astype(F32)
    tile_cols = pl.ds(pl.multiple_of(s0, tq), tq)
    lc = [_dot_nt(kc_ref[0, 0], q_head(h)) + cb_ref[h, :, tile_cols] for h in range(HG)]
    pc = []
    for h in range(HG):
        z = jnp.where(valid_c, lc[h], NEG_INF)
        e = jnp.exp2(z - jnp.max(z, 0, keepdims=True)) * valid_f
        pc.append(e / jnp.maximum(jnp.sum(e, 0, keepdims=True), 1e-30))
    oc = [_dot(vct_ref[0, 0], pc[h].astype(BF16)) for h in range(HG)]
    psum = functools.reduce(jnp.add, pc)

    p_hi = psum.astype(BF16)
    r1 = psum - p_hi.astype(F32)
    p_mid = r1.astype(BF16)
    p_lo = (r1 - p_mid.astype(F32)).astype(BF16)
    ovl = ovl_ref[...]
    imp = _dot(ovl, p_hi) + _dot(ovl, p_mid) + _dot(ovl, p_lo)
    jrow = lax.broadcasted_iota(jnp.int32, (NS, tq), 0)
    posl = s0 + lax.broadcasted_iota(jnp.int32, (NS, tq), 1)
    cur = lax.shift_right_logical(posl, int(math.log2(SEL_BLOCK)))
    forced = (jrow == 0) | (jrow == cur) | (jrow == cur - 1)
    started = jrow * SEL_BLOCK <= posl
    imp = jnp.where(forced, FORCED_SCORE, imp)
    imp = jnp.where(started, imp, NEG_INF)
    sel = jnp.zeros((NS, tq), F32)
    for it in range(max(sel_k, HG)):
        if it < sel_k:
            mx = jnp.max(imp, axis=0, keepdims=True)
            first = jnp.min(jnp.where(imp == mx, jrow, NS), axis=0, keepdims=True)
            hit = jrow == first
            sel = jnp.where(hit, 1.0, sel)
            imp = jnp.where(hit, PICKED, imp)
        if it < HG:
            win_score[1](it)
            win_fold[0](it)
    stage(win_score[2], win_fold[1])
    notsel = 1.0 - jnp.where(started, sel, 0.0)
    notsel = jnp.concatenate([notsel, jnp.zeros((LANES - NS, tq), F32)], axis=0)
    notsel = pltpu.roll(notsel.T, NSA_HD, 1).astype(BF16)
    for hg in range(HG):
        qa_scr[hg * tq:(hg + 1) * tq, :NSA_HD] = q_ref[0, 0, hg]
        qa_scr[hg * tq:(hg + 1) * tq, NSA_HD:] = notsel[:, NSA_HD:]

    prev_u = jnp.where(ti >= 1, ti - 1, 1)
    sel_job = lambda s_ref, u, n: (s_ref, n, ms_scr, vst_ref, u, accs_scr)
    stage(score(sb_scr, ks_ref, ti, 1, qa_head, (0,)), win_fold[2])
    stage(score(sa_scr, ks_ref, prev_u, 1, qa_head, (jnp.where(ti >= 1, 1, masked),)), fold(*sel_job(sb_scr, ti, 1), first=True))
    stage(None, fold(*sel_job(sa_scr, prev_u, 1)))

    n_plain = jnp.maximum(ti - 1, 0)
    n_quads = lax.shift_right_logical(n_plain, 2)

    def plain_run(first, count):
        bufs = (sa_scr, sb_scr)
        for i in range(count + 1):
            produce = score(bufs[i % 2], ks_ref, first + i, 1, qa_head) if i < count else None
            consume = fold(*sel_job(bufs[(i - 1) % 2], first + i - 1, 1)) if i > 0 else None
            stage(produce, consume)

    def plain_quad(j, carry):
        plain_run(4 * j, 4)
        return carry

    lax.fori_loop(0, n_quads, plain_quad, 0)
    left = n_plain - 4 * n_quads

    @pl.when(left >= 2)
    def _():
        plain_run(4 * n_quads, 2)

    @pl.when(lax.rem(left, 2) == 1)
    def _():
        stage(score(sa_scr, ks_ref, n_plain - 1, 1, qa_head))
        stage(None, fold(*sel_job(sa_scr, n_plain - 1, 1)))

    gt = gt_ref[0, 0]
    o_s = normalized(accs_scr)
    o_w = normalized(accw_scr)

    def gated(h):
        return (gt[3 * h:3 * h + 1] * oc[h] + gt[3 * h + 1:3 * h + 2] * head(o_s, h)
                + gt[3 * h + 2:3 * h + 3] * head(o_w, h))

    pairs = [jnp.concatenate([gated(2 * j), gated(2 * j + 1)], axis=0).T for j in range(HG // 2)]
    o_ref[0] = jnp.concatenate(pairs, axis=1).astype(BF16)


def _nsa_attention(qn, kc, vct, ks, vst, kw, vwt, gates_t, near, cmpb):
    B, G, HG, T, hd = qn.shape
    tq = TQ
    assert T % tq == 0 and T >= 3 * tq and WINDOW == 2 * tq and tq >= 2 * REL_MAX_DIST and tq % LANES == 0
    nq = T // tq
    NC = kc.shape[2]
    NS = T // SEL_BLOCK
    sel_k = min(SEL_TOPK, NS)
    n = np.arange(NC)
    j = np.arange(NS)
    ovl = ((n[None, :] * CMP_STRIDE < (j[:, None] + 1) * SEL_BLOCK)
           & (n[None, :] * CMP_STRIDE + CMP_LEN - 1 >= j[:, None] * SEL_BLOCK) & (n[None, :] < NC - 1))
    ovl = jnp.asarray(ovl, BF16)
    seq = lambda rows, width: pl.BlockSpec((1, 1, rows, width), lambda g, b, t: (b, g, 0, 0))
    R = HG * tq
    W = HG * hd
    return pl.pallas_call(
        functools.partial(_nsa_kernel, tq=tq, sel_k=sel_k),
        grid=(G, B, nq),
        in_specs=[pl.BlockSpec((1, 1, HG, tq, hd), lambda g, b, t: (b, g, 0, t, 0)),
                  seq(NC, hd), seq(hd, NC), seq(T, LANES), seq(NSA_VT_ROWS, T), seq(T, hd), seq(NSA_VT_ROWS, T),
                  pl.BlockSpec((1, 1, LANES, tq), lambda g, b, t: (b, g, 0, t)),
                  pl.BlockSpec((HG, len(NEAR_OFFSETS) + 1, tq, tq), lambda g, b, t: (g, 0, 0, 0),
                               pipeline_mode=pl.Buffered(1)),
                  pl.BlockSpec((HG, NC, T), lambda g, b, t: (g, 0, 0), pipeline_mode=pl.Buffered(1)),
                  pl.BlockSpec(ovl.shape, lambda g, b, t: (0, 0))],
        out_specs=pl.BlockSpec((1, tq, W), lambda g, b, t: (b, t, g)),
        out_shape=jax.ShapeDtypeStruct((B, T, G * W), BF16),
        scratch_shapes=[pltpu.VMEM((R, LANES), BF16),
                        pltpu.VMEM((2 * tq, R), F32), pltpu.VMEM((2 * tq, R), F32),
                        pltpu.VMEM((SUBLANES, R), F32), pltpu.VMEM((SUBLANES, R), F32),
                        pltpu.VMEM((SUBLANES, R), F32), pltpu.VMEM((SUBLANES, R), F32),
                        pltpu.VMEM((NSA_VT_ROWS, R), F32), pltpu.VMEM((NSA_VT_ROWS, R), F32)],
        compiler_params=pltpu.CompilerParams(vmem_limit_bytes=NSA_VMEM_LIMIT),
        name="nsa_attention",
    )(qn, kc, vct, ks, vst, kw, vwt, gates_t, near, cmpb, ovl)


def _merge_kernel(x_ref, oret_ref, on_ref, ma_ref, mb_ref, wr_ref, wn_ref, wo_ref, o_ref):
    y_ret = _dot(oret_ref[...], wr_ref[...])
    y_nsa = _dot(on_ref[...], wn_ref[...])
    mixed = ma_ref[...].astype(F32) * y_ret + mb_ref[...].astype(F32) * y_nsa
    o_ref[...] = x_ref[...] + _dot(mixed.astype(BF16), wo_ref[...])


def _merge(x2, oret, on, ma, mb, w_o_ret, w_o_nsa, w_out):
    M, D = x2.shape
    tm = min(TM_MERGE, M)
    row = lambda width: pl.BlockSpec((tm, width), lambda i: (i, 0))
    res = lambda a: pl.BlockSpec(a.shape, lambda i: (0, 0), pipeline_mode=pl.Buffered(1))
    w = [a.astype(BF16) for a in (w_o_ret, w_o_nsa, w_out)]
    return pl.pallas_call(
        _merge_kernel,
        grid=(M // tm,),
        in_specs=[row(D), row(oret.shape[1]), row(D), row(D), row(D)] + [res(a) for a in w],
        out_specs=row(D),
        out_shape=jax.ShapeDtypeStruct((M, D), F32),
        compiler_params=pltpu.CompilerParams(vmem_limit_bytes=48 * MIB),
        name="merge_out_projection",
    )(x2, oret, on, ma, mb, *w)


def _rms(x, g):
    return x * lax.rsqrt(jnp.mean(x * x, -1, keepdims=True) + EPS) * g


def _ffn_kernel(x_ref, g_ref, wi_ref, wo_ref, gf_ref, o_ref, h_scr, acc_scr, *, final_norm):
    x = x_ref[...]
    h_scr[...] = _rms(x, g_ref[...]).astype(BF16)
    acc_scr[...] = x
    cw = FFN_HIDDEN // 2
    for c in range(2):
        a = _dot(h_scr[...], wi_ref[:, c * cw:(c + 1) * cw])
        b = _dot(h_scr[...], wi_ref[:, FFN_HIDDEN + c * cw:FFN_HIDDEN + (c + 1) * cw])
        acc_scr[...] += _dot((_silu(a) * b).astype(BF16), wo_ref[c * cw:(c + 1) * cw, :])
    y = acc_scr[...]
    o_ref[...] = _rms(y, gf_ref[...]) if final_norm else y


def _ffn(x2, g, w_in, w_out, g_final, final_norm):
    M, D = x2.shape
    tm = min(TM_MERGE, M)
    row = pl.BlockSpec((tm, D), lambda i: (i, 0))
    vec = pl.BlockSpec((1, D), lambda i: (0, 0))
    res = lambda a: pl.BlockSpec(a.shape, lambda i: (0, 0), pipeline_mode=pl.Buffered(1))
    wi, wo = w_in.astype(BF16), w_out.astype(BF16)
    return pl.pallas_call(
        functools.partial(_ffn_kernel, final_norm=final_norm),
        grid=(M // tm,),
        in_specs=[row, vec, res(wi), res(wo), vec],
        out_specs=row,
        out_shape=jax.ShapeDtypeStruct((M, D), F32),
        scratch_shapes=[pltpu.VMEM((tm, D), BF16), pltpu.VMEM((tm, D), F32)],
        compiler_params=pltpu.CompilerParams(vmem_limit_bytes=52 * MIB),
        name="swiglu_ffn",
    )(x2, g.reshape(1, D), wi, wo, g_final.reshape(1, D))


def kernel(x, norm_mix_g, w_in, cmp_pos_k, cmp_pos_v, cmp_w1_k, cmp_w2_k, cmp_w1_v, cmp_w2_v, w_o_ret, w_o_nsa,
           w_out, norm_ffn_g, w_ffn_in, w_ffn_out, rel_bias, norm_final_g):
    B, T, D = x.shape
    depth = w_in.shape[0]
    G, HG = NSA_KV_GROUPS, NSA_HG
    half = RET_DK // 2
    freqs = ROPE_BASE ** (-jnp.arange(half, dtype=F32) / half)
    ang = jnp.arange(T, dtype=jnp.int32).astype(F32)[:, None] * freqs
    cos, sin = jnp.cos(ang), jnp.sin(ang)
    near, cmpb = _bias_tables(rel_bias, T, TQ)
    for i in range(depth):
        qr, kr, vr, gr, qn, kvc16, ks, vst, kw, vwt, gates_t, ma, mb = _in_projection(
            x, norm_mix_g[i], _pack_w_in(w_in[i]), cos, sin)
        o_ret = _retention(qr, kr, vr, gr)
        kc, vct = _compress(kvc16,cmp_pos_k[i], cmp_pos_v[i], cmp_w1_k[i], cmp_w2_k[i], cmp_w1_v[i], cmp_w2_v[i])
        o_n = _nsa_attention(qn.reshape(B, G, HG, T, NSA_HD), kc, vct, ks, vst, kw, vwt, gates_t, near, cmpb)
        o_n = o_n.reshape(B * T, NSA_HEADS * NSA_HD)
        x2 = _merge(x.reshape(B * T, D), o_ret.reshape(B * T, -1), o_n, ma.reshape(B * T, D), mb.reshape(B * T, D),
                    w_o_ret[i], w_o_nsa[i], w_out[i])
        x2 = _ffn(x2, norm_ffn_g[i], w_ffn_in[i], w_ffn_out[i], norm_final_g, final_norm=(i == depth - 1))
        x = x2.reshape(B, T, D)
    return x
```

```python
import functools
import math

import jax
import jax.numpy as jnp
import numpy as np
from jax import lax
from jax.experimental import pallas as pl
from jax.experimental.pallas import tpu as pltpu

F32 = jnp.float32
BF16 = jnp.bfloat16

D_MODEL = 1024
RET_HEADS = 4
RET_DK = 256
RET_DV = 512
RET_BLOCK = 256
ROPE_BASE = 10000.0
GN_EPS = 1e-5
NSA_HEADS = 16
NSA_KV_GROUPS = 2
NSA_HG = NSA_HEADS // NSA_KV_GROUPS
NSA_HD = 64
CMP_LEN = 32
CMP_STRIDE = 16
SEL_BLOCK = 64
SEL_TOPK = 8
WINDOW = 512
FORCED_SCORE = 1e4
REL_BUCKETS = 32
REL_MAX_DIST = 128
FFN_HIDDEN = -(-8 * D_MODEL // (3 * 256)) * 256
EPS = 1e-6
NEG_INF = -1e30
PICKED = -3e38
MASKED_CODE = REL_BUCKETS
NEAR_OFFSETS = (0, 1, 2)
LOG2E = math.log2(math.e)

LANES = 128
SUBLANES = 8
MIB = 1024 * 1024
NSA_VMEM_LIMIT = 56 * MIB
NSA_VT_ROWS = NSA_HD + 16

_SEG = np.cumsum([0, RET_HEADS * RET_DK, RET_HEADS * RET_DK, RET_HEADS * RET_DV, RET_HEADS * RET_DV,
                  NSA_HEADS * NSA_HD, 6 * NSA_KV_GROUPS * NSA_HD, 3 * NSA_HEADS, D_MODEL, D_MODEL])
N_QR, N_KR, N_VR, N_GR = RET_HEADS * RET_DK, RET_HEADS * RET_DK, RET_HEADS * RET_DV, RET_HEADS * RET_DV
N_QN, N_KV = NSA_HEADS * NSA_HD, 6 * NSA_KV_GROUPS * NSA_HD
N_GATE = NSA_KV_GROUPS * LANES
_P = np.cumsum([0, N_QR, N_KR, N_VR, N_GR, N_QN, N_KV, N_GATE, D_MODEL, D_MODEL])
P_QR, P_KR, P_VR, P_GR, P_QN, P_KV, P_GATE, P_MA, P_MB, P_END = [int(v) for v in _P]

TM_PROJ = 512
TM_MERGE = 512
TQ = 256
NT = (((1,), (1,)), ((), ()))


def _dot(a, b):
    return jnp.dot(a, b, preferred_element_type=F32)


def _dot_nt(a, b):
    return lax.dot_general(a, b, NT, preferred_element_type=F32)


def _sigmoid(x):
    return 1.0 / (1.0 + jnp.exp(-x))


def _silu(x):
    return x * _sigmoid(x)


def _inproj_kernel(x_ref, g_ref, cos_ref, sin_ref, feat_ref, w_ref,
                   qr_o, kr_o, vr_o, gr_o, qn_o, kvc_o, ks_o, vst_o, kw_o, vwt_o, gate_o, ma_o, mb_o, h_scr, kvc_scr):
    x = x_ref[0]
    h = x * lax.rsqrt(jnp.mean(x * x, -1, keepdims=True) + EPS) * g_ref[...]
    h_scr[...] = h.astype(BF16)
    cos = cos_ref[...]
    sin = sin_ref[...]
    half = RET_DK // 2

    def proj(lo, width):
        return _dot(h_scr[...], w_ref[:, lo:lo + width])

    y32 = proj(P_KV, N_KV)
    y = y32.astype(BF16)
    y_t = y32.T.astype(BF16)
    tm = y.shape[0]
    ones_row = (lax.broadcasted_iota(jnp.int32, (NSA_VT_ROWS - NSA_HD, tm), 0) == 0).astype(BF16)
    for g in range(NSA_KV_GROUPS):
        lo = lambda j: (j * NSA_KV_GROUPS + g) * NSA_HD
        ks_o[0, g, :, :NSA_HD] = y[:, lo(2):lo(2) + NSA_HD]
        ks_o[0, g, :, NSA_HD:] = feat_ref[:, NSA_HD:]
        kw_o[0, g] = y[:, lo(4):lo(4) + NSA_HD]
        for out, j in ((vst_o, 3), (vwt_o, 5)):
            out[0, g, :NSA_HD, :] = y_t[lo(j):lo(j) + NSA_HD, :]
            out[0, g, NSA_HD:, :] = ones_row
    for s in range(2):
        kvc_scr[s] = y32[:, s * LANES:(s + 1) * LANES]
        for l in range(CMP_STRIDE):
            rows = kvc_scr[s, pl.ds(l, tm // CMP_STRIDE, stride=CMP_STRIDE), :].astype(BF16)
            for g in range(NSA_KV_GROUPS):
                kvc_o[s, 0, g, :, l * NSA_HD:(l + 1) * NSA_HD] = rows[:, g * NSA_HD:(g + 1) * NSA_HD]
    gates_t = _sigmoid(proj(P_GATE, N_GATE)).T
    for g in range(NSA_KV_GROUPS):
        gate_o[0, g] = gates_t[g * LANES:(g + 1) * LANES, :]
    cw = 512
    per = cw // NSA_HD
    for c in range(N_QN // cw):
        y = (proj(P_QN + c * cw, cw) * (NSA_HD ** -0.5 * LOG2E)).astype(BF16)
        for j in range(per):
            qn_o[0, c * per + j] = y[:, j * NSA_HD:(j + 1) * NSA_HD]
    for hd in range(RET_HEADS):
        for out, base, scale in ((qr_o, P_QR, 1.0), (kr_o, P_KR, RET_DK ** -0.5)):
            y = proj(base + hd * RET_DK, RET_DK)
            x1, x2 = y[:, :half], y[:, half:]
            out[0, :, hd * RET_DK:hd * RET_DK + half] = ((x1 * cos - x2 * sin) * scale).astype(BF16)
            out[0, :, hd * RET_DK + half:(hd + 1) * RET_DK] = ((x1 * sin + x2 * cos) * scale).astype(BF16)
    for c in range(N_VR // cw):
        vr_o[0, :, c * cw:(c + 1) * cw] = proj(P_VR + c * cw, cw).astype(BF16)
        gr_o[0, :, c * cw:(c + 1) * cw] = _silu(proj(P_GR + c * cw, cw)).astype(BF16)
    for c in range(D_MODEL // cw):
        ma_o[0, :, c * cw:(c + 1) * cw] = _sigmoid(proj(P_MA + c * cw, cw)).astype(BF16)
        mb_o[0, :, c * cw:(c + 1) * cw] = _sigmoid(proj(P_MB + c * cw, cw)).astype(BF16)


def _pack_w_in(w):
    s = [int(v) for v in _SEG]
    gate = w[:, s[6]:s[7]].reshape(D_MODEL, NSA_KV_GROUPS, 3 * NSA_HG)
    gate = jnp.pad(gate, ((0, 0), (0, 0), (0, LANES - 3 * NSA_HG))).reshape(D_MODEL, N_GATE)
    return jnp.concatenate([w[:, s[0]:s[6]], gate, w[:, s[7]:s[9]]], axis=1).astype(BF16)


def _in_projection(x, g, w_packed, cos, sin):
    B, T, D = x.shape
    G = NSA_KV_GROUPS
    tm = min(TM_PROJ, T)
    nt = T // tm
    assert T // SEL_BLOCK <= LANES - NSA_HD
    own_block = (np.arange(T)[:, None] // SEL_BLOCK) == (np.arange(LANES)[None, :] - NSA_HD)
    feat = jnp.asarray(np.where(own_block, NEG_INF, 0.0), BF16)
    row = lambda width: pl.BlockSpec((1, tm, width), lambda b, t: (b, t, 0))
    grp = lambda width: pl.BlockSpec((1, G, tm, width), lambda b, t: (b, 0, t, 0))
    grp_t = lambda rows: pl.BlockSpec((1, G, rows, tm), lambda b, t: (b, 0, 0, t))
    grp_shape = lambda width: jax.ShapeDtypeStruct((B, G, T, width), BF16)
    grp_t_shape = lambda rows, dt: jax.ShapeDtypeStruct((B, G, rows, T), dt)
    out_shape = (
        jax.ShapeDtypeStruct((B, T, N_QR), BF16), jax.ShapeDtypeStruct((B, T, N_KR), BF16),
        jax.ShapeDtypeStruct((B, T, N_VR), BF16), jax.ShapeDtypeStruct((B, T, N_GR), BF16),
        jax.ShapeDtypeStruct((B, NSA_HEADS, T, NSA_HD), BF16),
        jax.ShapeDtypeStruct((2, B, G, T // CMP_STRIDE, CMP_STRIDE * NSA_HD), BF16),
        grp_shape(LANES), grp_t_shape(NSA_VT_ROWS, BF16), grp_shape(NSA_HD), grp_t_shape(NSA_VT_ROWS, BF16),
        grp_t_shape(LANES, F32),
        jax.ShapeDtypeStruct((B, T, D_MODEL), BF16), jax.ShapeDtypeStruct((B, T, D_MODEL), BF16))
    out_specs = (
        row(N_QR), row(N_KR), row(N_VR), row(N_GR),
        pl.BlockSpec((1, NSA_HEADS, tm, NSA_HD), lambda b, t: (b, 0, t, 0)),
        pl.BlockSpec((2, 1, G, tm // CMP_STRIDE, CMP_STRIDE * NSA_HD), lambda b, t: (0, b, 0, t, 0)),
        grp(LANES), grp_t(NSA_VT_ROWS), grp(NSA_HD), grp_t(NSA_VT_ROWS),
        grp_t(LANES),
        row(D_MODEL), row(D_MODEL))
    out_bytes = tm * (2 * (N_QR + N_KR + N_VR + N_GR + N_QN + 2 * D_MODEL) + 2 * G * (3 * NSA_HD + 3 * LANES)
                      + 4 * N_GATE)
    vmem = 2 * out_bytes + 2 * tm * D * 4 + 2 * D * P_END + tm * D * 2 + 10 * MIB
    return pl.pallas_call(
        _inproj_kernel,
        grid=(B, nt),
        in_specs=[pl.BlockSpec((1, tm, D), lambda b, t: (b, t, 0)),
                  pl.BlockSpec((1, D), lambda b, t: (0, 0)),
                  pl.BlockSpec((tm, RET_DK // 2), lambda b, t: (t, 0)),
                  pl.BlockSpec((tm, RET_DK // 2), lambda b, t: (t, 0)),
                  pl.BlockSpec((tm, LANES), lambda b, t: (t, 0)),
                  pl.BlockSpec((D, P_END), lambda b, t: (0, 0), pipeline_mode=pl.Buffered(1))],
        out_specs=out_specs,
        out_shape=out_shape,
        scratch_shapes=[pltpu.VMEM((tm, D), BF16), pltpu.VMEM((2, tm, G * NSA_HD), F32)],
        compiler_params=pltpu.CompilerParams(vmem_limit_bytes=int(vmem)),
        name="in_projection",
    )(x, g.reshape(1, D), cos, sin, feat, w_packed)


def _retention_kernel(q_ref, k_ref, v_ref, g_ref, dmask_ref, xi_ref, zeta_ref, gch_ref, o_ref):
    C = RET_BLOCK
    nc = q_ref.shape[1] // C
    dmask = dmask_ref[0]
    xi = xi_ref[0]
    zeta = zeta_ref[0]
    gch = gch_ref[0, :, 0:1]
    r = jnp.zeros((RET_DK, RET_DV), F32)
    for c in range(nc):
        rows = slice(c * C, (c + 1) * C)
        qc = q_ref[0, rows, :]
        kc = k_ref[0, rows, :]
        vc = v_ref[0, rows, :]
        kz = (kc.astype(F32) * zeta).astype(BF16)
        r_next = gch * r + lax.dot_general(kz, vc, (((0,), (0,)), ((), ())), preferred_element_type=F32)
        s = _dot_nt(qc, kc) * dmask
        o = _dot(s.astype(BF16), vc) + _dot(qc, r.astype(BF16)) * xi
        r = r_next
        mu = jnp.mean(o, -1, keepdims=True)
        d = o - mu
        var = jnp.mean(d * d, -1, keepdims=True)
        on = d * lax.rsqrt(var + GN_EPS)
        o_ref[0, rows, :] = (g_ref[0, rows, :].astype(F32) * on).astype(BF16)


def _retention(qr, kr, vr, gr):
    B, T, _ = qr.shape
    C = RET_BLOCK
    lg = jnp.log(1.0 - 2.0 ** (-5.0 - jnp.arange(RET_HEADS, dtype=F32)))
    n = jnp.arange(C, dtype=F32)
    diff = n[:, None] - n[None, :]
    dmask = jnp.where(diff >= 0, jnp.exp(jnp.maximum(diff, 0.0)[None] * lg[:, None, None]), 0.0)
    xi = jnp.exp((n + 1.0)[None] * lg[:, None])[:, :, None]
    zeta = jnp.exp((C - 1.0 - n)[None] * lg[:, None])[:, :, None]
    gch = jnp.broadcast_to(jnp.exp(C * lg)[:, None, None], (RET_HEADS, 1, LANES))
    qk_spec = pl.BlockSpec((1, T, RET_DK), lambda b, h: (b, 0, h))
    v_spec = pl.BlockSpec((1, T, RET_DV), lambda b, h: (b, 0, h))
    return pl.pallas_call(
        _retention_kernel,
        grid=(B, RET_HEADS),
        in_specs=[qk_spec, qk_spec, v_spec, v_spec,
                  pl.BlockSpec((1, C, C), lambda b, h: (h, 0, 0)),
                  pl.BlockSpec((1, C, 1), lambda b, h: (h, 0, 0)),
                  pl.BlockSpec((1, C, 1), lambda b, h: (h, 0, 0)),
                  pl.BlockSpec((1, 1, LANES), lambda b, h: (h, 0, 0))],
        out_specs=v_spec,
        out_shape=jax.ShapeDtypeStruct((B, T, RET_HEADS * RET_DV), BF16),
        compiler_params=pltpu.CompilerParams(vmem_limit_bytes=32 * MIB),
        name="retention",
    )(qr, kr, vr, gr, dmask, xi, zeta, gch)


def _gelu_tanh(x):
    return 0.5 * x * (1.0 + jnp.tanh(math.sqrt(2.0 / math.pi) * (x + 0.044715 * (x * x * x))))


def _compress_kernel(x_ref, pk_ref, pv_ref, w1k_ref, w2k_ref, w1v_ref, w2vt_ref, ko_ref, vto_ref):
    half = CMP_STRIDE * NSA_HD

    def hidden(s, g, p_ref, w1_ref):
        x = x_ref[s, 0, g].astype(F32)
        a = _dot((x + p_ref[0:1, :]).astype(BF16), w1_ref[0:half, :])
        b = _dot((x + p_ref[1:2, :]).astype(BF16), w1_ref[half:2 * half, :])
        pre = a + pltpu.roll(b, b.shape[0] - 1, 0)
        return _gelu_tanh(pre).astype(BF16)

    for g in range(NSA_KV_GROUPS):
        ko_ref[0, g] = _dot(hidden(0, g, pk_ref, w1k_ref), w2k_ref[...]).astype(BF16)
        vto_ref[0, g] = _dot_nt(w2vt_ref[...], hidden(1, g, pv_ref, w1v_ref)).astype(BF16)


def _compress(kvc16, pos_k, pos_v, w1k, w2k, w1v, w2v):
    _, B, G, NC, F = kvc16.shape
    full = lambda a: pl.BlockSpec(a.shape, lambda b: (0,) * a.ndim)
    pk = pos_k.reshape(2, F)
    pv = pos_v.reshape(2, F)
    w = [a.astype(BF16) for a in (w1k, w2k, w1v, w2v.T)]
    return pl.pallas_call(
        _compress_kernel,
        grid=(B,),
        in_specs=[pl.BlockSpec((2, 1, G, NC, F), lambda b: (0, b, 0, 0, 0)), full(pk), full(pv)]
        + [full(a) for a in w],
        out_specs=(pl.BlockSpec((1, G, NC, NSA_HD), lambda b: (b, 0, 0, 0)),
                   pl.BlockSpec((1, G, NSA_HD, NC), lambda b: (b, 0, 0, 0))),
        out_shape=(jax.ShapeDtypeStruct((B, G, NC, NSA_HD), BF16), jax.ShapeDtypeStruct((B, G, NSA_HD, NC), BF16)),
        compiler_params=pltpu.CompilerParams(vmem_limit_bytes=32 * MIB),
        name="nsa_compress",
    )(kvc16, pk, pv, *w)


def _bias_kernel(tab_ref, nb_ref, cb_ref, near_o, cmp_o):
    h = pl.program_id(0)
    far = tab_ref[REL_BUCKETS - 1, h]

    def build(bk):
        acc = jnp.zeros(bk.shape, F32)
        for k in range(REL_BUCKETS - 1):
            acc = jnp.where(bk == k, (tab_ref[k, h] - far) * LOG2E, acc)
        return jnp.where(bk == MASKED_CODE, NEG_INF, acc)

    near_o[0] = build(nb_ref[...])
    cmp_o[0] = build(cb_ref[...])


def _t5_bucket(dist):
    dist = jnp.maximum(dist, 0)
    max_exact = REL_BUCKETS // 2
    large = max_exact + (jnp.log(jnp.maximum(dist, 1).astype(F32) / max_exact)
                         / math.log(REL_MAX_DIST / max_exact) * (REL_BUCKETS - max_exact)).astype(jnp.int32)
    large = jnp.minimum(large, REL_BUCKETS - 1)
    return jnp.where(dist < max_exact, dist, large)


def _bias_tables(rel_bias, T, tq):
    nc = T // CMP_STRIDE
    i = jnp.arange(tq, dtype=jnp.int32)
    d = jnp.asarray(NEAR_OFFSETS, jnp.int32)[:, None, None] * tq + i[None, None, :] - i[None, :, None]
    near_b = jnp.where((d < 0) | (d >= WINDOW), MASKED_CODE, _t5_bucket(d))
    near_b = jnp.concatenate([near_b, jnp.full((1, tq, tq), MASKED_CODE, jnp.int32)], axis=0)
    cmp_end = jnp.arange(nc, dtype=jnp.int32) * CMP_STRIDE + CMP_LEN - 1
    cmp_b = _t5_bucket(jnp.arange(T, dtype=jnp.int32)[None, :] - cmp_end[:, None])
    return pl.pallas_call(
        _bias_kernel,
        grid=(NSA_HEADS,),
        in_specs=[pl.BlockSpec(memory_space=pltpu.SMEM),
                  pl.BlockSpec(near_b.shape, lambda h: (0, 0, 0)),
                  pl.BlockSpec(cmp_b.shape, lambda h: (0, 0))],
        out_specs=(pl.BlockSpec((1,) + near_b.shape, lambda h: (h, 0, 0, 0)),
                   pl.BlockSpec((1,) + cmp_b.shape, lambda h: (h, 0, 0))),
        out_shape=(jax.ShapeDtypeStruct((NSA_HEADS,) + near_b.shape, F32),
                   jax.ShapeDtypeStruct((NSA_HEADS,) + cmp_b.shape, F32)),
        name="rel_bias_tables",
    )(rel_bias, near_b, cmp_b)


def _nsa_kernel(q_ref, kc_ref, vct_ref, ks_ref, vst_ref, kw_ref, vwt_ref, gt_ref,
                near_ref, cb_ref, ovl_ref, o_ref,
                qa_scr, sa_scr, sb_scr, cma_scr, cmb_scr, ms_scr, mw_scr, accs_scr, accw_scr, *, tq, sel_k):
    HG = NSA_HG
    NC = kc_ref.shape[2]
    NS = ovl_ref.shape[0]
    UW = tq
    ti = pl.program_id(2)
    s0 = ti * tq

    def head(x, h):
        return x[:, h * tq:(h + 1) * tq]

    def keys(ref, u, n=1):
        return ref[0, 0, pl.ds(pl.multiple_of(u * UW, UW), n * UW), :]

    def values_t(ref, u, n=1):
        return ref[0, 0, :, pl.ds(pl.multiple_of(u * UW, UW), n * UW)]

    cmax = lambda s_ref: cma_scr if s_ref is sa_scr else cmb_scr

    def score(s_ref, k_ref, u, n, queries, entries=None):
        def produce(h):
            s = _dot_nt(keys(k_ref, u, n), queries(h))
            if entries is not None:
                s = jnp.concatenate([s[i * UW:(i + 1) * UW] + near_ref[h, e] for i, e in enumerate(entries)], axis=0)
            s_ref[0:n * UW, h * tq:(h + 1) * tq] = s
            groups = [s[r * SUBLANES:(r + 1) * SUBLANES] for r in range(n * UW // SUBLANES)]
            cmax(s_ref)[:, h * tq:(h + 1) * tq] = functools.reduce(jnp.maximum, groups)
        return produce

    def fold(s_ref, n, m_ref, vt_ref, u, acc_ref, first=False):
        def consume(h):
            c = slice(h * tq, (h + 1) * tq)
            m_new = jnp.max(cmax(s_ref)[:, c], axis=0, keepdims=True)
            if not first:
                m_old = m_ref[0:1, c]
                m_new = jnp.maximum(m_old, m_new)
            part = [_dot(values_t(vt_ref, u + i), jnp.exp2((s_ref[i * UW:(i + 1) * UW, c] - m_new).astype(BF16)))
                    for i in range(n)]
            part = functools.reduce(jnp.add, part)
            acc_ref[:, c] = part if first else jnp.exp2(m_old - m_new) * acc_ref[:, c] + part
            m_ref[:, c] = jnp.broadcast_to(m_new, (SUBLANES, tq))
        return consume

    def stage(produce=None, consume=None):
        for h in range(HG):
            if produce is not None:
                produce(h)
            if consume is not None:
                consume(h)

    q_head = lambda h: q_ref[0, 0, h]
    qa_head = lambda h: qa_scr[h * tq:(h + 1) * tq, :]

    def normalized(acc_ref):
        acc = acc_ref[...]
        return acc[:NSA_HD] / acc[NSA_HD:NSA_HD + 1]

    masked = len(NEAR_OFFSETS)
    w0 = jnp.maximum(ti - 2, 0)

    def window_entry(slot):
        back = jnp.minimum(ti, 2) - slot
        return jnp.where(back >= 0, back, masked)

    win_bufs = (sa_scr, sb_scr, sa_scr)
    win_slots = (1, 0, 2)
    win_score = [score(buf, kw_ref, w0 + slot, 1, q_head, (window_entry(slot),))
                 for buf, slot in zip(win_bufs, win_slots)]
    win_fold = [fold(buf, 1, mw_scr, vwt_ref, w0 + slot, accw_scr, first=(j == 0))
                for j, (buf, slot) in enumerate(zip(win_bufs, win_slots))]

    stage(win_score[0])

    pos_c = s0 + lax.broadcasted_iota(jnp.int32, (NC, tq), 1)
    cend = lax.broadcasted_iota(jnp.int32, (NC, tq), 0) * CMP_STRIDE + (CMP_LEN - 1)
    valid_c = cend <= pos_c
    valid_f = valid_c.astype(F32)
    tile_cols = pl.ds(pl.multiple_of(s0, tq), tq)
    lc = [_dot_nt(kc_ref[0, 0], q_head(h)) + cb_ref[h, :, tile_cols] for h in range(HG)]
    pc = []
    for h in range(HG):
        z = jnp.where(valid_c, lc[h], NEG_INF)
        e = jnp.exp2(z - jnp.max(z, 0, keepdims=True)) * valid_f
        pc.append(e / jnp.maximum(jnp.sum(e, 0, keepdims=True), 1e-30))
    oc = [_dot(vct_ref[0, 0], pc[h].astype(BF16)) for h in range(HG)]
    psum = functools.reduce(jnp.add, pc)

    p_hi = psum.astype(BF16)
    r1 = psum - p_hi.astype(F32)
    p_mid = r1.astype(BF16)
    p_lo = (r1 - p_mid.astype(F32)).astype(BF16)
    ovl = ovl_ref[...]
    imp = _dot(ovl, p_hi) + _dot(ovl, p_mid) + _dot(ovl, p_lo)
    jrow = lax.broadcasted_iota(jnp.int32, (NS, tq), 0)
    posl = s0 + lax.broadcasted_iota(jnp.int32, (NS, tq), 1)
    cur = lax.shift_right_logical(posl, int(math.log2(SEL_BLOCK)))
    forced = (jrow == 0) | (jrow == cur) | (jrow == cur - 1)
    started = jrow * SEL_BLOCK <= posl
    imp = jnp.where(forced, FORCED_SCORE, imp)
    imp = jnp.where(started, imp, NEG_INF)
    sel = jnp.zeros((NS, tq), F32)
    for it in range(max(sel_k, HG)):
        if it < sel_k:
            mx = jnp.max(imp, axis=0, keepdims=True)
            first = jnp.min(jnp.where(imp == mx, jrow, NS), axis=0, keepdims=True)
            hit = jrow == first
            sel = jnp.where(hit, 1.0, sel)
            imp = jnp.where(hit, PICKED, imp)
        if it < HG:
            win_score[1](it)
            win_fold[0](it)
    stage(win_score[2], win_fold[1])
    notsel = 1.0 - jnp.where(started, sel, 0.0)
    notsel = jnp.concatenate([notsel, jnp.zeros((LANES - NS, tq), F32)], axis=0)
    notsel = pltpu.roll(notsel.T, NSA_HD, 1).astype(BF16)
    for hg in range(HG):
        qa_scr[hg * tq:(hg + 1) * tq, :NSA_HD] = q_ref[0, 0, hg]
        qa_scr[hg * tq:(hg + 1) * tq, NSA_HD:] = notsel[:, NSA_HD:]

    prev_u = jnp.where(ti >= 1, ti - 1, 1)
    sel_job = lambda s_ref, u, n: (s_ref, n, ms_scr, vst_ref, u, accs_scr)
    stage(score(sb_scr, ks_ref, ti, 1, qa_head, (0,)), win_fold[2])
    stage(score(sa_scr, ks_ref, prev_u, 1, qa_head, (jnp.where(ti >= 1, 1, masked),)), fold(*sel_job(sb_scr, ti, 1), first=True))
    stage(None, fold(*sel_job(sa_scr, prev_u, 1)))

    n_plain = jnp.maximum(ti - 1, 0)
    n_quads = lax.shift_right_logical(n_plain, 2)

    def plain_run(first, count):
        bufs = (sa_scr, sb_scr)
        for i in range(count + 1):
            produce = score(bufs[i % 2], ks_ref, first + i, 1, qa_head) if i < count else None
            consume = fold(*sel_job(bufs[(i - 1) % 2], first + i - 1, 1)) if i > 0 else None
            stage(produce, consume)

    def plain_quad(j, carry):
        plain_run(4 * j, 4)
        return carry

    lax.fori_loop(0, n_quads, plain_quad, 0)
    left = n_plain - 4 * n_quads

    @pl.when(left >= 2)
    def _():
        plain_run(4 * n_quads, 2)

    @pl.when(lax.rem(left, 2) == 1)
    def _():
        stage(score(sa_scr, ks_ref, n_plain - 1, 1, qa_head))
        stage(None, fold(*sel_job(sa_scr, n_plain - 1, 1)))

    gt = gt_ref[0, 0]
    o_s = normalized(accs_scr)
    o_w = normalized(accw_scr)

    def gated(h):
        return (gt[3 * h:3 * h + 1] * oc[h] + gt[3 * h + 1:3 * h + 2] * head(o_s, h)
                + gt[3 * h + 2:3 * h + 3] * head(o_w, h))

    pairs = [jnp.concatenate([gated(2 * j), gated(2 * j + 1)], axis=0).T for j in range(HG // 2)]
    o_ref[0] = jnp.concatenate(pairs, axis=1).astype(BF16)


def _nsa_attention(qn, kc, vct, ks, vst, kw, vwt, gates_t, near, cmpb):
    B, G, HG, T, hd = qn.shape
    tq = TQ
    assert T % tq == 0 and T >= 3 * tq and WINDOW == 2 * tq and tq >= 2 * REL_MAX_DIST and tq % LANES == 0
    nq = T // tq
    NC = kc.shape[2]
    NS = T // SEL_BLOCK
    sel_k = min(SEL_TOPK, NS)
    n = np.arange(NC)
    j = np.arange(NS)
    ovl = ((n[None, :] * CMP_STRIDE < (j[:, None] + 1) * SEL_BLOCK)
           & (n[None, :] * CMP_STRIDE + CMP_LEN - 1 >= j[:, None] * SEL_BLOCK) & (n[None, :] < NC - 1))
    ovl = jnp.asarray(ovl, BF16)
    seq = lambda rows, width: pl.BlockSpec((1, 1, rows, width), lambda g, b, t: (b, g, 0, 0))
    R = HG * tq
    W = HG * hd
    return pl.pallas_call(
        functools.partial(_nsa_kernel, tq=tq, sel_k=sel_k),
        grid=(G, B, nq),
        in_specs=[pl.BlockSpec((1, 1, HG, tq, hd), lambda g, b, t: (b, g, 0, t, 0)),
                  seq(NC, hd), seq(hd, NC), seq(T, LANES), seq(NSA_VT_ROWS, T), seq(T, hd), seq(NSA_VT_ROWS, T),
                  pl.BlockSpec((1, 1, LANES, tq), lambda g, b, t: (b, g, 0, t)),
                  pl.BlockSpec((HG, len(NEAR_OFFSETS) + 1, tq, tq), lambda g, b, t: (g, 0, 0, 0),
                               pipeline_mode=pl.Buffered(1)),
                  pl.BlockSpec((HG, NC, T), lambda g, b, t: (g, 0, 0), pipeline_mode=pl.Buffered(1)),
                  pl.BlockSpec(ovl.shape, lambda g, b, t: (0, 0))],
        out_specs=pl.BlockSpec((1, tq, W), lambda g, b, t: (b, t, g)),
        out_shape=jax.ShapeDtypeStruct((B, T, G * W), BF16),
        scratch_shapes=[pltpu.VMEM((R, LANES), BF16),
                        pltpu.VMEM((2 * tq, R), F32), pltpu.VMEM((2 * tq, R), F32),
                        pltpu.VMEM((SUBLANES, R), F32), pltpu.VMEM((SUBLANES, R), F32),
                        pltpu.VMEM((SUBLANES, R), F32), pltpu.VMEM((SUBLANES, R), F32),
                        pltpu.VMEM((NSA_VT_ROWS, R), F32), pltpu.VMEM((NSA_VT_ROWS, R), F32)],
        compiler_params=pltpu.CompilerParams(vmem_limit_bytes=NSA_VMEM_LIMIT),
        name="nsa_attention",
    )(qn, kc, vct, ks, vst, kw, vwt, gates_t, near, cmpb, ovl)


def _merge_kernel(x_ref, oret_ref, on_ref, ma_ref, mb_ref, wr_ref, wn_ref, wo_ref, o_ref):
    y_ret = _dot(oret_ref[...], wr_ref[...])
    y_nsa = _dot(on_ref[...], wn_ref[...])
    mixed = ma_ref[...].astype(F32) * y_ret + mb_ref[...].astype(F32) * y_nsa
    o_ref[...] = x_ref[...] + _dot(mixed.astype(BF16), wo_ref[...])


def _merge(x2, oret, on, ma, mb, w_o_ret, w_o_nsa, w_out):
    M, D = x2.shape
    tm = min(TM_MERGE, M)
    row = lambda width: pl.BlockSpec((tm, width), lambda i: (i, 0))
    res = lambda a: pl.BlockSpec(a.shape, lambda i: (0, 0), pipeline_mode=pl.Buffered(1))
    w = [a.astype(BF16) for a in (w_o_ret, w_o_nsa, w_out)]
    return pl.pallas_call(
        _merge_kernel,
        grid=(M // tm,),
        in_specs=[row(D), row(oret.shape[1]), row(D), row(D), row(D)] + [res(a) for a in w],
        out_specs=row(D),
        out_shape=jax.ShapeDtypeStruct((M, D), F32),
        compiler_params=pltpu.CompilerParams(vmem_limit_bytes=48 * MIB),
        name="merge_out_projection",
    )(x2, oret, on, ma, mb, *w)


def _rms(x, g):
    return x * lax.rsqrt(jnp.mean(x * x, -1, keepdims=True) + EPS) * g


def _ffn_kernel(x_ref, g_ref, wi_ref, wo_ref, gf_ref, o_ref, h_scr, acc_scr, *, final_norm):
    x = x_ref[...]
    h_scr[...] = _rms(x, g_ref[...]).astype(BF16)
    acc_scr[...] = x
    cw = FFN_HIDDEN // 2
    for c in range(2):
        a = _dot(h_scr[...], wi_ref[:, c * cw:(c + 1) * cw])
        b = _dot(h_scr[...], wi_ref[:, FFN_HIDDEN + c * cw:FFN_HIDDEN + (c + 1) * cw])
        acc_scr[...] += _dot((_silu(a) * b).astype(BF16), wo_ref[c * cw:(c + 1) * cw, :])
    y = acc_scr[...]
    o_ref[...] = _rms(y, gf_ref[...]) if final_norm else y


def _ffn(x2, g, w_in, w_out, g_final, final_norm):
    M, D = x2.shape
    tm = min(TM_MERGE, M)
    row = pl.BlockSpec((tm, D), lambda i: (i, 0))
    vec = pl.BlockSpec((1, D), lambda i: (0, 0))
    res = lambda a: pl.BlockSpec(a.shape, lambda i: (0, 0), pipeline_mode=pl.Buffered(1))
    wi, wo = w_in.astype(BF16), w_out.astype(BF16)
    return pl.pallas_call(
        functools.partial(_ffn_kernel, final_norm=final_norm),
        grid=(M // tm,),
        in_specs=[row, vec, res(wi), res(wo), vec],
        out_specs=row,
        out_shape=jax.ShapeDtypeStruct((M, D), F32),
        scratch_shapes=[pltpu.VMEM((tm, D), BF16), pltpu.VMEM((tm, D), F32)],
        compiler_params=pltpu.CompilerParams(vmem_limit_bytes=52 * MIB),
        name="swiglu_ffn",
    )(x2, g.reshape(1, D), wi, wo, g_final.reshape(1, D))


def kernel(x, norm_mix_g, w_in, cmp_pos_k, cmp_pos_v, cmp_w1_k, cmp_w2_k, cmp_w1_v, cmp_w2_v, w_o_ret, w_o_nsa,
           w_out, norm_ffn_g, w_ffn_in, w_ffn_out, rel_bias, norm_final_g):
    B, T, D = x.shape
    depth = w_in.shape[0]
    G, HG = NSA_KV_GROUPS, NSA_HG
    half = RET_DK // 2
    freqs = ROPE_BASE ** (-jnp.arange(half, dtype=F32) / half)
    ang = jnp.arange(T, dtype=jnp.int32).astype(F32)[:, None] * freqs
    cos, sin = jnp.cos(ang), jnp.sin(ang)
    near, cmpb = _bias_tables(rel_bias, T, TQ)
    for i in range(depth):
        qr, kr, vr, gr, qn, kvc16, ks, vst, kw, vwt, gates_t, ma, mb = _in_projection(
            x, norm_mix_g[i], _pack_w_in(w_in[i]), cos, sin)
        o_ret = _retention(qr, kr, vr, gr)
        kc, vct = _compress(kvc16,cmp_pos_k[i], cmp_pos_v[i], cmp_w1_k[i], cmp_w2_k[i], cmp_w1_v[i], cmp_w2_v[i])
        o_n = _nsa_attention(qn.reshape(B, G, HG, T, NSA_HD), kc, vct, ks, vst, kw, vwt, gates_t, near, cmpb)
        o_n = o_n.reshape(B * T, NSA_HEADS * NSA_HD)
        x2 = _merge(x.reshape(B * T, D), o_ret.reshape(B * T, -1), o_n, ma.reshape(B * T, D), mb.reshape(B * T, D),
                    w_o_ret[i], w_o_nsa[i], w_out[i])
        x2 = _ffn(x2, norm_ffn_g[i], w_ffn_in[i], w_ffn_out[i], norm_final_g, final_norm=(i == depth - 1))
        x = x2.reshape(B, T, D)
    return x
```

```python
import functools
import math

import jax
import jax.numpy as jnp
import numpy as np
from jax import lax
from jax.experimental import pallas as pl
from jax.experimental.pallas import tpu as pltpu

F32 = jnp.float32
BF16 = jnp.bfloat16

D_MODEL = 1024
RET_HEADS = 4
RET_DK = 256
RET_DV = 512
RET_BLOCK = 256
ROPE_BASE = 10000.0
GN_EPS = 1e-5
NSA_HEADS = 16
NSA_KV_GROUPS = 2
NSA_HG = NSA_HEADS // NSA_KV_GROUPS
NSA_HD = 64
CMP_LEN = 32
CMP_STRIDE = 16
SEL_BLOCK = 64
SEL_TOPK = 8
WINDOW = 512
FORCED_SCORE = 1e4
REL_BUCKETS = 32
REL_MAX_DIST = 128
FFN_HIDDEN = -(-8 * D_MODEL // (3 * 256)) * 256
EPS = 1e-6
NEG_INF = -1e30
PICKED = -3e38
MASKED_CODE = REL_BUCKETS
NEAR_OFFSETS = (0, 1, 2)
LOG2E = math.log2(math.e)

LANES = 128
SUBLANES = 8
MIB = 1024 * 1024
NSA_VMEM_LIMIT = 56 * MIB
NSA_VT_ROWS = NSA_HD + 16

_SEG = np.cumsum([0, RET_HEADS * RET_DK, RET_HEADS * RET_DK, RET_HEADS * RET_DV, RET_HEADS * RET_DV,
                  NSA_HEADS * NSA_HD, 6 * NSA_KV_GROUPS * NSA_HD, 3 * NSA_HEADS, D_MODEL, D_MODEL])
N_QR, N_KR, N_VR, N_GR = RET_HEADS * RET_DK, RET_HEADS * RET_DK, RET_HEADS * RET_DV, RET_HEADS * RET_DV
N_QN, N_KV = NSA_HEADS * NSA_HD, 6 * NSA_KV_GROUPS * NSA_HD
N_GATE = NSA_KV_GROUPS * LANES
_P = np.cumsum([0, N_QR, N_KR, N_VR, N_GR, N_QN, N_KV, N_GATE, D_MODEL, D_MODEL])
P_QR, P_KR, P_VR, P_GR, P_QN, P_KV, P_GATE, P_MA, P_MB, P_END = [int(v) for v in _P]

TM_PROJ = 512
TM_MIX_FFN = 512
TQ = 256
NT = (((1,), (1,)), ((), ()))


def _dot(a, b):
    return jnp.dot(a, b, preferred_element_type=F32)


def _dot_nt(a, b):
    return lax.dot_general(a, b, NT, preferred_element_type=F32)


def _sigmoid(x):
    return 1.0 / (1.0 + jnp.exp(-x))


def _silu(x):
    return x * _sigmoid(x)


def _inproj_kernel(x_ref, g_ref, cos_ref, sin_ref, feat_ref, w_ref,
                   qr_o, kr_o, vr_o, gr_o, qn_o, kvc_o, ks_o, vst_o, kw_o, vwt_o, gate_o, ma_o, mb_o, h_scr, kvc_scr):
    x = x_ref[0]
    h = x * lax.rsqrt(jnp.mean(x * x, -1, keepdims=True) + EPS) * g_ref[...]
    h_scr[...] = h.astype(BF16)
    cos = cos_ref[...]
    sin = sin_ref[...]
    half = RET_DK // 2

    def proj(lo, width):
        return _dot(h_scr[...], w_ref[:, lo:lo + width])

    y32 = proj(P_KV, N_KV)
    y = y32.astype(BF16)
    y_t = y32.T.astype(BF16)
    tm = y.shape[0]
    ones_row = (lax.broadcasted_iota(jnp.int32, (NSA_VT_ROWS - NSA_HD, tm), 0) == 0).astype(BF16)
    for g in range(NSA_KV_GROUPS):
        lo = lambda j: (j * NSA_KV_GROUPS + g) * NSA_HD
        ks_o[0, g, :, :NSA_HD] = y[:, lo(2):lo(2) + NSA_HD]
        ks_o[0, g, :, NSA_HD:] = feat_ref[:, NSA_HD:]
        kw_o[0, g] = y[:, lo(4):lo(4) + NSA_HD]
        for out, j in ((vst_o, 3), (vwt_o, 5)):
            out[0, g, :NSA_HD, :] = y_t[lo(j):lo(j) + NSA_HD, :]
            out[0, g, NSA_HD:, :] = ones_row
    for s in range(2):
        kvc_scr[s] = y32[:, s * LANES:(s + 1) * LANES]
        for l in range(CMP_STRIDE):
            rows = kvc_scr[s, pl.ds(l, tm // CMP_STRIDE, stride=CMP_STRIDE), :].astype(BF16)
            for g in range(NSA_KV_GROUPS):
                kvc_o[s, 0, g, :, l * NSA_HD:(l + 1) * NSA_HD] = rows[:, g * NSA_HD:(g + 1) * NSA_HD]
    gates_t = _sigmoid(proj(P_GATE, N_GATE)).T
    for g in range(NSA_KV_GROUPS):
        gate_o[0, g] = gates_t[g * LANES:(g + 1) * LANES, :]
    cw = 512
    per = cw // NSA_HD
    for c in range(N_QN // cw):
        y = (proj(P_QN + c * cw, cw) * (NSA_HD ** -0.5 * LOG2E)).astype(BF16)
        for j in range(per):
            qn_o[0, c * per + j] = y[:, j * NSA_HD:(j + 1) * NSA_HD]
    for hd in range(RET_HEADS):
        for out, base, scale in ((qr_o, P_QR, 1.0), (kr_o, P_KR, RET_DK ** -0.5)):
            y = proj(base + hd * RET_DK, RET_DK)
            x1, x2 = y[:, :half], y[:, half:]
            out[0, :, hd * RET_DK:hd * RET_DK + half] = ((x1 * cos - x2 * sin) * scale).astype(BF16)
            out[0, :, hd * RET_DK + half:(hd + 1) * RET_DK] = ((x1 * sin + x2 * cos) * scale).astype(BF16)
    for c in range(N_VR // cw):
        vr_o[0, :, c * cw:(c + 1) * cw] = proj(P_VR + c * cw, cw).astype(BF16)
        gr_o[0, :, c * cw:(c + 1) * cw] = _silu(proj(P_GR + c * cw, cw)).astype(BF16)
    for c in range(D_MODEL // cw):
        ma_o[0, :, c * cw:(c + 1) * cw] = _sigmoid(proj(P_MA + c * cw, cw)).astype(BF16)
        mb_o[0, :, c * cw:(c + 1) * cw] = _sigmoid(proj(P_MB + c * cw, cw)).astype(BF16)


def _pack_w_in(w):
    s = [int(v) for v in _SEG]
    gate = w[:, s[6]:s[7]].reshape(D_MODEL, NSA_KV_GROUPS, 3 * NSA_HG)
    gate = jnp.pad(gate, ((0, 0), (0, 0), (0, LANES - 3 * NSA_HG))).reshape(D_MODEL, N_GATE)
    return jnp.concatenate([w[:, s[0]:s[6]], gate, w[:, s[7]:s[9]]], axis=1).astype(BF16)


def _in_projection(x, g, w_packed, cos, sin):
    B, T, D = x.shape
    G = NSA_KV_GROUPS
    tm = min(TM_PROJ, T)
    nt = T // tm
    assert T // SEL_BLOCK <= LANES - NSA_HD
    own_block = (np.arange(T)[:, None] // SEL_BLOCK) == (np.arange(LANES)[None, :] - NSA_HD)
    feat = jnp.asarray(np.where(own_block, NEG_INF, 0.0), BF16)
    row = lambda width: pl.BlockSpec((1, tm, width), lambda b, t: (b, t, 0))
    grp = lambda width: pl.BlockSpec((1, G, tm, width), lambda b, t: (b, 0, t, 0))
    grp_t = lambda rows: pl.BlockSpec((1, G, rows, tm), lambda b, t: (b, 0, 0, t))
    grp_shape = lambda width: jax.ShapeDtypeStruct((B, G, T, width), BF16)
    grp_t_shape = lambda rows, dt: jax.ShapeDtypeStruct((B, G, rows, T), dt)
    out_shape = (
        jax.ShapeDtypeStruct((B, T, N_QR), BF16), jax.ShapeDtypeStruct((B, T, N_KR), BF16),
        jax.ShapeDtypeStruct((B, T, N_VR), BF16), jax.ShapeDtypeStruct((B, T, N_GR), BF16),
        jax.ShapeDtypeStruct((B, NSA_HEADS, T, NSA_HD), BF16),
        jax.ShapeDtypeStruct((2, B, G, T // CMP_STRIDE, CMP_STRIDE * NSA_HD), BF16),
        grp_shape(LANES), grp_t_shape(NSA_VT_ROWS, BF16), grp_shape(NSA_HD), grp_t_shape(NSA_VT_ROWS, BF16),
        grp_t_shape(LANES, F32),
        jax.ShapeDtypeStruct((B, T, D_MODEL), BF16), jax.ShapeDtypeStruct((B, T, D_MODEL), BF16))
    out_specs = (
        row(N_QR), row(N_KR), row(N_VR), row(N_GR),
        pl.BlockSpec((1, NSA_HEADS, tm, NSA_HD), lambda b, t: (b, 0, t, 0)),
        pl.BlockSpec((2, 1, G, tm // CMP_STRIDE, CMP_STRIDE * NSA_HD), lambda b, t: (0, b, 0, t, 0)),
        grp(LANES), grp_t(NSA_VT_ROWS), grp(NSA_HD), grp_t(NSA_VT_ROWS),
        grp_t(LANES),
        row(D_MODEL), row(D_MODEL))
    out_bytes = tm * (2 * (N_QR + N_KR + N_VR + N_GR + N_QN + 2 * D_MODEL) + 2 * G * (3 * NSA_HD + 3 * LANES)
                      + 4 * N_GATE)
    vmem = 2 * out_bytes + 2 * tm * D * 4 + 2 * D * P_END + tm * D * 2 + 10 * MIB
    return pl.pallas_call(
        _inproj_kernel,
        grid=(B, nt),
        in_specs=[pl.BlockSpec((1, tm, D), lambda b, t: (b, t, 0)),
                  pl.BlockSpec((1, D), lambda b, t: (0, 0)),
                  pl.BlockSpec((tm, RET_DK // 2), lambda b, t: (t, 0)),
                  pl.BlockSpec((tm, RET_DK // 2), lambda b, t: (t, 0)),
                  pl.BlockSpec((tm, LANES), lambda b, t: (t, 0)),
                  pl.BlockSpec((D, P_END), lambda b, t: (0, 0), pipeline_mode=pl.Buffered(1))],
        out_specs=out_specs,
        out_shape=out_shape,
        scratch_shapes=[pltpu.VMEM((tm, D), BF16), pltpu.VMEM((2, tm, G * NSA_HD), F32)],
        compiler_params=pltpu.CompilerParams(vmem_limit_bytes=int(vmem)),
        name="in_projection",
    )(x, g.reshape(1, D), cos, sin, feat, w_packed)


def _retention_kernel(q_ref, k_ref, v_ref, g_ref, dmask_ref, xi_ref, zeta_ref, gch_ref, o_ref):
    C = RET_BLOCK
    nc = q_ref.shape[1] // C
    dmask = dmask_ref[0]
    xi = xi_ref[0]
    zeta = zeta_ref[0]
    gch = gch_ref[0, :, 0:1]
    r = jnp.zeros((RET_DK, RET_DV), F32)
    for c in range(nc):
        rows = slice(c * C, (c + 1) * C)
        qc = q_ref[0, rows, :]
        kc = k_ref[0, rows, :]
        vc = v_ref[0, rows, :]
        kz = (kc.astype(F32) * zeta).astype(BF16)
        r_next = gch * r + lax.dot_general(kz, vc, (((0,), (0,)), ((), ())), preferred_element_type=F32)
        s = _dot_nt(qc, kc) * dmask
        o = _dot(s.astype(BF16), vc) + _dot(qc, r.astype(BF16)) * xi
        r = r_next
        mu = jnp.mean(o, -1, keepdims=True)
        d = o - mu
        var = jnp.mean(d * d, -1, keepdims=True)
        on = d * lax.rsqrt(var + GN_EPS)
        o_ref[0, rows, :] = (g_ref[0, rows, :].astype(F32) * on).astype(BF16)


def _retention(qr, kr, vr, gr):
    B, T, _ = qr.shape
    C = RET_BLOCK
    lg = jnp.log(1.0 - 2.0 ** (-5.0 - jnp.arange(RET_HEADS, dtype=F32)))
    n = jnp.arange(C, dtype=F32)
    diff = n[:, None] - n[None, :]
    dmask = jnp.where(diff >= 0, jnp.exp(jnp.maximum(diff, 0.0)[None] * lg[:, None, None]), 0.0)
    xi = jnp.exp((n + 1.0)[None] * lg[:, None])[:, :, None]
    zeta = jnp.exp((C - 1.0 - n)[None] * lg[:, None])[:, :, None]
    gch = jnp.broadcast_to(jnp.exp(C * lg)[:, None, None], (RET_HEADS, 1, LANES))
    qk_spec = pl.BlockSpec((1, T, RET_DK), lambda b, h: (b, 0, h))
    v_spec = pl.BlockSpec((1, T, RET_DV), lambda b, h: (b, 0, h))
    return pl.pallas_call(
        _retention_kernel,
        grid=(B, RET_HEADS),
        in_specs=[qk_spec, qk_spec, v_spec, v_spec,
                  pl.BlockSpec((1, C, C), lambda b, h: (h, 0, 0)),
                  pl.BlockSpec((1, C, 1), lambda b, h: (h, 0, 0)),
                  pl.BlockSpec((1, C, 1), lambda b, h: (h, 0, 0)),
                  pl.BlockSpec((1, 1, LANES), lambda b, h: (h, 0, 0))],
        out_specs=v_spec,
        out_shape=jax.ShapeDtypeStruct((B, T, RET_HEADS * RET_DV), BF16),
        compiler_params=pltpu.CompilerParams(vmem_limit_bytes=32 * MIB),
        name="retention",
    )(qr, kr, vr, gr, dmask, xi, zeta, gch)


def _gelu_tanh(x):
    return 0.5 * x * (1.0 + jnp.tanh(math.sqrt(2.0 / math.pi) * (x + 0.044715 * (x * x * x))))


def _compress_kernel(x_ref, pk_ref, pv_ref, w1k_ref, w2k_ref, w1v_ref, w2vt_ref, ko_ref, vto_ref):
    half = CMP_STRIDE * NSA_HD

    def hidden(s, g, p_ref, w1_ref):
        x = x_ref[s, 0, g].astype(F32)
        a = _dot((x + p_ref[0:1, :]).astype(BF16), w1_ref[0:half, :])
        b = _dot((x + p_ref[1:2, :]).astype(BF16), w1_ref[half:2 * half, :])
        pre = a + pltpu.roll(b, b.shape[0] - 1, 0)
        return _gelu_tanh(pre).astype(BF16)

    for g in range(NSA_KV_GROUPS):
        ko_ref[0, g] = _dot(hidden(0, g, pk_ref, w1k_ref), w2k_ref[...]).astype(BF16)
        vto_ref[0, g] = _dot_nt(w2vt_ref[...], hidden(1, g, pv_ref, w1v_ref)).astype(BF16)


def _compress(kvc16, pos_k, pos_v, w1k, w2k, w1v, w2v):
    _, B, G, NC, F = kvc16.shape
    full = lambda a: pl.BlockSpec(a.shape, lambda b: (0,) * a.ndim)
    pk = pos_k.reshape(2, F)
    pv = pos_v.reshape(2, F)
    w = [a.astype(BF16) for a in (w1k, w2k, w1v, w2v.T)]
    return pl.pallas_call(
        _compress_kernel,
        grid=(B,),
        in_specs=[pl.BlockSpec((2, 1, G, NC, F), lambda b: (0, b, 0, 0, 0)), full(pk), full(pv)]
        + [full(a) for a in w],
        out_specs=(pl.BlockSpec((1, G, NC, NSA_HD), lambda b: (b, 0, 0, 0)),
                   pl.BlockSpec((1, G, NSA_HD, NC), lambda b: (b, 0, 0, 0))),
        out_shape=(jax.ShapeDtypeStruct((B, G, NC, NSA_HD), BF16), jax.ShapeDtypeStruct((B, G, NSA_HD, NC), BF16)),
        compiler_params=pltpu.CompilerParams(vmem_limit_bytes=32 * MIB),
        name="nsa_compress",
    )(kvc16, pk, pv, *w)


def _bias_kernel(tab_ref, nb_ref, cb_ref, near_o, cmp_o):
    h = pl.program_id(0)
    far = tab_ref[REL_BUCKETS - 1, h]

    def build(bk):
        acc = jnp.zeros(bk.shape, F32)
        for k in range(REL_BUCKETS - 1):
            acc = jnp.where(bk == k, (tab_ref[k, h] - far) * LOG2E, acc)
        return jnp.where(bk == MASKED_CODE, NEG_INF, acc)

    near_o[0] = build(nb_ref[...])
    cmp_o[0] = build(cb_ref[...])


def _t5_bucket(dist):
    dist = jnp.maximum(dist, 0)
    max_exact = REL_BUCKETS // 2
    large = max_exact + (jnp.log(jnp.maximum(dist, 1).astype(F32) / max_exact)
                         / math.log(REL_MAX_DIST / max_exact) * (REL_BUCKETS - max_exact)).astype(jnp.int32)
    large = jnp.minimum(large, REL_BUCKETS - 1)
    return jnp.where(dist < max_exact, dist, large)


def _bias_tables(rel_bias, T, tq):
    nc = T // CMP_STRIDE
    i = jnp.arange(tq, dtype=jnp.int32)
    d = jnp.asarray(NEAR_OFFSETS, jnp.int32)[:, None, None] * tq + i[None, None, :] - i[None, :, None]
    near_b = jnp.where((d < 0) | (d >= WINDOW), MASKED_CODE, _t5_bucket(d))
    near_b = jnp.concatenate([near_b, jnp.full((1, tq, tq), MASKED_CODE, jnp.int32)], axis=0)
    cmp_end = jnp.arange(nc, dtype=jnp.int32) * CMP_STRIDE + CMP_LEN - 1
    cmp_b = _t5_bucket(jnp.arange(T, dtype=jnp.int32)[None, :] - cmp_end[:, None])
    return pl.pallas_call(
        _bias_kernel,
        grid=(NSA_HEADS,),
        in_specs=[pl.BlockSpec(memory_space=pltpu.SMEM),
                  pl.BlockSpec(near_b.shape, lambda h: (0, 0, 0)),
                  pl.BlockSpec(cmp_b.shape, lambda h: (0, 0))],
        out_specs=(pl.BlockSpec((1,) + near_b.shape, lambda h: (h, 0, 0, 0)),
                   pl.BlockSpec((1,) + cmp_b.shape, lambda h: (h, 0, 0))),
        out_shape=(jax.ShapeDtypeStruct((NSA_HEADS,) + near_b.shape, F32),
                   jax.ShapeDtypeStruct((NSA_HEADS,) + cmp_b.shape, F32)),
        name="rel_bias_tables",
    )(rel_bias, near_b, cmp_b)


def _nsa_kernel(q_ref, kc_ref, vct_ref, ks_ref, vst_ref, kw_ref, vwt_ref, gt_ref,
                near_ref, cb_ref, ovl_ref, o_ref,
                qa_scr, sa_scr, sb_scr, cma_scr, cmb_scr, ms_scr, mw_scr, accs_scr, accw_scr, *, tq, sel_k):
    HG = NSA_HG
    NC = kc_ref.shape[2]
    NS = ovl_ref.shape[0]
    UW = tq
    ti = pl.program_id(2)
    s0 = ti * tq

    def head(x, h):
        return x[:, h * tq:(h + 1) * tq]

    def keys(ref, u, n=1):
        return ref[0, 0, pl.ds(pl.multiple_of(u * UW, UW), n * UW), :]

    def values_t(ref, u, n=1):
        return ref[0, 0, :, pl.ds(pl.multiple_of(u * UW, UW), n * UW)]

    cmax = lambda s_ref: cma_scr if s_ref is sa_scr else cmb_scr

    def score(s_ref, k_ref, u, n, queries, entries=None):
        def produce(h):
            s = _dot_nt(keys(k_ref, u, n), queries(h))
            if entries is not None:
                s = jnp.concatenate([s[i * UW:(i + 1) * UW] + near_ref[h, e] for i, e in enumerate(entries)], axis=0)
            s_ref[0:n * UW, h * tq:(h + 1) * tq] = s
            groups = [s[r * SUBLANES:(r + 1) * SUBLANES] for r in range(n * UW // SUBLANES)]
            cmax(s_ref)[:, h * tq:(h + 1) * tq] = functools.reduce(jnp.maximum, groups)
        return produce

    def fold(s_ref, n, m_ref, vt_ref, u, acc_ref, first=False):
        def consume(h):
            c = slice(h * tq, (h + 1) * tq)
            m_new = jnp.max(cmax(s_ref)[:, c], axis=0, keepdims=True)
            if not first:
                m_old = m_ref[0:1, c]
                m_new = jnp.maximum(m_old, m_new)
            part = [_dot(values_t(vt_ref, u + i), jnp.exp2((s_ref[i * UW:(i + 1) * UW, c] - m_new).astype(BF16)))
                    for i in range(n)]
            part = functools.reduce(jnp.add, part)
            acc_ref[:, c] = part if first else jnp.exp2(m_old - m_new) * acc_ref[:, c] + part
            m_ref[:, c] = jnp.broadcast_to(m_new, (SUBLANES, tq))
        return consume

    def stage(produce=None, consume=None):
        for h in range(HG):
            if produce is not None:
                produce(h)
            if consume is not None:
                consume(h)

    q_head = lambda h: q_ref[0, 0, h]
    qa_head = lambda h: qa_scr[h * tq:(h + 1) * tq, :]

    def normalized(acc_ref):
        acc = acc_ref[...]
        return acc[:NSA_HD] / acc[NSA_HD:NSA_HD + 1]

    masked = len(NEAR_OFFSETS)
    w0 = jnp.maximum(ti - 2, 0)

    def window_entry(slot):
        back = jnp.minimum(ti, 2) - slot
        return jnp.where(back >= 0, back, masked)

    win_bufs = (sa_scr, sb_scr, sa_scr)
    win_slots = (1, 0, 2)
    win_score = [score(buf, kw_ref, w0 + slot, 1, q_head, (window_entry(slot),))
                 for buf, slot in zip(win_bufs, win_slots)]
    win_fold = [fold(buf, 1, mw_scr, vwt_ref, w0 + slot, accw_scr, first=(j == 0))
                for j, (buf, slot) in enumerate(zip(win_bufs, win_slots))]

    stage(win_score[0])

    pos_c = s0 + lax.broadcasted_iota(jnp.int32, (NC, tq), 1)
    cend = lax.broadcasted_iota(jnp.int32, (NC, tq), 0) * CMP_STRIDE + (CMP_LEN - 1)
    valid_c = cend <= pos_c
    valid_f = valid_c.astype(F32)
    tile_cols = pl.ds(pl.multiple_of(s0, tq), tq)
    lc = [_dot_nt(kc_ref[0, 0], q_head(h)) + cb_ref[h, :, tile_cols] for h in range(HG)]
    pc = []
    for h in range(HG):
        z = jnp.where(valid_c, lc[h], NEG_INF)
        e = jnp.exp2(z - jnp.max(z, 0, keepdims=True)) * valid_f
        pc.append(e / jnp.maximum(jnp.sum(e, 0, keepdims=True), 1e-30))
    oc = [_dot(vct_ref[0, 0], pc[h].astype(BF16)) for h in range(HG)]
    psum = functools.reduce(jnp.add, pc)

    p_hi = psum.astype(BF16)
    r1 = psum - p_hi.astype(F32)
    p_mid = r1.astype(BF16)
    p_lo = (r1 - p_mid.astype(F32)).astype(BF16)
    ovl = ovl_ref[...]
    imp = _dot(ovl, p_hi) + _dot(ovl, p_mid) + _dot(ovl, p_lo)
    jrow = lax.broadcasted_iota(jnp.int32, (NS, tq), 0)
    posl = s0 + lax.broadcasted_iota(jnp.int32, (NS, tq), 1)
    cur = lax.shift_right_logical(posl, int(math.log2(SEL_BLOCK)))
    forced = (jrow == 0) | (jrow == cur) | (jrow == cur - 1)
    started = jrow * SEL_BLOCK <= posl
    imp = jnp.where(forced, FORCED_SCORE, imp)
    imp = jnp.where(started, imp, NEG_INF)
    sel = jnp.zeros((NS, tq), F32)
    for it in range(max(sel_k, HG)):
        if it < sel_k:
            mx = jnp.max(imp, axis=0, keepdims=True)
            first = jnp.min(jnp.where(imp == mx, jrow, NS), axis=0, keepdims=True)
            hit = jrow == first
            sel = jnp.where(hit, 1.0, sel)
            imp = jnp.where(hit, PICKED, imp)
        if it < HG:
            win_score[1](it)
            win_fold[0](it)
    stage(win_score[2], win_fold[1])
    notsel = 1.0 - jnp.where(started, sel, 0.0)
    notsel = jnp.concatenate([notsel, jnp.zeros((LANES - NS, tq), F32)], axis=0)
    notsel = pltpu.roll(notsel.T, NSA_HD, 1).astype(BF16)
    for hg in range(HG):
        qa_scr[hg * tq:(hg + 1) * tq, :NSA_HD] = q_ref[0, 0, hg]
        qa_scr[hg * tq:(hg + 1) * tq, NSA_HD:] = notsel[:, NSA_HD:]

    prev_u = jnp.where(ti >= 1, ti - 1, 1)
    sel_job = lambda s_ref, u, n: (s_ref, n, ms_scr, vst_ref, u, accs_scr)
    stage(score(sb_scr, ks_ref, ti, 1, qa_head, (0,)), win_fold[2])
    stage(score(sa_scr, ks_ref, prev_u, 1, qa_head, (jnp.where(ti >= 1, 1, masked),)), fold(*sel_job(sb_scr, ti, 1), first=True))
    stage(None, fold(*sel_job(sa_scr, prev_u, 1)))

    n_plain = jnp.maximum(ti - 1, 0)
    n_quads = lax.shift_right_logical(n_plain, 2)

    def plain_run(first, count):
        bufs = (sa_scr, sb_scr)
        for i in range(count + 1):
            produce = score(bufs[i % 2], ks_ref, first + i, 1, qa_head) if i < count else None
            consume = fold(*sel_job(bufs[(i - 1) % 2], first + i - 1, 1)) if i > 0 else None
            stage(produce, consume)

    def plain_quad(j, carry):
        plain_run(4 * j, 4)
        return carry

    lax.fori_loop(0, n_quads, plain_quad, 0)
    left = n_plain - 4 * n_quads

    @pl.when(left >= 2)
    def _():
        plain_run(4 * n_quads, 2)

    @pl.when(lax.rem(left, 2) == 1)
    def _():
        stage(score(sa_scr, ks_ref, n_plain - 1, 1, qa_head))
        stage(None, fold(*sel_job(sa_scr, n_plain - 1, 1)))

    gt = gt_ref[0, 0]
    o_s = normalized(accs_scr)
    o_w = normalized(accw_scr)

    def gated(h):
        return (gt[3 * h:3 * h + 1] * oc[h] + gt[3 * h + 1:3 * h + 2] * head(o_s, h)
                + gt[3 * h + 2:3 * h + 3] * head(o_w, h))

    pairs = [jnp.concatenate([gated(2 * j), gated(2 * j + 1)], axis=0).T for j in range(HG // 2)]
    o_ref[0] = jnp.concatenate(pairs, axis=1).astype(BF16)


def _nsa_attention(qn, kc, vct, ks, vst, kw, vwt, gates_t, near, cmpb):
    B, G, HG, T, hd = qn.shape
    tq = TQ
    assert T % tq == 0 and T >= 3 * tq and WINDOW == 2 * tq and tq >= 2 * REL_MAX_DIST and tq % LANES == 0
    nq = T // tq
    NC = kc.shape[2]
    NS = T // SEL_BLOCK
    sel_k = min(SEL_TOPK, NS)
    n = np.arange(NC)
    j = np.arange(NS)
    ovl = ((n[None, :] * CMP_STRIDE < (j[:, None] + 1) * SEL_BLOCK)
           & (n[None, :] * CMP_STRIDE + CMP_LEN - 1 >= j[:, None] * SEL_BLOCK) & (n[None, :] < NC - 1))
    ovl = jnp.asarray(ovl, BF16)
    seq = lambda rows, width: pl.BlockSpec((1, 1, rows, width), lambda g, b, t: (b, g, 0, 0))
    R = HG * tq
    W = HG * hd
    return pl.pallas_call(
        functools.partial(_nsa_kernel, tq=tq, sel_k=sel_k),
        grid=(G, B, nq),
        in_specs=[pl.BlockSpec((1, 1, HG, tq, hd), lambda g, b, t: (b, g, 0, t, 0)),
                  seq(NC, hd), seq(hd, NC), seq(T, LANES), seq(NSA_VT_ROWS, T), seq(T, hd), seq(NSA_VT_ROWS, T),
                  pl.BlockSpec((1, 1, LANES, tq), lambda g, b, t: (b, g, 0, t)),
                  pl.BlockSpec((HG, len(NEAR_OFFSETS) + 1, tq, tq), lambda g, b, t: (g, 0, 0, 0),
                               pipeline_mode=pl.Buffered(1)),
                  pl.BlockSpec((HG, NC, T), lambda g, b, t: (g, 0, 0), pipeline_mode=pl.Buffered(1)),
                  pl.BlockSpec(ovl.shape, lambda g, b, t: (0, 0))],
        out_specs=pl.BlockSpec((1, tq, W), lambda g, b, t: (b, t, g)),
        out_shape=jax.ShapeDtypeStruct((B, T, G * W), BF16),
        scratch_shapes=[pltpu.VMEM((R, LANES), BF16),
                        pltpu.VMEM((2 * tq, R), F32), pltpu.VMEM((2 * tq, R), F32),
                        pltpu.VMEM((SUBLANES, R), F32), pltpu.VMEM((SUBLANES, R), F32),
                        pltpu.VMEM((SUBLANES, R), F32), pltpu.VMEM((SUBLANES, R), F32),
                        pltpu.VMEM((NSA_VT_ROWS, R), F32), pltpu.VMEM((NSA_VT_ROWS, R), F32)],
        compiler_params=pltpu.CompilerParams(vmem_limit_bytes=NSA_VMEM_LIMIT),
        name="nsa_attention",
    )(qn, kc, vct, ks, vst, kw, vwt, gates_t, near, cmpb, ovl)


def _rms(x, g):
    return x * lax.rsqrt(jnp.mean(x * x, -1, keepdims=True) + EPS) * g


def _mix_ffn_kernel(x_ref, oret_ref, on_ref, ma_ref, mb_ref, wr_ref, wn_ref, wom_ref,
                    g_ref, wi_ref, wo_ref, gf_ref, o_ref, h_scr, acc_scr, *, final_norm):
    y_ret = _dot(oret_ref[...], wr_ref[...])
    y_nsa = _dot(on_ref[...], wn_ref[...])
    mixed = ma_ref[...].astype(F32) * y_ret + mb_ref[...].astype(F32) * y_nsa
    x = x_ref[...] + _dot(mixed.astype(BF16), wom_ref[...])
    h_scr[...] = _rms(x, g_ref[...]).astype(BF16)
    acc_scr[...] = x
    cw = FFN_HIDDEN // 2
    for c in range(2):
        a = _dot(h_scr[...], wi_ref[:, c * cw:(c + 1) * cw])
        b = _dot(h_scr[...], wi_ref[:, FFN_HIDDEN + c * cw:FFN_HIDDEN + (c + 1) * cw])
        acc_scr[...] += _dot((_silu(a) * b).astype(BF16), wo_ref[c * cw:(c + 1) * cw, :])
    y = acc_scr[...]
    o_ref[...] = _rms(y, gf_ref[...]) if final_norm else y


def _mix_ffn(x2, oret, on, ma, mb, w_o_ret, w_o_nsa, w_out, g, w_in, w_ffn_out, g_final, final_norm):
    M, D = x2.shape
    tm = min(TM_MIX_FFN, M)
    row = lambda width: pl.BlockSpec((tm, width), lambda i: (i, 0))
    vec = pl.BlockSpec((1, D), lambda i: (0, 0))
    res = lambda a: pl.BlockSpec(a.shape, lambda i: (0, 0), pipeline_mode=pl.Buffered(1))
    wm = [a.astype(BF16) for a in (w_o_ret, w_o_nsa, w_out)]
    wi, wo = w_in.astype(BF16), w_ffn_out.astype(BF16)
    return pl.pallas_call(
        functools.partial(_mix_ffn_kernel, final_norm=final_norm),
        grid=(M // tm,),
        in_specs=[row(D), row(oret.shape[1]), row(D), row(D), row(D)] + [res(a) for a in wm]
        + [vec, res(wi), res(wo), vec],
        out_specs=row(D),
        out_shape=jax.ShapeDtypeStruct((M, D), F32),
        scratch_shapes=[pltpu.VMEM((tm, D), BF16), pltpu.VMEM((tm, D), F32)],
        compiler_params=pltpu.CompilerParams(vmem_limit_bytes=56 * MIB),
        name="mix_ffn",
    )(x2, oret, on, ma, mb, *wm, g.reshape(1, D), wi, wo, g_final.reshape(1, D))


def kernel(x, norm_mix_g, w_in, cmp_pos_k, cmp_pos_v, cmp_w1_k, cmp_w2_k, cmp_w1_v, cmp_w2_v, w_o_ret, w_o_nsa,
           w_out, norm_ffn_g, w_ffn_in, w_ffn_out, rel_bias, norm_final_g):
    B, T, D = x.shape
    depth = w_in.shape[0]
    G, HG = NSA_KV_GROUPS, NSA_HG
    half = RET_DK // 2
    freqs = ROPE_BASE ** (-jnp.arange(half, dtype=F32) / half)
    ang = jnp.arange(T, dtype=jnp.int32).astype(F32)[:, None] * freqs
    cos, sin = jnp.cos(ang), jnp.sin(ang)
    near, cmpb = _bias_tables(rel_bias, T, TQ)
    for i in range(depth):
        qr, kr, vr, gr, qn, kvc16, ks, vst, kw, vwt, gates_t, ma, mb = _in_projection(
            x, norm_mix_g[i], _pack_w_in(w_in[i]), cos, sin)
        o_ret = _retention(qr, kr, vr, gr)
        kc, vct = _compress(kvc16,cmp_pos_k[i], cmp_pos_v[i], cmp_w1_k[i], cmp_w2_k[i], cmp_w1_v[i], cmp_w2_v[i])
        o_n = _nsa_attention(qn.reshape(B, G, HG, T, NSA_HD), kc, vct, ks, vst, kw, vwt, gates_t, near, cmpb)
        o_n = o_n.reshape(B * T, NSA_HEADS * NSA_HD)
        x2 = _mix_ffn(x.reshape(B * T, D), o_ret.reshape(B * T, -1), o_n, ma.reshape(B * T, D), mb.reshape(B * T, D),
                      w_o_ret[i], w_o_nsa[i], w_out[i], norm_ffn_g[i], w_ffn_in[i], w_ffn_out[i], norm_final_g,
                      final_norm=(i == depth - 1))
        x = x2.reshape(B, T, D)
    return x
```
